```python
import math
import jax, jax.numpy as jnp
from jax import lax
import numpy as np

D_MODEL = 2048
BATCH = 8
SEQ = 2048
DEPTH = 1

SGU_WIDTH = D_MODEL // 2
SGU_CHUNK = 128
SGU_GROUPS = 8
SGU_GROUP_DIM = SGU_WIDTH // SGU_GROUPS
HY_WIDTH = D_MODEL // 2
HY_ORDER = 2
HY_SHORT = 3
HY_EMB = 33
HY_BANDS = (HY_EMB - 1) // 2
HY_FILTER_HIDDEN = 64
HY_DIRS = 2
HY_DECAY_TARGET = 1e-2
HY_FAST_DECAY = 0.3
HY_SLOW_DECAY = 1.5
HY_FILTER_INIT_SCALE = 0.05
N_BRANCHES = 2
D_FF = 4 * D_MODEL
IN_WIDTH = 2 * SGU_WIDTH + (HY_ORDER + 1) * HY_WIDTH + N_BRANCHES * D_MODEL
N_MOD = 6
DEEPNORM_ALPHA = (2.0 * DEPTH) ** 0.25
DEEPNORM_BETA = (8.0 * DEPTH) ** -0.25
LN_EPS = 1e-5

kernel_name = "hybrid_gmlp_hyena_deepnorm_adaln_block"


def layer_norm(x, g, b):
    xf = x.astype(jnp.float32)
    mu = jnp.mean(xf, axis=-1, keepdims=True)
    var = jnp.mean(jnp.square(xf - mu), axis=-1, keepdims=True)
    y = (xf - mu) * lax.rsqrt(var + LN_EPS) * g.astype(jnp.float32) + b.astype(jnp.float32)
    return y.astype(x.dtype)


def spatial_gating(u, v, ln_g, ln_b, w_s, b_s):
    bsz, s, _ = v.shape
    n_chunks = s // SGU_CHUNK
    vn = layer_norm(v, ln_g, ln_b).reshape(bsz, n_chunks, SGU_CHUNK, SGU_GROUPS, SGU_GROUP_DIM)
    mixed = jnp.einsum('gqp,bnpgc->bnqgc', w_s, vn) + b_s.T[None, None, :, :, None]
    return u * mixed.reshape(bsz, s, SGU_WIDTH)


def short_conv(z, w, b):
    s = z.shape[1]
    pad = HY_SHORT // 2
    zp = jnp.pad(z, ((0, 0), (pad, HY_SHORT - 1 - pad), (0, 0)))
    return sum(w[j] * zp[:, j:j + s] for j in range(HY_SHORT)) + b


def hyena_filters(L, w1, b1, w2, b2, freq, w3):
    f32 = jnp.float32
    t = jnp.linspace(0.0, 1.0, L, dtype=f32)[:, None]
    omega = 2.0 * math.pi * jnp.arange(L, dtype=f32)[:, None] / L
    bands = jnp.linspace(1e-4, HY_BANDS - 1, HY_BANDS, dtype=f32)[None, :]
    feats = jnp.concatenate([t, jnp.cos(bands * omega), -jnp.sin(bands * omega)], axis=-1)
    fr = freq.astype(f32)
    h = jnp.sin(fr * (feats @ w1.astype(f32) + b1.astype(f32)))
    h = jnp.sin(fr * (h @ w2.astype(f32) + b2.astype(f32)))
    h = h @ w3.astype(f32)
    min_decay = math.log(HY_DECAY_TARGET) / HY_SLOW_DECAY
    max_decay = math.log(HY_DECAY_TARGET) / HY_FAST_DECAY
    deltas = jnp.abs(jnp.linspace(min_decay, max_decay, HY_WIDTH, dtype=f32))
    window = jnp.exp(-t * deltas)
    return h.reshape(L, HY_DIRS, HY_ORDER, HY_WIDTH) * window[:, None, None, :]


def two_sided_spectrum(h_fwd, h_bwd):
    L, ch = h_fwd.shape
    k = jnp.concatenate([h_fwd.at[0].add(h_bwd[0]), jnp.zeros((1, ch), h_fwd.dtype), h_bwd[:0:-1]], axis=0)
    return jnp.fft.rfft(k, axis=0)


def fft_long_conv(u, k_spec, skip):
    L = u.shape[1]
    uf = u.astype(jnp.float32)
    y = jnp.fft.irfft(jnp.fft.rfft(uf, n=2 * L, axis=1) * k_spec[None], n=2 * L, axis=1)[:, :L]
    return (y + uf * skip.astype(jnp.float32)).astype(u.dtype)


def hyena(z, conv_w, conv_b, w1, b1, w2, b2, freq, w3, skip):
    z = short_conv(z, conv_w, conv_b)
    v, x1, x2 = jnp.split(z, HY_ORDER + 1, axis=-1)
    filt = hyena_filters(z.shape[1], w1, b1, w2, b2, freq, w3)
    y = v
    for n, gate in enumerate((x1, x2)):
        k_spec = two_sided_spectrum(filt[:, 0, n], filt[:, 1, n])
        y = gate * fft_long_conv(y, k_spec, skip[n])
    return y


def setup_inputs(seed: int = 0) -> dict:
    key = jax.random.key(seed)
    keys = iter(jax.random.split(key, 40))
    D = D_MODEL
    L = DEPTH

    def nrm(shape, scale):
        return jax.random.normal(next(keys), shape, jnp.float32) * scale

    return {
        "x": nrm((BATCH, SEQ, D), 1.0),
        "c": nrm((BATCH, D), 1.0),
        "w_ada": nrm((L, D, N_MOD * D), 0.5 * D ** -0.5),
        "b_ada": nrm((L, N_MOD * D), 0.02),
        "w_in": nrm((L, D, IN_WIDTH), D ** -0.5),
        "b_in": nrm((L, IN_WIDTH), 0.02),
        "sgu_ln_g": 1.0 + nrm((L, SGU_WIDTH), 0.05),
        "sgu_ln_b": nrm((L, SGU_WIDTH), 0.02),
        "sgu_w": nrm((L, SGU_GROUPS, SGU_CHUNK, SGU_CHUNK), SGU_CHUNK ** -0.5),
        "sgu_b": 1.0 + nrm((L, SGU_GROUPS, SGU_CHUNK), 0.1),
        "hy_conv_w": nrm((L, HY_SHORT, (HY_ORDER + 1) * HY_WIDTH), HY_SHORT ** -0.5),
        "hy_conv_b": nrm((L, (HY_ORDER + 1) * HY_WIDTH), 0.02),
        "hy_w1": nrm((L, HY_EMB, HY_FILTER_HIDDEN), HY_EMB ** -0.5),
        "hy_b1": nrm((L, HY_FILTER_HIDDEN), 0.1),
        "hy_w2": nrm((L, HY_FILTER_HIDDEN, HY_FILTER_HIDDEN), HY_FILTER_HIDDEN ** -0.5),
        "hy_b2": nrm((L, HY_FILTER_HIDDEN), 0.1),
        "hy_freq": 1.0 + nrm((L, HY_FILTER_HIDDEN), 0.1),
        "hy_w3": nrm((L, HY_FILTER_HIDDEN, HY_DIRS * HY_ORDER * HY_WIDTH), HY_FILTER_INIT_SCALE * HY_FILTER_HIDDEN ** -0.5),
        "hy_skip": nrm((L, HY_ORDER, HY_WIDTH), 0.5),
        "w_branch_a": nrm((L, SGU_WIDTH, D), DEEPNORM_BETA * SGU_WIDTH ** -0.5),
        "w_branch_b": nrm((L, HY_WIDTH, D), DEEPNORM_BETA * HY_WIDTH ** -0.5),
        "w_o": nrm((L, D, D), DEEPNORM_BETA * D ** -0.5),
        "b_o": nrm((L, D), 0.02),
        "ln1_g": 1.0 + nrm((L, D), 0.05),
        "ln1_b": nrm((L, D), 0.02),
        "w_m1": nrm((L, D, D_FF), DEEPNORM_BETA * D ** -0.5),
        "b_m1": nrm((L, D_FF), 0.02),
        "w_m2": nrm((L, D_FF, D), DEEPNORM_BETA * D_FF ** -0.5),
        "b_m2": nrm((L, D), 0.02),
        "ln2_g": 1.0 + nrm((L, D), 0.05),
        "ln2_b": nrm((L, D), 0.02),
    }


def reference(x, c, w_ada, b_ada, w_in, b_in, sgu_ln_g, sgu_ln_b, sgu_w, sgu_b,
              hy_conv_w, hy_conv_b, hy_w1, hy_b1, hy_w2, hy_b2, hy_freq, hy_w3, hy_skip,
              w_branch_a, w_branch_b, w_o, b_o, ln1_g, ln1_b,
              w_m1, b_m1, w_m2, b_m2, ln2_g, ln2_b):
    split_pts = [SGU_WIDTH, 2 * SGU_WIDTH, 2 * SGU_WIDTH + (HY_ORDER + 1) * HY_WIDTH]
    cond = jax.nn.silu(c)
    for l in range(DEPTH):
        mod = (cond @ w_ada[l] + b_ada[l])[:, None, :]
        sh1, sc1, g1, sh2, sc2, g2 = jnp.split(mod, N_MOD, axis=-1)

        h = x * (1.0 + sc1) + sh1
        z = h @ w_in[l] + b_in[l]
        z_u, z_v, z_h, z_g = jnp.split(z, split_pts, axis=-1)
        y_a = spatial_gating(jax.nn.gelu(z_u, approximate=False), jax.nn.gelu(z_v, approximate=False),
                             sgu_ln_g[l], sgu_ln_b[l], sgu_w[l], sgu_b[l]) @ w_branch_a[l]
        y_b = hyena(z_h, hy_conv_w[l], hy_conv_b[l], hy_w1[l], hy_b1[l], hy_w2[l], hy_b2[l],
                    hy_freq[l], hy_w3[l], hy_skip[l]) @ w_branch_b[l]
        gate_a, gate_b = jnp.split(jax.nn.sigmoid(z_g), N_BRANCHES, axis=-1)
        mix = (gate_a * y_a + gate_b * y_b) @ w_o[l] + b_o[l]
        x = layer_norm(DEEPNORM_ALPHA * x + g1 * mix, ln1_g[l], ln1_b[l])

        h = x * (1.0 + sc2) + sh2
        f = jnp.square(jax.nn.relu(h @ w_m1[l] + b_m1[l])) @ w_m2[l] + b_m2[l]
        x = layer_norm(DEEPNORM_ALPHA * x + g2 * f, ln2_g[l], ln2_b[l])
    return x
```

```python
import functools
import math

import jax
import jax.numpy as jnp
from jax import lax
from jax.experimental import pallas as pl
from jax.experimental.pallas import tpu as pltpu

F32 = jnp.float32
BF16 = jnp.bfloat16

SGU_CHUNK = 128
SGU_GROUPS = 8
HY_ORDER = 2
HY_SHORT = 3
HY_EMB = 33
HY_DECAY_TARGET = 1e-2
HY_FAST_DECAY = 0.3
HY_SLOW_DECAY = 1.5
N_MOD = 6
LN_EPS = 1e-5

LANE = 128
MXU_DIM = 256
VMEM_LIMIT = 48 * 1024 * 1024


def _cparams(sem):
    return pltpu.CompilerParams(dimension_semantics=sem, vmem_limit_bytes=VMEM_LIMIT)


def _resident(block_shape, index_map):
    return pl.BlockSpec(block_shape, index_map, pipeline_mode=pl.Buffered(1))


def _gelu(x):
    return 0.5 * x * (1.0 + lax.erf(x * (1.0 / math.sqrt(2.0))))


def _layer_norm(x, g, b):
    mu = jnp.mean(x, axis=-1, keepdims=True)
    xc = x - mu
    var = jnp.mean(xc * xc, axis=-1, keepdims=True)
    return xc * lax.rsqrt(var + LN_EPS) * g + b


def _mod_kernel(c_ref, w_ref, b_ref, o_ref):
    c = c_ref[...]
    cond = c * jax.nn.sigmoid(c)
    o_ref[...] = jnp.dot(cond, w_ref[...], preferred_element_type=F32,
                         precision=lax.Precision.HIGHEST) + b_ref[...]


def _mod_call(c, w_ada, b_ada):
    bsz, d = c.shape
    n = w_ada.shape[1]
    tn = 1024
    return pl.pallas_call(
        _mod_kernel,
        grid=(n // tn,),
        in_specs=[pl.BlockSpec((bsz, d), lambda j: (0, 0)),
                  pl.BlockSpec((d, tn), lambda j: (0, j)),
                  pl.BlockSpec((1, tn), lambda j: (0, j))],
        out_specs=pl.BlockSpec((bsz, tn), lambda j: (0, j)),
        out_shape=jax.ShapeDtypeStruct((bsz, n), F32),
        compiler_params=_cparams(("arbitrary",)),
        name="adaln_mod",
    )(c, w_ada, b_ada.reshape(1, n))


def _inproj_kernel(x_ref, sc_ref, sh_ref, w_ref, b_ref, g_ref, be_ref, o_ref, h_scr, *, seg):
    j = pl.program_id(1)

    @pl.when(j == 0)
    def _():
        h_scr[...] = (x_ref[...] * (1.0 + sc_ref[0]) + sh_ref[0]).astype(BF16)

    def proj():
        return jnp.dot(h_scr[...], w_ref[...], preferred_element_type=F32) + b_ref[...]

    @pl.when(j < seg[0])
    def _():
        o_ref[...] = _gelu(proj()).astype(o_ref.dtype)

    @pl.when(jnp.logical_and(j >= seg[0], j < seg[1]))
    def _():
        o_ref[...] = _layer_norm(_gelu(proj()), g_ref[...], be_ref[...]).astype(o_ref.dtype)

    @pl.when(jnp.logical_and(j >= seg[1], j < seg[2]))
    def _():
        o_ref[...] = proj().astype(o_ref.dtype)

    @pl.when(j >= seg[2])
    def _():
        o_ref[...] = jax.nn.sigmoid(proj()).astype(o_ref.dtype)


def _inproj_call(xt, mod3, w_in, b_in, ln_g, ln_b, seq, sgu_w, hy_w):
    t, d = xt.shape
    n = w_in.shape[1]
    tm = 1024
    tn = sgu_w
    assert hy_w == sgu_w and seq % tm == 0 and d % (2 * tn) == 0
    nblk = n // tn
    seg = (1, 2, 2 + HY_ORDER + 1)
    gate_blk = -(-seg[2] * tn // d) * d // tn
    shift = gate_blk - seg[2]
    per_b = seq // tm
    return pl.pallas_call(
        functools.partial(_inproj_kernel, seg=seg),
        grid=(t // tm, nblk),
        in_specs=[pl.BlockSpec((tm, d), lambda i, j: (i, 0)),
                  pl.BlockSpec((1, 1, d), lambda i, j: (i // per_b, 0, 1)),
                  pl.BlockSpec((1, 1, d), lambda i, j: (i // per_b, 0, 0)),
                  pl.BlockSpec((d, tn), lambda i, j: (0, j)),
                  pl.BlockSpec((1, tn), lambda i, j: (0, j)),
                  pl.BlockSpec((1, tn), lambda i, j: (0, 0)),
                  pl.BlockSpec((1, tn), lambda i, j: (0, 0))],
        out_specs=pl.BlockSpec((tm, tn), lambda i, j: (i, j + jnp.where(j >= seg[2], shift, 0))),
        out_shape=jax.ShapeDtypeStruct((t, (nblk + shift) * tn), BF16),
        scratch_shapes=[pltpu.VMEM((tm, d), BF16)],
        compiler_params=_cparams(("parallel", "arbitrary")),
        name="in_proj",
    )(xt, mod3, mod3, w_in, b_in.reshape(1, n), ln_g.reshape(1, tn), ln_b.reshape(1, tn)), gate_blk


def _filter_mlp_kernel(f_ref, w1_ref, b1_ref, w2_ref, b2_ref, fr_ref, o_ref):
    hp = lax.Precision.HIGHEST
    fr = fr_ref[...]
    h = jnp.sin(fr * (jnp.dot(f_ref[...], w1_ref[...], preferred_element_type=F32, precision=hp)
                      + b1_ref[...]))
    o_ref[...] = jnp.sin(fr * (jnp.dot(h, w2_ref[...], preferred_element_type=F32, precision=hp)
                               + b2_ref[...]))


def _filter_mlp_call(feats, w1, b1, w2, b2, freq):
    seq = feats.shape[0]
    hid = w1.shape[1]
    return pl.pallas_call(
        _filter_mlp_kernel,
        out_shape=jax.ShapeDtypeStruct((seq, hid), F32),
        name="hyena_filter_mlp",
    )(feats, w1, b1.reshape(1, hid), w2, b2.reshape(1, hid), freq.reshape(1, hid))


def _filter_spec_kernel(h_ref, wf_ref, wb_ref, t_ref, dl_ref, cos_ref, sin_ref,
                        kre_ref, kim_ref, kny_ref):
    hp = lax.Precision.HIGHEST
    seq = h_ref.shape[0]
    window = jnp.exp(-t_ref[...] * dl_ref[...])
    hf = jnp.dot(h_ref[...], wf_ref[...], preferred_element_type=F32, precision=hp) * window
    hb = jnp.dot(h_ref[...], wb_ref[...], preferred_element_type=F32, precision=hp) * window
    even = hf + hb
    odd = hb - hf
    row = lax.broadcasted_iota(jnp.int32, even.shape, 0)
    inv_n = 1.0 / (2 * seq)
    wgt = jnp.where(row == 0, inv_n, 2.0 * inv_n)
    kre_ref[0] = jnp.dot(cos_ref[...], even.astype(BF16), preferred_element_type=F32) * wgt
    kim_ref[0] = jnp.dot(sin_ref[...], odd.astype(BF16), preferred_element_type=F32) * wgt
    sign = jnp.where((row & 1) == 0, 1.0, -1.0)
    kny_ref[0] = jnp.sum(even * sign, axis=0, keepdims=True) * inv_n


def _filter_spec_call(h2, w3, tcol, deltas, cos_m, sin_m, width):
    seq, hid = h2.shape
    cb = MXU_DIM
    ncb = width // cb
    per_dir = HY_ORDER * ncb
    out_sd = jax.ShapeDtypeStruct((HY_ORDER, seq, width), F32)
    return pl.pallas_call(
        _filter_spec_kernel,
        grid=(HY_ORDER, ncb),
        in_specs=[_resident((seq, hid), lambda n, k: (0, 0)),
                  pl.BlockSpec((hid, cb), lambda n, k: (0, n * ncb + k)),
                  pl.BlockSpec((hid, cb), lambda n, k: (0, per_dir + n * ncb + k)),
                  _resident((seq, 1), lambda n, k: (0, 0)),
                  pl.BlockSpec((1, cb), lambda n, k: (0, k)),
                  _resident((seq, seq), lambda n, k: (0, 0)),
                  _resident((seq, seq), lambda n, k: (0, 0))],
        out_specs=[pl.BlockSpec((1, seq, cb), lambda n, k: (n, 0, k)),
                   pl.BlockSpec((1, seq, cb), lambda n, k: (n, 0, k)),
                   pl.BlockSpec((1, 1, cb), lambda n, k: (n, 0, k))],
        out_shape=[out_sd, out_sd, jax.ShapeDtypeStruct((HY_ORDER, 1, width), F32)],
        compiler_params=_cparams(("arbitrary", "arbitrary")),
        name="hyena_filter_spectrum",
    )(h2, w3, w3, tcol, deltas, cos_m, sin_m)


def _short_conv_into(dst_ref, z_ref, w_ref, b_ref):
    seq = z_ref.shape[0]
    z = z_ref[...].astype(F32)
    w = w_ref[...]
    dst_ref[...] = (w[0:1] * pltpu.roll(z, 1, 0) + w[1:2] * z
                    + w[2:3] * pltpu.roll(z, seq - 1, 0) + b_ref[...])
    dst_ref[0:1, :] -= w[0:1] * z_ref[seq - 1:seq, :].astype(F32)
    dst_ref[seq - 1:seq, :] -= w[2:3] * z_ref[0:1, :].astype(F32)


def _hyena_kernel(v_ref, x1_ref, x2_ref, wv_ref, w1_ref, w2_ref, bv_ref, b1_ref, b2_ref,
                  skip_ref, kre_ref, kim_ref, kny_ref, cos_ref, sin_ref, o_ref,
                  y_scr, y16_scr, g_scr, a_scr, b_scr, *, rc):
    seq, cb = v_ref.shape
    chunks = [slice(r, r + rc) for r in range(0, seq, rc)]
    parity = lax.broadcasted_iota(jnp.int32, (rc, cb), 0) & 1
    sign = jnp.where(parity == 0, 1.0, -1.0)

    _short_conv_into(y_scr, v_ref, wv_ref, bv_ref)
    _short_conv_into(g_scr.at[0], x1_ref, w1_ref, b1_ref)
    _short_conv_into(g_scr.at[1], x2_ref, w2_ref, b2_ref)

    for n in range(HY_ORDER):
        x_ny = jnp.zeros((1, cb), F32)
        for rows in chunks:
            yc = y_scr[rows, :]
            y16_scr[rows, :] = yc.astype(BF16)
            x_ny = x_ny + jnp.sum(yc * sign, axis=0, keepdims=True)
        for rows in chunks:
            p = jnp.dot(cos_ref[rows, :], y16_scr[...], preferred_element_type=F32)
            q = jnp.dot(sin_ref[rows, :], y16_scr[...], preferred_element_type=F32)
            kre = kre_ref[n, rows, :]
            kim = kim_ref[n, rows, :]
            a_scr[rows, :] = (p * kre + q * kim).astype(BF16)
            b_scr[rows, :] = (q * kre - p * kim).astype(BF16)
        ny = x_ny * kny_ref[n]
        skip = skip_ref[n:n + 1, :]
        for rows in chunks:
            conv = (jnp.dot(cos_ref[rows, :], a_scr[...], preferred_element_type=F32)
                    + jnp.dot(sin_ref[rows, :], b_scr[...], preferred_element_type=F32)
                    + sign * ny)
            y_new = g_scr[n, rows, :] * (conv + y_scr[rows, :] * skip)
            if n + 1 < HY_ORDER:
                y_scr[rows, :] = y_new
            else:
                o_ref[rows, :] = y_new.astype(o_ref.dtype)


def _hyena_call(z, conv_w, conv_b, skip, kre, kim, kny, cos_m, sin_m, bsz, seq, width, col0):
    cb = MXU_DIM
    ncb = width // cb
    base = col0 // cb
    zspec = lambda part: pl.BlockSpec((seq, cb), lambda k, b: (b, base + part * ncb + k))
    wspec = lambda part: pl.BlockSpec((HY_SHORT, cb), lambda k, b: (0, part * ncb + k))
    bspec = lambda part: pl.BlockSpec((1, cb), lambda k, b: (0, part * ncb + k))
    cb_all = conv_b.reshape(1, -1)
    return pl.pallas_call(
        functools.partial(_hyena_kernel, rc=512),
        grid=(ncb, bsz),
        in_specs=[zspec(0), zspec(1), zspec(2), wspec(0), wspec(1), wspec(2),
                  bspec(0), bspec(1), bspec(2),
                  pl.BlockSpec((HY_ORDER, cb), lambda k, b: (0, k)),
                  _resident((HY_ORDER, seq, cb), lambda k, b: (0, 0, k)),
                  _resident((HY_ORDER, seq, cb), lambda k, b: (0, 0, k)),
                  pl.BlockSpec((HY_ORDER, 1, cb), lambda k, b: (0, 0, k)),
                  _resident((seq, seq), lambda k, b: (0, 0)),
                  _resident((seq, seq), lambda k, b: (0, 0))],
        out_specs=pl.BlockSpec((seq, cb), lambda k, b: (b, k)),
        out_shape=jax.ShapeDtypeStruct((bsz * seq, width), BF16),
        scratch_shapes=[pltpu.VMEM((seq, cb), F32), pltpu.VMEM((seq, cb), BF16),
                        pltpu.VMEM((HY_ORDER, seq, cb), F32),
                        pltpu.VMEM((seq, cb), BF16), pltpu.VMEM((seq, cb), BF16)],
        compiler_params=_cparams(("arbitrary", "arbitrary")),
        name="hyena_mixer",
    )(z, z, z, conv_w, conv_w, conv_w, cb_all, cb_all, cb_all, skip, kre, kim, kny, cos_m, sin_m)


def _merge_kernel(u_ref, vn_ref, yh_ref, ga_ref, gb_ref, ws_ref, bs_ref, wa_ref, wb_ref, o_ref,
                  s_scr):
    tm = u_ref.shape[0]
    gd = u_ref.shape[1] // SGU_GROUPS
    for n in range(tm // SGU_CHUNK):
        rows = slice(n * SGU_CHUNK, (n + 1) * SGU_CHUNK)
        for g in range(SGU_GROUPS):
            cols = slice(g * gd, (g + 1) * gd)
            mixed = jnp.dot(ws_ref[g], vn_ref[rows, cols], preferred_element_type=F32)
            s_scr[rows, cols] = (u_ref[rows, cols].astype(F32)
                                 * (mixed + bs_ref[:, cols])).astype(BF16)
    y_a = jnp.dot(s_scr[...], wa_ref[...], preferred_element_type=F32)
    y_b = jnp.dot(yh_ref[...], wb_ref[...], preferred_element_type=F32)
    o_ref[...] = (ga_ref[...].astype(F32) * y_a + gb_ref[...].astype(F32) * y_b).astype(o_ref.dtype)


def _merge_call(z, yh, sgu_w16, sgu_bfull, w_a, w_b, d, gate_blk_cols):
    t = z.shape[0]
    wdt = w_a.shape[0]
    tm = 512
    ga = gate_blk_cols // d
    return pl.pallas_call(
        _merge_kernel,
        grid=(t // tm,),
        in_specs=[pl.BlockSpec((tm, wdt), lambda i: (i, 0)),
                  pl.BlockSpec((tm, wdt), lambda i: (i, 1)),
                  pl.BlockSpec((tm, wdt), lambda i: (i, 0)),
                  pl.BlockSpec((tm, d), lambda i: (i, ga)),
                  pl.BlockSpec((tm, d), lambda i: (i, ga + 1)),
                  _resident(sgu_w16.shape, lambda i: (0, 0, 0)),
                  _resident(sgu_bfull.shape, lambda i: (0, 0)),
                  _resident(w_a.shape, lambda i: (0, 0)),
                  _resident(w_b.shape, lambda i: (0, 0))],
        out_specs=pl.BlockSpec((tm, d), lambda i: (i, 0)),
        out_shape=jax.ShapeDtypeStruct((t, d), BF16),
        scratch_shapes=[pltpu.VMEM((tm, wdt), BF16)],
        compiler_params=_cparams(("parallel",)),
        name="gated_merge",
    )(z, z, yh, z, z, sgu_w16, sgu_bfull, w_a, w_b)


def _oproj_kernel(m_ref, x_ref, g1_ref, w_ref, b_ref, lg_ref, lb_ref, o_ref, *, alpha):
    mix = jnp.dot(m_ref[...], w_ref[...], preferred_element_type=F32) + b_ref[...]
    o_ref[...] = _layer_norm(alpha * x_ref[...] + g1_ref[0] * mix, lg_ref[...], lb_ref[...])


def _oproj_call(m, xt, mod3, w_o, b_o, ln_g, ln_b, seq, alpha):
    t, d = xt.shape
    tm = 512
    per_b = seq // tm
    row = lambda i: (0, 0)
    return pl.pallas_call(
        functools.partial(_oproj_kernel, alpha=alpha),
        grid=(t // tm,),
        in_specs=[pl.BlockSpec((tm, d), lambda i: (i, 0)),
                  pl.BlockSpec((tm, d), lambda i: (i, 0)),
                  pl.BlockSpec((1, 1, d), lambda i: (i // per_b, 0, 2)),
                  _resident((d, d), row),
                  pl.BlockSpec((1, d), row), pl.BlockSpec((1, d), row), pl.BlockSpec((1, d), row)],
        out_specs=pl.BlockSpec((tm, d), lambda i: (i, 0)),
        out_shape=jax.ShapeDtypeStruct((t, d), F32),
        compiler_params=_cparams(("parallel",)),
        name="out_proj_ln",
    )(m, xt, mod3, w_o, b_o.reshape(1, d), ln_g.reshape(1, d), ln_b.reshape(1, d))


def _mlp_kernel(x_ref, sc_ref, sh_ref, g2_ref, w1_ref, b1_ref, w2_ref, b2_ref, lg_ref, lb_ref,
                o_ref, h_scr, acc_scr, *, alpha):
    j = pl.program_id(1)

    @pl.when(j == 0)
    def _():
        h_scr[...] = (x_ref[...] * (1.0 + sc_ref[0]) + sh_ref[0]).astype(BF16)
        acc_scr[...] = jnp.zeros_like(acc_scr)

    hid = jnp.dot(h_scr[...], w1_ref[...], preferred_element_type=F32) + b1_ref[...]
    hid = jnp.square(jnp.maximum(hid, 0.0)).astype(BF16)
    acc_scr[...] += jnp.dot(hid, w2_ref[...], preferred_element_type=F32)

    @pl.when(j == pl.num_programs(1) - 1)
    def _():
        f = acc_scr[...] + b2_ref[...]
        o_ref[...] = _layer_norm(alpha * x_ref[...] + g2_ref[0] * f, lg_ref[...], lb_ref[...])


def _mlp_call(x1, mod3, w_m1, b_m1, w_m2, b_m2, ln_g, ln_b, seq, alpha):
    t, d = x1.shape
    dff = w_m1.shape[1]
    tm = 512
    tf = 1024
    per_b = seq // tm
    row = lambda i, j: (0, 0)
    return pl.pallas_call(
        functools.partial(_mlp_kernel, alpha=alpha),
        grid=(t // tm, dff // tf),
        in_specs=[pl.BlockSpec((tm, d), lambda i, j: (i, 0)),
                  pl.BlockSpec((1, 1, d), lambda i, j: (i // per_b, 0, 4)),
                  pl.BlockSpec((1, 1, d), lambda i, j: (i // per_b, 0, 3)),
                  pl.BlockSpec((1, 1, d), lambda i, j: (i // per_b, 0, 5)),
                  pl.BlockSpec((d, tf), lambda i, j: (0, j)),
                  pl.BlockSpec((1, tf), lambda i, j: (0, j)),
                  pl.BlockSpec((tf, d), lambda i, j: (j, 0)),
                  pl.BlockSpec((1, d), row), pl.BlockSpec((1, d), row), pl.BlockSpec((1, d), row)],
        out_specs=pl.BlockSpec((tm, d), lambda i, j: (i, 0)),
        out_shape=jax.ShapeDtypeStruct((t, d), F32),
        scratch_shapes=[pltpu.VMEM((tm, d), BF16), pltpu.VMEM((tm, d), F32)],
        compiler_params=_cparams(("parallel", "arbitrary")),
        name="mlp_ln",
    )(x1, mod3, mod3, mod3, w_m1, b_m1.reshape(1, dff), w_m2, b_m2.reshape(1, d),
      ln_g.reshape(1, d), ln_b.reshape(1, d))


def _dft_tables(seq):
    idx = jnp.arange(seq, dtype=jnp.int32)
    k = (idx[:, None] * idx[None, :]) & (2 * seq - 1)
    ang = k.astype(F32) * (math.pi / seq)
    return jnp.cos(ang).astype(BF16), jnp.sin(ang).astype(BF16)


def _filter_features(seq):
    bands_n = (HY_EMB - 1) // 2
    t = jnp.linspace(0.0, 1.0, seq, dtype=F32)[:, None]
    omega = 2.0 * math.pi * jnp.arange(seq, dtype=F32)[:, None] / seq
    bands = jnp.linspace(1e-4, bands_n - 1, bands_n, dtype=F32)[None, :]
    feats = jnp.concatenate([t, jnp.cos(bands * omega), -jnp.sin(bands * omega)], axis=-1)
    return t, feats


def _decay_rates(width):
    min_decay = math.log(HY_DECAY_TARGET) / HY_SLOW_DECAY
    max_decay = math.log(HY_DECAY_TARGET) / HY_FAST_DECAY
    return jnp.abs(jnp.linspace(min_decay, max_decay, width, dtype=F32))[None, :]


def kernel(x, c, w_ada, b_ada, w_in, b_in, sgu_ln_g, sgu_ln_b, sgu_w, sgu_b, hy_conv_w, hy_conv_b, hy_w1, hy_b1, hy_w2, hy_b2, hy_freq, hy_w3, hy_skip, w_branch_a, w_branch_b, w_o, b_o, ln1_g, ln1_b, w_m1, b_m1, w_m2, b_m2, ln2_g, ln2_b):
    bsz, seq, d = x.shape
    depth = w_ada.shape[0]
    alpha = (2.0 * depth) ** 0.25
    sgu_width = w_branch_a.shape[1]
    hy_width = w_branch_b.shape[1]
    assert seq % SGU_CHUNK == 0 and sgu_w.shape[-1] == SGU_CHUNK

    cos_m, sin_m = _dft_tables(seq)
    tcol, feats = _filter_features(seq)
    deltas = _decay_rates(hy_width)

    xt = x.reshape(bsz * seq, d)
    for l in range(depth):
        mod3 = _mod_call(c, w_ada[l], b_ada[l]).reshape(bsz, 1, N_MOD * d)

        z, gate_blk = _inproj_call(xt, mod3, w_in[l].astype(BF16), b_in[l], sgu_ln_g[l], sgu_ln_b[l],
                                   seq, sgu_width, hy_width)

        h2 = _filter_mlp_call(feats, hy_w1[l], hy_b1[l], hy_w2[l], hy_b2[l], hy_freq[l])
        kre, kim, kny = _filter_spec_call(h2, hy_w3[l], tcol, deltas, cos_m, sin_m, hy_width)
        yh = _hyena_call(z, hy_conv_w[l], hy_conv_b[l], hy_skip[l], kre, kim, kny, cos_m, sin_m,
                         bsz, seq, hy_width, 2 * sgu_width)

        gd = sgu_width // SGU_GROUPS
        bs_full = jnp.repeat(sgu_b[l].T, gd, axis=1)
        m = _merge_call(z, yh, sgu_w[l].astype(BF16), bs_full, w_branch_a[l].astype(BF16),
                        w_branch_b[l].astype(BF16), d, gate_blk * sgu_width)

        x1 = _oproj_call(m, xt, mod3, w_o[l].astype(BF16), b_o[l], ln1_g[l], ln1_b[l], seq, alpha)
        xt = _mlp_call(x1, mod3, w_m1[l].astype(BF16), b_m1[l], w_m2[l].astype(BF16), b_m2[l],
                       ln2_g[l], ln2_b[l], seq, alpha)
    return xt.reshape(bsz, seq, d)
```

```python
import functools
import math

import jax
import jax.numpy as jnp
from jax import lax
from jax.experimental import pallas as pl
from jax.experimental.pallas import tpu as pltpu

F32 = jnp.float32
BF16 = jnp.bfloat16

SGU_CHUNK = 128
SGU_GROUPS = 8
HY_ORDER = 2
HY_SHORT = 3
HY_EMB = 33
HY_DECAY_TARGET = 1e-2
HY_FAST_DECAY = 0.3
HY_SLOW_DECAY = 1.5
N_MOD = 6
LN_EPS = 1e-5

LANE = 128
MXU_DIM = 256
VMEM_LIMIT = 48 * 1024 * 1024


def _cparams(sem):
    return pltpu.CompilerParams(dimension_semantics=sem, vmem_limit_bytes=VMEM_LIMIT)


def _resident(block_shape, index_map):
    return pl.BlockSpec(block_shape, index_map, pipeline_mode=pl.Buffered(1))


def _gelu(x):
    return 0.5 * x * (1.0 + lax.erf(x * (1.0 / math.sqrt(2.0))))


def _layer_norm(x, g, b):
    mu = jnp.mean(x, axis=-1, keepdims=True)
    xc = x - mu
    var = jnp.mean(xc * xc, axis=-1, keepdims=True)
    return xc * lax.rsqrt(var + LN_EPS) * g + b


def _mod_kernel(c_ref, w_ref, b_ref, o_ref):
    c = c_ref[...]
    cond = c * jax.nn.sigmoid(c)
    o_ref[...] = jnp.dot(cond, w_ref[...], preferred_element_type=F32,
                         precision=lax.Precision.HIGHEST) + b_ref[...]


def _mod_call(c, w_ada, b_ada):
    bsz, d = c.shape
    n = w_ada.shape[1]
    tn = 1024
    return pl.pallas_call(
        _mod_kernel,
        grid=(n // tn,),
        in_specs=[pl.BlockSpec((bsz, d), lambda j: (0, 0)),
                  pl.BlockSpec((d, tn), lambda j: (0, j)),
                  pl.BlockSpec((1, tn), lambda j: (0, j))],
        out_specs=pl.BlockSpec((bsz, tn), lambda j: (0, j)),
        out_shape=jax.ShapeDtypeStruct((bsz, n), F32),
        compiler_params=_cparams(("arbitrary",)),
        name="adaln_mod",
    )(c, w_ada, b_ada.reshape(1, n))


def _inproj_kernel(x_ref, sc_ref, sh_ref, w_ref, b_ref, g_ref, be_ref, o_ref, h_scr, *, seg):
    j = pl.program_id(1)

    @pl.when(j == 0)
    def _():
        h_scr[...] = (x_ref[...] * (1.0 + sc_ref[0]) + sh_ref[0]).astype(BF16)

    def proj():
        return jnp.dot(h_scr[...], w_ref[...], preferred_element_type=F32) + b_ref[...]

    @pl.when(j < seg[0])
    def _():
        o_ref[...] = _gelu(proj()).astype(o_ref.dtype)

    @pl.when(jnp.logical_and(j >= seg[0], j < seg[1]))
    def _():
        o_ref[...] = _layer_norm(_gelu(proj()), g_ref[...], be_ref[...]).astype(o_ref.dtype)

    @pl.when(jnp.logical_and(j >= seg[1], j < seg[2]))
    def _():
        o_ref[...] = proj().astype(o_ref.dtype)

    @pl.when(j >= seg[2])
    def _():
        o_ref[...] = jax.nn.sigmoid(proj()).astype(o_ref.dtype)


def _inproj_call(xt, mod3, w_in, b_in, ln_g, ln_b, seq, sgu_w, hy_w):
    t, d = xt.shape
    n = w_in.shape[1]
    tm = 1024
    tn = sgu_w
    assert hy_w == sgu_w and seq % tm == 0 and d % (2 * tn) == 0
    nblk = n // tn
    seg = (1, 2, 2 + HY_ORDER + 1)
    gate_blk = -(-seg[2] * tn // d) * d // tn
    shift = gate_blk - seg[2]
    per_b = seq // tm
    return pl.pallas_call(
        functools.partial(_inproj_kernel, seg=seg),
        grid=(t // tm, nblk),
        in_specs=[pl.BlockSpec((tm, d), lambda i, j: (i, 0)),
                  pl.BlockSpec((1, 1, d), lambda i, j: (i // per_b, 0, 1)),
                  pl.BlockSpec((1, 1, d), lambda i, j: (i // per_b, 0, 0)),
                  pl.BlockSpec((d, tn), lambda i, j: (0, j)),
                  pl.BlockSpec((1, tn), lambda i, j: (0, j)),
                  pl.BlockSpec((1, tn), lambda i, j: (0, 0)),
                  pl.BlockSpec((1, tn), lambda i, j: (0, 0))],
        out_specs=pl.BlockSpec((tm, tn), lambda i, j: (i, j + jnp.where(j >= seg[2], shift, 0))),
        out_shape=jax.ShapeDtypeStruct((t, (nblk + shift) * tn), BF16),
        scratch_shapes=[pltpu.VMEM((tm, d), BF16)],
        compiler_params=_cparams(("parallel", "arbitrary")),
        name="in_proj",
    )(xt, mod3, mod3, w_in, b_in.reshape(1, n), ln_g.reshape(1, tn), ln_b.reshape(1, tn)), gate_blk


def _filter_mlp_kernel(f_ref, w1_ref, b1_ref, w2_ref, b2_ref, fr_ref, o_ref):
    hp = lax.Precision.HIGHEST
    fr = fr_ref[...]
    h = jnp.sin(fr * (jnp.dot(f_ref[...], w1_ref[...], preferred_element_type=F32, precision=hp)
                      + b1_ref[...]))
    o_ref[...] = jnp.sin(fr * (jnp.dot(h, w2_ref[...], preferred_element_type=F32, precision=hp)
                               + b2_ref[...]))


def _filter_mlp_call(feats, w1, b1, w2, b2, freq):
    seq = feats.shape[0]
    hid = w1.shape[1]
    return pl.pallas_call(
        _filter_mlp_kernel,
        out_shape=jax.ShapeDtypeStruct((seq, hid), F32),
        name="hyena_filter_mlp",
    )(feats, w1, b1.reshape(1, hid), w2, b2.reshape(1, hid), freq.reshape(1, hid))


DFT_GROUPS = 9
_RSQRT2 = 1.0 / math.sqrt(2.0)


def _cos8(k):
    return math.cos(math.pi * k / 8.0)


def _sin8(k):
    return math.sin(math.pi * k / 8.0)


def _packed_start(g, n2):
    return {0: 0, 8: n2}.get(g, 2 * n2 * g)


def _dft16_fwd(u):
    def half(p0, p1, p2, p3):
        a, b, c, d = p0 + p2, p0 - p2, p1 + p3, p1 - p3
        rd, rc = _RSQRT2 * d, _RSQRT2 * c
        return a + c, a - c, b, d, (p0 + rd, -(rc + p2)), (p0 - rd, p2 - rc)

    e0, e4, be, de, e1, e3 = half(u[0], u[2], u[4], u[6])
    o0, o4, bo, do, (x1, y1), (x3, y3) = half(u[1], u[3], u[5], u[7])
    re, im = {}, {}
    re[4], im[4] = e4, -o4
    re[1], im[1] = e1[0] + _cos8(1) * x1 + _sin8(1) * y1, e1[1] + _cos8(1) * y1 - _sin8(1) * x1
    re[3], im[3] = e3[0] + _cos8(3) * x3 + _sin8(3) * y3, e3[1] + _cos8(3) * y3 - _sin8(3) * x3
    re[5], im[5] = e3[0] + _cos8(5) * x3 - _sin8(5) * y3, -e3[1] - _cos8(5) * y3 - _sin8(5) * x3
    re[7], im[7] = e1[0] + _cos8(7) * x1 - _sin8(7) * y1, -e1[1] - _cos8(7) * y1 - _sin8(7) * x1
    t1, t2 = _RSQRT2 * (bo - do), _RSQRT2 * (bo + do)
    re[2], im[2] = be + t1, -de - t2
    re[6], im[6] = be - t1, de - t2
    return e0 + o0, e0 - o0, re, im


def _dft16_inv(h0, h8, a, b):
    p, q = h0 + h8, h0 - h8
    pa, pm, qm, qp = p + a[4], p - a[4], q - b[4], q + b[4]
    sa, da, sb, db = a[2] + a[6], a[2] - a[6], b[2] + b[6], b[2] - b[6]
    m1, m2 = _RSQRT2 * (da - sb), _RSQRT2 * (da + sb)
    e1, e2, e3, e4 = a[1] - a[7], a[3] - a[5], a[1] + a[7], a[3] + a[5]
    g1, g2, g3, g4 = b[1] + b[7], b[3] + b[5], b[1] - b[7], b[3] - b[5]
    ca1, sb1 = e1 * _cos8(1) + e2 * _cos8(3), g1 * _sin8(1) + g2 * _sin8(3)
    ca2, sb2 = _RSQRT2 * (e3 - e4), _RSQRT2 * (g3 + g4)
    ca3, sb3 = e1 * _cos8(3) - e2 * _cos8(1), g1 * _sin8(3) - g2 * _sin8(1)
    return [pa + sa + (e3 + e4), qm + m1 + (ca1 - sb1), pm - db + (ca2 - sb2), qp - m2 + (ca3 - sb3),
            pa - sa + (g4 - g3), qm - m1 - (ca3 + sb3), pm + db - (ca2 + sb2), qp + m2 - (ca1 + sb1)]


def _stage1_fwd(src_ref, dst_ref, n2, rb, lt):
    cb = src_ref.shape[1]

    def body(i, carry):
        r0 = i * rb
        at = lambda off: pl.ds(pl.multiple_of(off + r0, rb), rb)
        for l0 in range(0, cb, lt):
            lanes = slice(l0, l0 + lt)
            g0, g8, re, im = _dft16_fwd([src_ref[at(n2 * t1), lanes] for t1 in range(8)])
            dst_ref[at(_packed_start(0, n2)), lanes] = g0.astype(dst_ref.dtype)
            dst_ref[at(_packed_start(8, n2)), lanes] = g8.astype(dst_ref.dtype)
            for g in range(1, 8):
                dst_ref[at(_packed_start(g, n2)), lanes] = re[g].astype(dst_ref.dtype)
                dst_ref[at(_packed_start(g, n2) + n2), lanes] = im[g].astype(dst_ref.dtype)
        return carry

    lax.fori_loop(0, n2 // rb, body, 0)


def _stage2_fwd(g, tr_ref, tc_ref, src_ref, n2):
    start = _packed_start(g, n2)
    if g in (0, 8):
        return jnp.dot(tr_ref[0 if g == 0 else 1], src_ref[start:start + n2, :],
                       preferred_element_type=F32)
    return jnp.dot(tc_ref[g - 1], src_ref[start:start + 2 * n2, :], preferred_element_type=F32)


def _dft_stage_tables(seq):
    n = 2 * seq
    n2 = n // 16
    f2 = jnp.arange(n2, dtype=jnp.int32)[:, None]
    t2 = jnp.arange(n2, dtype=jnp.int32)[None, :]

    def cs(g):
        ang = ((t2 * (g + 16 * f2)) & (n - 1)).astype(F32) * (2.0 * math.pi / n)
        return jnp.cos(ang), jnp.sin(ang)

    real = jnp.stack([jnp.concatenate([c, -s], axis=0) for c, s in (cs(0), cs(8))])
    cplx = jnp.stack([jnp.block([[c, s], [-s, c]]) for c, s in (cs(g) for g in range(1, 8))])
    return (real.astype(BF16), cplx.astype(BF16),
            jnp.swapaxes(real, 1, 2).astype(BF16), jnp.swapaxes(cplx, 1, 2).astype(BF16))


def _filter_spec_kernel(h_ref, wf_ref, wb_ref, t_ref, dl_ref, tr_ref, tc_ref, k_ref,
                        e_scr, o_scr, ge_scr, go_scr, *, rb, lt):
    hp = lax.Precision.HIGHEST
    seq = h_ref.shape[0]
    n2 = seq // 8
    window = jnp.exp(-t_ref[...] * dl_ref[...])
    hf = jnp.dot(h_ref[...], wf_ref[...], preferred_element_type=F32, precision=hp) * window
    hb = jnp.dot(h_ref[...], wb_ref[...], preferred_element_type=F32, precision=hp) * window
    e_scr[...] = hf + hb
    o_scr[...] = hf - hb
    _stage1_fwd(e_scr, ge_scr, n2, rb, lt)
    _stage1_fwd(o_scr, go_scr, n2, rb, lt)
    for g in range(DFT_GROUPS):
        wgt = (1.0 if g in (0, 8) else 2.0) / (2 * seq)
        rows = slice(2 * n2 * g, 2 * n2 * g + n2)
        rows_im = slice(2 * n2 * g + n2, 2 * n2 * (g + 1))
        k_ref[0, rows, :] = _stage2_fwd(g, tr_ref, tc_ref, ge_scr, n2)[:n2] * wgt
        k_ref[0, rows_im, :] = _stage2_fwd(g, tr_ref, tc_ref, go_scr, n2)[n2:] * wgt


def _filter_spec_call(h2, w3, tcol, deltas, t_real, t_cplx, width):
    seq, hid = h2.shape
    n2 = seq // 8
    cb = MXU_DIM
    ncb = width // cb
    per_dir = HY_ORDER * ncb
    rows = DFT_GROUPS * 2 * n2
    const = lambda n, k: (0, 0, 0)
    return pl.pallas_call(
        functools.partial(_filter_spec_kernel, rb=16, lt=LANE),
        grid=(HY_ORDER, ncb),
        in_specs=[_resident((seq, hid), lambda n, k: (0, 0)),
                  pl.BlockSpec((hid, cb), lambda n, k: (0, n * ncb + k)),
                  pl.BlockSpec((hid, cb), lambda n, k: (0, per_dir + n * ncb + k)),
                  _resident((seq, 1), lambda n, k: (0, 0)),
                  pl.BlockSpec((1, cb), lambda n, k: (0, k)),
                  _resident(t_real.shape, const), _resident(t_cplx.shape, const)],
        out_specs=pl.BlockSpec((1, rows, cb), lambda n, k: (n, 0, k)),
        out_shape=jax.ShapeDtypeStruct((HY_ORDER, rows, width), F32),
        scratch_shapes=[pltpu.VMEM((seq, cb), F32), pltpu.VMEM((seq, cb), F32),
                        pltpu.VMEM((2 * seq, cb), BF16), pltpu.VMEM((2 * seq, cb), BF16)],
        compiler_params=_cparams(("arbitrary", "arbitrary")),
        name="hyena_filter_spectrum",
    )(h2, w3, w3, tcol, deltas, t_real, t_cplx)


def _short_conv_into(dst_ref, z_ref, w_ref, b_ref):
    seq = z_ref.shape[0]
    z = z_ref[...].astype(F32)
    w = w_ref[...]
    dst_ref[...] = (w[0:1] * pltpu.roll(z, 1, 0) + w[1:2] * z
                    + w[2:3] * pltpu.roll(z, seq - 1, 0) + b_ref[...])
    dst_ref[0:1, :] -= w[0:1] * z_ref[seq - 1:seq, :].astype(F32)
    dst_ref[seq - 1:seq, :] -= w[2:3] * z_ref[0:1, :].astype(F32)


def _hyena_kernel(v_ref, x1_ref, x2_ref, wv_ref, w1_ref, w2_ref, bv_ref, b1_ref, b2_ref,
                  skip_ref, k_ref, tr_ref, tc_ref, ur_ref, uc_ref, o_ref,
                  y_scr, gate_scr, g_scr, yc_scr, h_scr, *, rb, lt):
    seq, cb = v_ref.shape
    n2 = seq // 8

    _short_conv_into(y_scr, v_ref, wv_ref, bv_ref)
    _short_conv_into(gate_scr.at[0], x1_ref, w1_ref, b1_ref)
    _short_conv_into(gate_scr.at[1], x2_ref, w2_ref, b2_ref)

    for n in range(HY_ORDER):
        last = n + 1 == HY_ORDER
        _stage1_fwd(y_scr, g_scr, n2, rb, lt)
        for g in range(DFT_GROUPS):
            x = _stage2_fwd(g, tr_ref, tc_ref, g_scr, n2)
            xr, xi = x[:n2], x[n2:]
            base = 2 * n2 * g
            kr = k_ref[n, base:base + n2, :]
            ki = k_ref[n, base + n2:base + 2 * n2, :]
            yc_scr[base:base + n2, :] = (xr * kr - xi * ki).astype(BF16)
            yc_scr[base + n2:base + 2 * n2, :] = (xr * ki + xi * kr).astype(BF16)
        for g in range(DFT_GROUPS):
            start = _packed_start(g, n2)
            ycat = yc_scr[2 * n2 * g:2 * n2 * (g + 1), :]
            if g in (0, 8):
                h_scr[start:start + n2, :] = jnp.dot(ur_ref[0 if g == 0 else 1], ycat,
                                                     preferred_element_type=F32)
            else:
                h_scr[start:start + 2 * n2, :] = jnp.dot(uc_ref[g - 1], ycat,
                                                         preferred_element_type=F32)

        def body(i, carry):
            r0 = i * rb
            at = lambda off: pl.ds(pl.multiple_of(off + r0, rb), rb)
            for l0 in range(0, cb, lt):
                lanes = slice(l0, l0 + lt)
                a = {g: h_scr[at(_packed_start(g, n2)), lanes] for g in range(1, 8)}
                b = {g: h_scr[at(_packed_start(g, n2) + n2), lanes] for g in range(1, 8)}
                conv = _dft16_inv(h_scr[at(_packed_start(0, n2)), lanes],
                                  h_scr[at(_packed_start(8, n2)), lanes], a, b)
                skip = skip_ref[n:n + 1, lanes]
                for t1 in range(8):
                    rows = at(n2 * t1)
                    y_new = gate_scr[n, rows, lanes] * (conv[t1] + y_scr[rows, lanes] * skip)
                    if last:
                        o_ref[rows, lanes] = y_new.astype(o_ref.dtype)
                    else:
                        y_scr[rows, lanes] = y_new
            return carry

        lax.fori_loop(0, n2 // rb, body, 0)


def _hyena_call(z, conv_w, conv_b, skip, kspec, tables, bsz, seq, width, col0):
    cb = MXU_DIM
    ncb = width // cb
    base = col0 // cb
    zspec = lambda part: pl.BlockSpec((seq, cb), lambda k, b: (b, base + part * ncb + k))
    wspec = lambda part: pl.BlockSpec((HY_SHORT, cb), lambda k, b: (0, part * ncb + k))
    bspec = lambda part: pl.BlockSpec((1, cb), lambda k, b: (0, part * ncb + k))
    cb_all = conv_b.reshape(1, -1)
    const = lambda k, b: (0, 0, 0)
    return pl.pallas_call(
        functools.partial(_hyena_kernel, rb=16, lt=LANE),
        grid=(ncb, bsz),
        in_specs=[zspec(0), zspec(1), zspec(2), wspec(0), wspec(1), wspec(2),
                  bspec(0), bspec(1), bspec(2),
                  pl.BlockSpec((HY_ORDER, cb), lambda k, b: (0, k)),
                  _resident((HY_ORDER, kspec.shape[1], cb), lambda k, b: (0, 0, k))]
                 + [_resident(t.shape, const) for t in tables],
        out_specs=pl.BlockSpec((seq, cb), lambda k, b: (b, k)),
        out_shape=jax.ShapeDtypeStruct((bsz * seq, width), BF16),
        scratch_shapes=[pltpu.VMEM((seq, cb), F32),
                        pltpu.VMEM((HY_ORDER, seq, cb), F32),
                        pltpu.VMEM((2 * seq, cb), BF16),
                        pltpu.VMEM((kspec.shape[1], cb), BF16),
                        pltpu.VMEM((2 * seq, cb), F32)],
        compiler_params=_cparams(("arbitrary", "arbitrary")),
        name="hyena_mixer",
    )(z, z, z, conv_w, conv_w, conv_w, cb_all, cb_all, cb_all, skip, kspec, *tables)


def _merge_kernel(u_ref, vn_ref, yh_ref, ga_ref, gb_ref, ws_ref, bs_ref, wa_ref, wb_ref, o_ref,
                  s_scr):
    tm = u_ref.shape[0]
    gd = u_ref.shape[1] // SGU_GROUPS
    for n in range(tm // SGU_CHUNK):
        rows = slice(n * SGU_CHUNK, (n + 1) * SGU_CHUNK)
        for g in range(SGU_GROUPS):
            cols = slice(g * gd, (g + 1) * gd)
            mixed = jnp.dot(ws_ref[g], vn_ref[rows, cols], preferred_element_type=F32)
            s_scr[rows, cols] = (u_ref[rows, cols].astype(F32)
                                 * (mixed + bs_ref[:, cols])).astype(BF16)
    y_a = jnp.dot(s_scr[...], wa_ref[...], preferred_element_type=F32)
    y_b = jnp.dot(yh_ref[...], wb_ref[...], preferred_element_type=F32)
    o_ref[...] = (ga_ref[...].astype(F32) * y_a + gb_ref[...].astype(F32) * y_b).astype(o_ref.dtype)


def _merge_call(z, yh, sgu_w16, sgu_bfull, w_a, w_b, d, gate_blk_cols):
    t = z.shape[0]
    wdt = w_a.shape[0]
    tm = 512
    ga = gate_blk_cols // d
    return pl.pallas_call(
        _merge_kernel,
        grid=(t // tm,),
        in_specs=[pl.BlockSpec((tm, wdt), lambda i: (i, 0)),
                  pl.BlockSpec((tm, wdt), lambda i: (i, 1)),
                  pl.BlockSpec((tm, wdt), lambda i: (i, 0)),
                  pl.BlockSpec((tm, d), lambda i: (i, ga)),
                  pl.BlockSpec((tm, d), lambda i: (i, ga + 1)),
                  _resident(sgu_w16.shape, lambda i: (0, 0, 0)),
                  _resident(sgu_bfull.shape, lambda i: (0, 0)),
                  _resident(w_a.shape, lambda i: (0, 0)),
                  _resident(w_b.shape, lambda i: (0, 0))],
        out_specs=pl.BlockSpec((tm, d), lambda i: (i, 0)),
        out_shape=jax.ShapeDtypeStruct((t, d), BF16),
        scratch_shapes=[pltpu.VMEM((tm, wdt), BF16)],
        compiler_params=_cparams(("parallel",)),
        name="gated_merge",
    )(z, z, yh, z, z, sgu_w16, sgu_bfull, w_a, w_b)


def _oproj_kernel(m_ref, x_ref, g1_ref, w_ref, b_ref, lg_ref, lb_ref, o_ref, *, alpha):
    mix = jnp.dot(m_ref[...], w_ref[...], preferred_element_type=F32) + b_ref[...]
    o_ref[...] = _layer_norm(alpha * x_ref[...] + g1_ref[0] * mix, lg_ref[...], lb_ref[...])


def _oproj_call(m, xt, mod3, w_o, b_o, ln_g, ln_b, seq, alpha):
    t, d = xt.shape
    tm = 512
    per_b = seq // tm
    row = lambda i: (0, 0)
    return pl.pallas_call(
        functools.partial(_oproj_kernel, alpha=alpha),
        grid=(t // tm,),
        in_specs=[pl.BlockSpec((tm, d), lambda i: (i, 0)),
                  pl.BlockSpec((tm, d), lambda i: (i, 0)),
                  pl.BlockSpec((1, 1, d), lambda i: (i // per_b, 0, 2)),
                  _resident((d, d), row),
                  pl.BlockSpec((1, d), row), pl.BlockSpec((1, d), row), pl.BlockSpec((1, d), row)],
        out_specs=pl.BlockSpec((tm, d), lambda i: (i, 0)),
        out_shape=jax.ShapeDtypeStruct((t, d), F32),
        compiler_params=_cparams(("parallel",)),
        name="out_proj_ln",
    )(m, xt, mod3, w_o, b_o.reshape(1, d), ln_g.reshape(1, d), ln_b.reshape(1, d))


def _mlp_kernel(x_ref, sc_ref, sh_ref, g2_ref, w1_ref, b1_ref, w2_ref, b2_ref, lg_ref, lb_ref,
                o_ref, h_scr, acc_scr, *, alpha):
    j = pl.program_id(1)

    @pl.when(j == 0)
    def _():
        h_scr[...] = (x_ref[...] * (1.0 + sc_ref[0]) + sh_ref[0]).astype(BF16)
        acc_scr[...] = jnp.zeros_like(acc_scr)

    hid = jnp.dot(h_scr[...], w1_ref[...], preferred_element_type=F32) + b1_ref[...]
    hid = jnp.square(jnp.maximum(hid, 0.0)).astype(BF16)
    acc_scr[...] += jnp.dot(hid, w2_ref[...], preferred_element_type=F32)

    @pl.when(j == pl.num_programs(1) - 1)
    def _():
        f = acc_scr[...] + b2_ref[...]
        o_ref[...] = _layer_norm(alpha * x_ref[...] + g2_ref[0] * f, lg_ref[...], lb_ref[...])


def _mlp_call(x1, mod3, w_m1, b_m1, w_m2, b_m2, ln_g, ln_b, seq, alpha):
    t, d = x1.shape
    dff = w_m1.shape[1]
    tm = 512
    tf = 1024
    per_b = seq // tm
    row = lambda i, j: (0, 0)
    return pl.pallas_call(
        functools.partial(_mlp_kernel, alpha=alpha),
        grid=(t // tm, dff // tf),
        in_specs=[pl.BlockSpec((tm, d), lambda i, j: (i, 0)),
                  pl.BlockSpec((1, 1, d), lambda i, j: (i // per_b, 0, 4)),
                  pl.BlockSpec((1, 1, d), lambda i, j: (i // per_b, 0, 3)),
                  pl.BlockSpec((1, 1, d), lambda i, j: (i // per_b, 0, 5)),
                  pl.BlockSpec((d, tf), lambda i, j: (0, j)),
                  pl.BlockSpec((1, tf), lambda i, j: (0, j)),
                  pl.BlockSpec((tf, d), lambda i, j: (j, 0)),
                  pl.BlockSpec((1, d), row), pl.BlockSpec((1, d), row), pl.BlockSpec((1, d), row)],
        out_specs=pl.BlockSpec((tm, d), lambda i, j: (i, 0)),
        out_shape=jax.ShapeDtypeStruct((t, d), F32),
        scratch_shapes=[pltpu.VMEM((tm, d), BF16), pltpu.VMEM((tm, d), F32)],
        compiler_params=_cparams(("parallel", "arbitrary")),
        name="mlp_ln",
    )(x1, mod3, mod3, mod3, w_m1, b_m1.reshape(1, dff), w_m2, b_m2.reshape(1, d),
      ln_g.reshape(1, d), ln_b.reshape(1, d))


def _filter_features(seq):
    bands_n = (HY_EMB - 1) // 2
    t = jnp.linspace(0.0, 1.0, seq, dtype=F32)[:, None]
    omega = 2.0 * math.pi * jnp.arange(seq, dtype=F32)[:, None] / seq
    bands = jnp.linspace(1e-4, bands_n - 1, bands_n, dtype=F32)[None, :]
    feats = jnp.concatenate([t, jnp.cos(bands * omega), -jnp.sin(bands * omega)], axis=-1)
    return t, feats


def _decay_rates(width):
    min_decay = math.log(HY_DECAY_TARGET) / HY_SLOW_DECAY
    max_decay = math.log(HY_DECAY_TARGET) / HY_FAST_DECAY
    return jnp.abs(jnp.linspace(min_decay, max_decay, width, dtype=F32))[None, :]


def kernel(x, c, w_ada, b_ada, w_in, b_in, sgu_ln_g, sgu_ln_b, sgu_w, sgu_b, hy_conv_w, hy_conv_b, hy_w1, hy_b1, hy_w2, hy_b2, hy_freq, hy_w3, hy_skip, w_branch_a, w_branch_b, w_o, b_o, ln1_g, ln1_b, w_m1, b_m1, w_m2, b_m2, ln2_g, ln2_b):
    bsz, seq, d = x.shape
    depth = w_ada.shape[0]
    alpha = (2.0 * depth) ** 0.25
    sgu_width = w_branch_a.shape[1]
    hy_width = w_branch_b.shape[1]
    assert seq % SGU_CHUNK == 0 and sgu_w.shape[-1] == SGU_CHUNK

    assert seq % (8 * MXU_DIM) == 0
    tables = _dft_stage_tables(seq)
    tcol, feats = _filter_features(seq)
    deltas = _decay_rates(hy_width)

    xt = x.reshape(bsz * seq, d)
    for l in range(depth):
        mod3 = _mod_call(c, w_ada[l], b_ada[l]).reshape(bsz, 1, N_MOD * d)

        z, gate_blk = _inproj_call(xt, mod3, w_in[l].astype(BF16), b_in[l], sgu_ln_g[l], sgu_ln_b[l],
                                   seq, sgu_width, hy_width)

        h2 = _filter_mlp_call(feats, hy_w1[l], hy_b1[l], hy_w2[l], hy_b2[l], hy_freq[l])
        kspec = _filter_spec_call(h2, hy_w3[l], tcol, deltas, tables[0], tables[1], hy_width)
        yh = _hyena_call(z, hy_conv_w[l], hy_conv_b[l], hy_skip[l], kspec, tables,
                         bsz, seq, hy_width, 2 * sgu_width)

        gd = sgu_width // SGU_GROUPS
        bs_full = jnp.repeat(sgu_b[l].T, gd, axis=1)
        m = _merge_call(z, yh, sgu_w[l].astype(BF16), bs_full, w_branch_a[l].astype(BF16),
                        w_branch_b[l].astype(BF16), d, gate_blk * sgu_width)

        x1 = _oproj_call(m, xt, mod3, w_o[l].astype(BF16), b_o[l], ln1_g[l], ln1_b[l], seq, alpha)
        xt = _mlp_call(x1, mod3, w_m1[l].astype(BF16), b_m1[l], w_m2[l].astype(BF16), b_m2[l],
                       ln2_g[l], ln2_b[l], seq, alpha)
    return xt.reshape(bsz, seq, d)
```

```python
import functools
import math

import jax
import jax.numpy as jnp
from jax import lax
from jax.experimental import pallas as pl
from jax.experimental.pallas import tpu as pltpu

F32 = jnp.float32
BF16 = jnp.bfloat16

SGU_CHUNK = 128
SGU_GROUPS = 8
HY_ORDER = 2
HY_SHORT = 3
HY_EMB = 33
HY_DECAY_TARGET = 1e-2
HY_FAST_DECAY = 0.3
HY_SLOW_DECAY = 1.5
N_MOD = 6
LN_EPS = 1e-5

LANE = 128
MXU_DIM = 256
VMEM_LIMIT = 48 * 1024 * 1024


def _cparams(sem):
    return pltpu.CompilerParams(dimension_semantics=sem, vmem_limit_bytes=VMEM_LIMIT)


def _resident(block_shape, index_map):
    return pl.BlockSpec(block_shape, index_map, pipeline_mode=pl.Buffered(1))


def _gelu(x):
    return 0.5 * x * (1.0 + lax.erf(x * (1.0 / math.sqrt(2.0))))


def _layer_norm(x, g, b):
    mu = jnp.mean(x, axis=-1, keepdims=True)
    xc = x - mu
    var = jnp.mean(xc * xc, axis=-1, keepdims=True)
    return xc * lax.rsqrt(var + LN_EPS) * g + b


def _dot_bf16x3(a, b):
    m = a.shape[0]
    a_hi = a.astype(BF16)
    b_hi = b.astype(BF16)
    a_hi32 = a_hi.astype(F32)
    b_lo = (b - b_hi.astype(F32)).astype(BF16)
    stacked = jnp.concatenate([a_hi32, a - a_hi32], axis=0).astype(BF16)
    top = jnp.dot(stacked, b_hi, preferred_element_type=F32)
    return top[:m] + top[m:] + jnp.dot(a_hi, b_lo, preferred_element_type=F32)


def _mod_kernel(c_ref, w_ref, b_ref, o_ref):
    c = c_ref[...]
    cond = c * jax.nn.sigmoid(c)
    o_ref[...] = _dot_bf16x3(cond, w_ref[...]) + b_ref[...]


def _mod_call(c, w_ada, b_ada):
    bsz, d = c.shape
    n = w_ada.shape[1]
    tn = 1024
    return pl.pallas_call(
        _mod_kernel,
        grid=(n // tn,),
        in_specs=[pl.BlockSpec((bsz, d), lambda j: (0, 0)),
                  pl.BlockSpec((d, tn), lambda j: (0, j)),
                  pl.BlockSpec((1, tn), lambda j: (0, j))],
        out_specs=pl.BlockSpec((bsz, tn), lambda j: (0, j)),
        out_shape=jax.ShapeDtypeStruct((bsz, n), F32),
        compiler_params=_cparams(("arbitrary",)),
        name="adaln_mod",
    )(c, w_ada, b_ada.reshape(1, n))


def _row_chunks(total, size):
    return [slice(r, r + size) for r in range(0, total, size)]


def _inproj_kernel(x_ref, sc_ref, sh_ref, w_ref, b_ref, g_ref, be_ref, o_ref, h_scr, *, seg, rc):
    j = pl.program_id(1)
    chunks = _row_chunks(x_ref.shape[0], rc)

    def emit(act, modulate=False):
        for rows in chunks:
            if modulate:
                h_scr[rows, :] = (x_ref[rows, :] * (1.0 + sc_ref[0]) + sh_ref[0]).astype(BF16)
            acc = jnp.dot(h_scr[rows, :], w_ref[...], preferred_element_type=F32) + b_ref[...]
            o_ref[rows, :] = act(acc).astype(o_ref.dtype)

    @pl.when(j == 0)
    def _():
        emit(jax.nn.sigmoid, modulate=True)

    @pl.when(jnp.logical_and(j > 0, j < seg[0]))
    def _():
        emit(jax.nn.sigmoid)

    @pl.when(jnp.logical_and(j >= seg[0], j < seg[1]))
    def _():
        emit(_gelu)

    @pl.when(jnp.logical_and(j >= seg[1], j < seg[2]))
    def _():
        emit(lambda a: _layer_norm(_gelu(a), g_ref[...], be_ref[...]))

    @pl.when(j >= seg[2])
    def _():
        emit(lambda a: a)


def _inproj_call(xt, mod3, w_perm, b_perm, ln_g, ln_b, seq, sgu_w, hy_w):
    t, d = xt.shape
    n = w_perm.shape[1]
    tm = 1024
    tn = sgu_w
    assert hy_w == sgu_w and seq % tm == 0 and d % tn == 0
    n_gate = 2 * d // tn
    seg = (n_gate, n_gate + 1, n_gate + 2)
    per_b = seq // tm
    return pl.pallas_call(
        functools.partial(_inproj_kernel, seg=seg, rc=256),
        grid=(t // tm, n // tn),
        in_specs=[pl.BlockSpec((tm, d), lambda i, j: (i, 0)),
                  pl.BlockSpec((1, 1, d), lambda i, j: (i // per_b, 0, 1)),
                  pl.BlockSpec((1, 1, d), lambda i, j: (i // per_b, 0, 0)),
                  pl.BlockSpec((d, tn), lambda i, j: (0, j)),
                  pl.BlockSpec((1, tn), lambda i, j: (0, j)),
                  pl.BlockSpec((1, tn), lambda i, j: (0, 0)),
                  pl.BlockSpec((1, tn), lambda i, j: (0, 0))],
        out_specs=pl.BlockSpec((tm, tn), lambda i, j: (i, j)),
        out_shape=jax.ShapeDtypeStruct((t, n), BF16),
        scratch_shapes=[pltpu.VMEM((tm, d), BF16)],
        compiler_params=_cparams(("parallel", "arbitrary")),
        name="in_proj",
    )(xt, mod3, mod3, w_perm, b_perm.reshape(1, n), ln_g.reshape(1, tn), ln_b.reshape(1, tn))


def _filter_mlp_kernel(f_ref, w1_ref, b1_ref, w2_ref, b2_ref, fr_ref, o_ref):
    hp = lax.Precision.HIGHEST
    fr = fr_ref[...]
    h = jnp.sin(fr * (jnp.dot(f_ref[...], w1_ref[...], preferred_element_type=F32, precision=hp)
                      + b1_ref[...]))
    o_ref[...] = jnp.sin(fr * (jnp.dot(h, w2_ref[...], preferred_element_type=F32, precision=hp)
                               + b2_ref[...]))


def _filter_mlp_call(feats, w1, b1, w2, b2, freq):
    seq = feats.shape[0]
    hid = w1.shape[1]
    return pl.pallas_call(
        _filter_mlp_kernel,
        out_shape=jax.ShapeDtypeStruct((seq, hid), F32),
        name="hyena_filter_mlp",
    )(feats, w1, b1.reshape(1, hid), w2, b2.reshape(1, hid), freq.reshape(1, hid))


DFT_GROUPS = 9
_RSQRT2 = 1.0 / math.sqrt(2.0)


def _cos8(k):
    return math.cos(math.pi * k / 8.0)


def _sin8(k):
    return math.sin(math.pi * k / 8.0)


def _packed_start(g, n2):
    return {0: 0, 8: n2}.get(g, 2 * n2 * g)


def _dft16_fwd(u):
    def half(p0, p1, p2, p3):
        a, b, c, d = p0 + p2, p0 - p2, p1 + p3, p1 - p3
        rd, rc = _RSQRT2 * d, _RSQRT2 * c
        return a + c, a - c, b, d, (p0 + rd, -(rc + p2)), (p0 - rd, p2 - rc)

    e0, e4, be, de, e1, e3 = half(u[0], u[2], u[4], u[6])
    o0, o4, bo, do, (x1, y1), (x3, y3) = half(u[1], u[3], u[5], u[7])
    re, im = {}, {}
    re[4], im[4] = e4, -o4
    re[1], im[1] = e1[0] + _cos8(1) * x1 + _sin8(1) * y1, e1[1] + _cos8(1) * y1 - _sin8(1) * x1
    re[3], im[3] = e3[0] + _cos8(3) * x3 + _sin8(3) * y3, e3[1] + _cos8(3) * y3 - _sin8(3) * x3
    re[5], im[5] = e3[0] + _cos8(5) * x3 - _sin8(5) * y3, -e3[1] - _cos8(5) * y3 - _sin8(5) * x3
    re[7], im[7] = e1[0] + _cos8(7) * x1 - _sin8(7) * y1, -e1[1] - _cos8(7) * y1 - _sin8(7) * x1
    t1, t2 = _RSQRT2 * (bo - do), _RSQRT2 * (bo + do)
    re[2], im[2] = be + t1, -de - t2
    re[6], im[6] = be - t1, de - t2
    return e0 + o0, e0 - o0, re, im


def _dft16_inv(h0, h8, a, b):
    p, q = h0 + h8, h0 - h8
    pa, pm, qm, qp = p + a[4], p - a[4], q - b[4], q + b[4]
    sa, da, sb, db = a[2] + a[6], a[2] - a[6], b[2] + b[6], b[2] - b[6]
    m1, m2 = _RSQRT2 * (da - sb), _RSQRT2 * (da + sb)
    e1, e2, e3, e4 = a[1] - a[7], a[3] - a[5], a[1] + a[7], a[3] + a[5]
    g1, g2, g3, g4 = b[1] + b[7], b[3] + b[5], b[1] - b[7], b[3] - b[5]
    ca1, sb1 = e1 * _cos8(1) + e2 * _cos8(3), g1 * _sin8(1) + g2 * _sin8(3)
    ca2, sb2 = _RSQRT2 * (e3 - e4), _RSQRT2 * (g3 + g4)
    ca3, sb3 = e1 * _cos8(3) - e2 * _cos8(1), g1 * _sin8(3) - g2 * _sin8(1)
    return [pa + sa + (e3 + e4), qm + m1 + (ca1 - sb1), pm - db + (ca2 - sb2), qp - m2 + (ca3 - sb3),
            pa - sa + (g4 - g3), qm - m1 - (ca3 + sb3), pm + db - (ca2 + sb2), qp + m2 - (ca1 + sb1)]


def _stage1_fwd(src_ref, dst_ref, n2, rb, lt):
    cb = src_ref.shape[1]

    def body(i, carry):
        r0 = i * rb
        at = lambda off: pl.ds(pl.multiple_of(off + r0, rb), rb)
        for l0 in range(0, cb, lt):
            lanes = slice(l0, l0 + lt)
            g0, g8, re, im = _dft16_fwd([src_ref[at(n2 * t1), lanes] for t1 in range(8)])
            dst_ref[at(_packed_start(0, n2)), lanes] = g0.astype(dst_ref.dtype)
            dst_ref[at(_packed_start(8, n2)), lanes] = g8.astype(dst_ref.dtype)
            for g in range(1, 8):
                dst_ref[at(_packed_start(g, n2)), lanes] = re[g].astype(dst_ref.dtype)
                dst_ref[at(_packed_start(g, n2) + n2), lanes] = im[g].astype(dst_ref.dtype)
        return carry

    lax.fori_loop(0, n2 // rb, body, 0)


def _stage2_fwd(g, tr_ref, tc_ref, src_ref, n2):
    start = _packed_start(g, n2)
    if g in (0, 8):
        return jnp.dot(tr_ref[0 if g == 0 else 1], src_ref[start:start + n2, :],
                       preferred_element_type=F32)
    return jnp.dot(tc_ref[g - 1], src_ref[start:start + 2 * n2, :], preferred_element_type=F32)


def _dft_stage_tables(seq):
    n = 2 * seq
    n2 = n // 16
    f2 = jnp.arange(n2, dtype=jnp.int32)[:, None]
    t2 = jnp.arange(n2, dtype=jnp.int32)[None, :]

    def cs(g):
        ang = ((t2 * (g + 16 * f2)) & (n - 1)).astype(F32) * (2.0 * math.pi / n)
        return jnp.cos(ang), jnp.sin(ang)

    real = jnp.stack([jnp.concatenate([c, -s], axis=0) for c, s in (cs(0), cs(8))])
    cplx = jnp.stack([jnp.block([[c, s], [-s, c]]) for c, s in (cs(g) for g in range(1, 8))])
    return (real.astype(BF16), cplx.astype(BF16),
            jnp.swapaxes(real, 1, 2).astype(BF16), jnp.swapaxes(cplx, 1, 2).astype(BF16))


def _filter_spec_kernel(h_ref, wf_ref, wb_ref, t_ref, dl_ref, tr_ref, tc_ref, k_ref,
                        e_scr, o_scr, ge_scr, go_scr, *, rb, lt):
    seq = h_ref.shape[0]
    n2 = seq // 8
    window = jnp.exp(-t_ref[...] * dl_ref[...])
    e_scr[...] = _dot_bf16x3(h_ref[...], wf_ref[...] + wb_ref[...]) * window
    o_scr[...] = _dot_bf16x3(h_ref[...], wf_ref[...] - wb_ref[...]) * window
    _stage1_fwd(e_scr, ge_scr, n2, rb, lt)
    _stage1_fwd(o_scr, go_scr, n2, rb, lt)
    for g in range(DFT_GROUPS):
        wgt = (1.0 if g in (0, 8) else 2.0) / (2 * seq)
        rows = slice(2 * n2 * g, 2 * n2 * g + n2)
        rows_im = slice(2 * n2 * g + n2, 2 * n2 * (g + 1))
        k_ref[0, rows, :] = _stage2_fwd(g, tr_ref, tc_ref, ge_scr, n2)[:n2] * wgt
        k_ref[0, rows_im, :] = _stage2_fwd(g, tr_ref, tc_ref, go_scr, n2)[n2:] * wgt


def _filter_spec_call(h2, w3, tcol, deltas, t_real, t_cplx, width):
    seq, hid = h2.shape
    n2 = seq // 8
    cb = MXU_DIM
    ncb = width // cb
    per_dir = HY_ORDER * ncb
    rows = DFT_GROUPS * 2 * n2
    const = lambda n, k: (0, 0, 0)
    return pl.pallas_call(
        functools.partial(_filter_spec_kernel, rb=16, lt=LANE),
        grid=(HY_ORDER, ncb),
        in_specs=[_resident((seq, hid), lambda n, k: (0, 0)),
                  pl.BlockSpec((hid, cb), lambda n, k: (0, n * ncb + k)),
                  pl.BlockSpec((hid, cb), lambda n, k: (0, per_dir + n * ncb + k)),
                  _resident((seq, 1), lambda n, k: (0, 0)),
                  pl.BlockSpec((1, cb), lambda n, k: (0, k)),
                  _resident(t_real.shape, const), _resident(t_cplx.shape, const)],
        out_specs=pl.BlockSpec((1, rows, cb), lambda n, k: (n, 0, k)),
        out_shape=jax.ShapeDtypeStruct((HY_ORDER, rows, width), F32),
        scratch_shapes=[pltpu.VMEM((seq, cb), F32), pltpu.VMEM((seq, cb), F32),
                        pltpu.VMEM((2 * seq, cb), BF16), pltpu.VMEM((2 * seq, cb), BF16)],
        compiler_params=_cparams(("arbitrary", "arbitrary")),
        name="hyena_filter_spectrum",
    )(h2, w3, w3, tcol, deltas, t_real, t_cplx)


def _short_conv_into(dst_ref, z_ref, w_ref, b_ref):
    seq = z_ref.shape[0]
    z = z_ref[...].astype(F32)
    w = w_ref[...]
    dst_ref[...] = (w[0:1] * pltpu.roll(z, 1, 0) + w[1:2] * z
                    + w[2:3] * pltpu.roll(z, seq - 1, 0) + b_ref[...])
    dst_ref[0:1, :] -= w[0:1] * z_ref[seq - 1:seq, :].astype(F32)
    dst_ref[seq - 1:seq, :] -= w[2:3] * z_ref[0:1, :].astype(F32)


def _hyena_kernel(v_ref, x1_ref, x2_ref, wv_ref, w1_ref, w2_ref, bv_ref, b1_ref, b2_ref,
                  skip_ref, k_ref, tr_ref, tc_ref, ur_ref, uc_ref, o_ref,
                  y_a_scr, y_b_scr, gate_scr, g_scr, yc_scr, h_scr, *, rb, lt):
    seq, cb = v_ref.shape
    n2 = seq // 8

    _short_conv_into(y_a_scr, v_ref, wv_ref, bv_ref)
    _short_conv_into(gate_scr.at[0], x1_ref, w1_ref, b1_ref)
    _short_conv_into(gate_scr.at[1], x2_ref, w2_ref, b2_ref)

    for n in range(HY_ORDER):
        last = n + 1 == HY_ORDER
        y_scr, y_next = (y_a_scr, y_b_scr) if n % 2 == 0 else (y_b_scr, y_a_scr)
        _stage1_fwd(y_scr, g_scr, n2, rb, lt)
        for g in range(DFT_GROUPS):
            x = _stage2_fwd(g, tr_ref, tc_ref, g_scr, n2)
            xr, xi = x[:n2], x[n2:]
            base = 2 * n2 * g
            kr = k_ref[n, base:base + n2, :]
            ki = k_ref[n, base + n2:base + 2 * n2, :]
            yc_scr[base:base + n2, :] = (xr * kr - xi * ki).astype(BF16)
            yc_scr[base + n2:base + 2 * n2, :] = (xr * ki + xi * kr).astype(BF16)
        for g in range(DFT_GROUPS):
            start = _packed_start(g, n2)
            ycat = yc_scr[2 * n2 * g:2 * n2 * (g + 1), :]
            if g in (0, 8):
                h_scr[start:start + n2, :] = jnp.dot(ur_ref[0 if g == 0 else 1], ycat,
                                                     preferred_element_type=F32)
            else:
                h_scr[start:start + 2 * n2, :] = jnp.dot(uc_ref[g - 1], ycat,
                                                         preferred_element_type=F32)

        def body(i, carry):
            r0 = i * rb
            at = lambda off: pl.ds(pl.multiple_of(off + r0, rb), rb)
            for l0 in range(0, cb, lt):
                lanes = slice(l0, l0 + lt)
                a = {g: h_scr[at(_packed_start(g, n2)), lanes] for g in range(1, 8)}
                b = {g: h_scr[at(_packed_start(g, n2) + n2), lanes] for g in range(1, 8)}
                conv = _dft16_inv(h_scr[at(_packed_start(0, n2)), lanes],
                                  h_scr[at(_packed_start(8, n2)), lanes], a, b)
                skip = skip_ref[n:n + 1, lanes]
                for t1 in range(8):
                    rows = at(n2 * t1)
                    y_new = gate_scr[n, rows, lanes] * (conv[t1] + y_scr[rows, lanes] * skip)
                    if last:
                        o_ref[rows, lanes] = y_new.astype(o_ref.dtype)
                    else:
                        y_next[rows, lanes] = y_new
            return carry

        lax.fori_loop(0, n2 // rb, body, 0)


def _hyena_call(z, conv_w, conv_b, skip, kspec, tables, bsz, seq, width, col0):
    cb = MXU_DIM
    ncb = width // cb
    base = col0 // cb
    zspec = lambda part: pl.BlockSpec((seq, cb), lambda k, b: (b, base + part * ncb + k))
    wspec = lambda part: pl.BlockSpec((HY_SHORT, cb), lambda k, b: (0, part * ncb + k))
    bspec = lambda part: pl.BlockSpec((1, cb), lambda k, b: (0, part * ncb + k))
    cb_all = conv_b.reshape(1, -1)
    const = lambda k, b: (0, 0, 0)
    return pl.pallas_call(
        functools.partial(_hyena_kernel, rb=16, lt=LANE),
        grid=(ncb, bsz),
        in_specs=[zspec(0), zspec(1), zspec(2), wspec(0), wspec(1), wspec(2),
                  bspec(0), bspec(1), bspec(2),
                  pl.BlockSpec((HY_ORDER, cb), lambda k, b: (0, k)),
                  _resident((HY_ORDER, kspec.shape[1], cb), lambda k, b: (0, 0, k))]
                 + [_resident(t.shape, const) for t in tables],
        out_specs=pl.BlockSpec((seq, cb), lambda k, b: (b, k)),
        out_shape=jax.ShapeDtypeStruct((bsz * seq, width), BF16),
        scratch_shapes=[pltpu.VMEM((seq, cb), F32), pltpu.VMEM((seq, cb), F32),
                        pltpu.VMEM((HY_ORDER, seq, cb), F32),
                        pltpu.VMEM((2 * seq, cb), BF16),
                        pltpu.VMEM((kspec.shape[1], cb), BF16),
                        pltpu.VMEM((2 * seq, cb), F32)],
        compiler_params=_cparams(("arbitrary", "arbitrary")),
        name="hyena_mixer",
    )(z, z, z, conv_w, conv_w, conv_w, cb_all, cb_all, cb_all, skip, kspec, *tables)


def _merge_kernel(u_ref, vn_ref, yh_ref, ga_ref, gb_ref, ws_ref, bs_ref, wa_ref, wb_ref, o_ref,
                  s_scr):
    tm = u_ref.shape[0]
    gd = u_ref.shape[1] // SGU_GROUPS
    for n in range(tm // SGU_CHUNK):
        rows = slice(n * SGU_CHUNK, (n + 1) * SGU_CHUNK)
        for g in range(SGU_GROUPS):
            cols = slice(g * gd, (g + 1) * gd)
            mixed = jnp.dot(ws_ref[g], vn_ref[rows, cols], preferred_element_type=F32)
            s_scr[rows, cols] = (u_ref[rows, cols].astype(F32)
                                 * (mixed + bs_ref[:, cols])).astype(BF16)
    y_a = jnp.dot(s_scr[...], wa_ref[...], preferred_element_type=F32)
    y_b = jnp.dot(yh_ref[...], wb_ref[...], preferred_element_type=F32)
    o_ref[...] = (ga_ref[...].astype(F32) * y_a + gb_ref[...].astype(F32) * y_b).astype(o_ref.dtype)


def _merge_call(z, yh, sgu_w16, sgu_bfull, w_a, w_b, d):
    t = z.shape[0]
    wdt = w_a.shape[0]
    tm = 512
    u_blk = 2 * d // wdt
    return pl.pallas_call(
        _merge_kernel,
        grid=(t // tm,),
        in_specs=[pl.BlockSpec((tm, wdt), lambda i: (i, u_blk)),
                  pl.BlockSpec((tm, wdt), lambda i: (i, u_blk + 1)),
                  pl.BlockSpec((tm, wdt), lambda i: (i, 0)),
                  pl.BlockSpec((tm, d), lambda i: (i, 0)),
                  pl.BlockSpec((tm, d), lambda i: (i, 1)),
                  _resident(sgu_w16.shape, lambda i: (0, 0, 0)),
                  _resident(sgu_bfull.shape, lambda i: (0, 0)),
                  _resident(w_a.shape, lambda i: (0, 0)),
                  _resident(w_b.shape, lambda i: (0, 0))],
        out_specs=pl.BlockSpec((tm, d), lambda i: (i, 0)),
        out_shape=jax.ShapeDtypeStruct((t, d), BF16),
        scratch_shapes=[pltpu.VMEM((tm, wdt), BF16)],
        compiler_params=_cparams(("parallel",)),
        name="gated_merge",
    )(z, z, yh, z, z, sgu_w16, sgu_bfull, w_a, w_b)


def _oproj_kernel(m_ref, x_ref, g1_ref, w_ref, b_ref, lg_ref, lb_ref, o_ref, *, alpha, rc):
    for rows in _row_chunks(m_ref.shape[0], rc):
        mix = jnp.dot(m_ref[rows, :], w_ref[...], preferred_element_type=F32) + b_ref[...]
        o_ref[rows, :] = _layer_norm(alpha * x_ref[rows, :] + g1_ref[0] * mix,
                                     lg_ref[...], lb_ref[...])


def _oproj_call(m, xt, mod3, w_o, b_o, ln_g, ln_b, seq, alpha):
    t, d = xt.shape
    tm = 512
    per_b = seq // tm
    row = lambda i: (0, 0)
    return pl.pallas_call(
        functools.partial(_oproj_kernel, alpha=alpha, rc=256),
        grid=(t // tm,),
        in_specs=[pl.BlockSpec((tm, d), lambda i: (i, 0)),
                  pl.BlockSpec((tm, d), lambda i: (i, 0)),
                  pl.BlockSpec((1, 1, d), lambda i: (i // per_b, 0, 2)),
                  _resident((d, d), row),
                  pl.BlockSpec((1, d), row), pl.BlockSpec((1, d), row), pl.BlockSpec((1, d), row)],
        out_specs=pl.BlockSpec((tm, d), lambda i: (i, 0)),
        out_shape=jax.ShapeDtypeStruct((t, d), F32),
        compiler_params=_cparams(("parallel",)),
        name="out_proj_ln",
    )(m, xt, mod3, w_o, b_o.reshape(1, d), ln_g.reshape(1, d), ln_b.reshape(1, d))


def _mlp_kernel(x_ref, sc_ref, sh_ref, g2_ref, w1_ref, b1_ref, w2_ref, b2_ref, lg_ref, lb_ref,
                o_ref, h_scr, acc_scr, *, alpha, rc):
    j = pl.program_id(1)
    last = pl.num_programs(1) - 1
    chunks = _row_chunks(x_ref.shape[0], rc)

    def partial_out(rows):
        hid = jnp.dot(h_scr[rows, :], w1_ref[...], preferred_element_type=F32) + b1_ref[...]
        hid = jnp.square(jnp.maximum(hid, 0.0)).astype(BF16)
        return jnp.dot(hid, w2_ref[...], preferred_element_type=F32)

    @pl.when(j == 0)
    def _():
        for rows in chunks:
            h_scr[rows, :] = (x_ref[rows, :] * (1.0 + sc_ref[0]) + sh_ref[0]).astype(BF16)
            acc_scr[rows, :] = partial_out(rows)

    @pl.when(jnp.logical_and(j > 0, j < last))
    def _():
        acc_scr[...] += partial_out(slice(None))

    @pl.when(j == last)
    def _():
        for rows in chunks:
            f = acc_scr[rows, :] + partial_out(rows) + b2_ref[...]
            o_ref[rows, :] = _layer_norm(alpha * x_ref[rows, :] + g2_ref[0] * f,
                                         lg_ref[...], lb_ref[...])


def _mlp_call(x1, mod3, w_m1, b_m1, w_m2, b_m2, ln_g, ln_b, seq, alpha):
    t, d = x1.shape
    dff = w_m1.shape[1]
    tm = 512
    tf = 1024
    per_b = seq // tm
    row = lambda i, j: (0, 0)
    return pl.pallas_call(
        functools.partial(_mlp_kernel, alpha=alpha, rc=256),
        grid=(t // tm, dff // tf),
        in_specs=[pl.BlockSpec((tm, d), lambda i, j: (i, 0)),
                  pl.BlockSpec((1, 1, d), lambda i, j: (i // per_b, 0, 4)),
                  pl.BlockSpec((1, 1, d), lambda i, j: (i // per_b, 0, 3)),
                  pl.BlockSpec((1, 1, d), lambda i, j: (i // per_b, 0, 5)),
                  pl.BlockSpec((d, tf), lambda i, j: (0, j)),
                  pl.BlockSpec((1, tf), lambda i, j: (0, j)),
                  pl.BlockSpec((tf, d), lambda i, j: (j, 0)),
                  pl.BlockSpec((1, d), row), pl.BlockSpec((1, d), row), pl.BlockSpec((1, d), row)],
        out_specs=pl.BlockSpec((tm, d), lambda i, j: (i, 0)),
        out_shape=jax.ShapeDtypeStruct((t, d), F32),
        scratch_shapes=[pltpu.VMEM((tm, d), BF16), pltpu.VMEM((tm, d), F32)],
        compiler_params=_cparams(("parallel", "arbitrary")),
        name="mlp_ln",
    )(x1, mod3, mod3, mod3, w_m1, b_m1.reshape(1, dff), w_m2, b_m2.reshape(1, d),
      ln_g.reshape(1, d), ln_b.reshape(1, d))


def _filter_features(seq):
    bands_n = (HY_EMB - 1) // 2
    t = jnp.linspace(0.0, 1.0, seq, dtype=F32)[:, None]
    omega = 2.0 * math.pi * jnp.arange(seq, dtype=F32)[:, None] / seq
    bands = jnp.linspace(1e-4, bands_n - 1, bands_n, dtype=F32)[None, :]
    feats = jnp.concatenate([t, jnp.cos(bands * omega), -jnp.sin(bands * omega)], axis=-1)
    return t, feats


def _decay_rates(width):
    min_decay = math.log(HY_DECAY_TARGET) / HY_SLOW_DECAY
    max_decay = math.log(HY_DECAY_TARGET) / HY_FAST_DECAY
    return jnp.abs(jnp.linspace(min_decay, max_decay, width, dtype=F32))[None, :]


def kernel(x, c, w_ada, b_ada, w_in, b_in, sgu_ln_g, sgu_ln_b, sgu_w, sgu_b, hy_conv_w, hy_conv_b, hy_w1, hy_b1, hy_w2, hy_b2, hy_freq, hy_w3, hy_skip, w_branch_a, w_branch_b, w_o, b_o, ln1_g, ln1_b, w_m1, b_m1, w_m2, b_m2, ln2_g, ln2_b):
    bsz, seq, d = x.shape
    depth = w_ada.shape[0]
    alpha = (2.0 * depth) ** 0.25
    sgu_width = w_branch_a.shape[1]
    hy_width = w_branch_b.shape[1]
    assert seq % SGU_CHUNK == 0 and sgu_w.shape[-1] == SGU_CHUNK

    assert seq % (8 * MXU_DIM) == 0
    tables = _dft_stage_tables(seq)
    tcol, feats = _filter_features(seq)
    deltas = _decay_rates(hy_width)

    xt = x.reshape(bsz * seq, d)
    for l in range(depth):
        mod3 = _mod_call(c, w_ada[l], b_ada[l]).reshape(bsz, 1, N_MOD * d)

        n_front = w_in.shape[2] - 2 * d
        w_perm = jnp.concatenate([w_in[l][:, n_front:], w_in[l][:, :n_front]], axis=1).astype(BF16)
        b_perm = jnp.concatenate([b_in[l][n_front:], b_in[l][:n_front]])
        z = _inproj_call(xt, mod3, w_perm, b_perm, sgu_ln_g[l], sgu_ln_b[l], seq, sgu_width, hy_width)

        h2 = _filter_mlp_call(feats, hy_w1[l], hy_b1[l], hy_w2[l], hy_b2[l], hy_freq[l])
        kspec = _filter_spec_call(h2, hy_w3[l], tcol, deltas, tables[0], tables[1], hy_width)
        yh = _hyena_call(z, hy_conv_w[l], hy_conv_b[l], hy_skip[l], kspec, tables,
                         bsz, seq, hy_width, 2 * d + 2 * sgu_width)

        gd = sgu_width // SGU_GROUPS
        bs_full = jnp.repeat(sgu_b[l].T, gd, axis=1)
        m = _merge_call(z, yh, sgu_w[l].astype(BF16), bs_full, w_branch_a[l].astype(BF16),
                        w_branch_b[l].astype(BF16), d)

        x1 = _oproj_call(m, xt, mod3, w_o[l].astype(BF16), b_o[l], ln1_g[l], ln1_b[l], seq, alpha)
        xt = _mlp_call(x1, mod3, w_m1[l].astype(BF16), b_m1[l], w_m2[l].astype(BF16), b_m2[l],
                       ln2_g[l], ln2_b[l], seq, alpha)
    return xt.reshape(bsz, seq, d)
```

```python
import functools
import math

import jax
import jax.numpy as jnp
from jax import lax
from jax.experimental import pallas as pl
from jax.experimental.pallas import tpu as pltpu

F32 = jnp.float32
BF16 = jnp.bfloat16

SGU_CHUNK = 128
SGU_GROUPS = 8
HY_ORDER = 2
HY_SHORT = 3
HY_EMB = 33
HY_DECAY_TARGET = 1e-2
HY_FAST_DECAY = 0.3
HY_SLOW_DECAY = 1.5
N_MOD = 6
LN_EPS = 1e-5

LANE = 128
MXU_DIM = 256
VMEM_LIMIT = 48 * 1024 * 1024


def _cparams(sem):
    return pltpu.CompilerParams(dimension_semantics=sem, vmem_limit_bytes=VMEM_LIMIT)


def _resident(block_shape, index_map):
    return pl.BlockSpec(block_shape, index_map, pipeline_mode=pl.Buffered(1))


def _gelu(x):
    return 0.5 * x * (1.0 + lax.erf(x * (1.0 / math.sqrt(2.0))))


def _layer_norm(x, g, b):
    mu = jnp.mean(x, axis=-1, keepdims=True)
    xc = x - mu
    var = jnp.mean(xc * xc, axis=-1, keepdims=True)
    return xc * lax.rsqrt(var + LN_EPS) * g + b


def _dot_bf16x3(a, b):
    m = a.shape[0]
    a_hi = a.astype(BF16)
    b_hi = b.astype(BF16)
    a_hi32 = a_hi.astype(F32)
    b_lo = (b - b_hi.astype(F32)).astype(BF16)
    stacked = jnp.concatenate([a_hi32, a - a_hi32], axis=0).astype(BF16)
    top = jnp.dot(stacked, b_hi, preferred_element_type=F32)
    return top[:m] + top[m:] + jnp.dot(a_hi, b_lo, preferred_element_type=F32)


def _mod_kernel(c_ref, w_ref, b_ref, o_ref):
    c = c_ref[...]
    cond = c * jax.nn.sigmoid(c)
    o_ref[...] = _dot_bf16x3(cond, w_ref[...]) + b_ref[...]


def _mod_call(c, w_ada, b_ada):
    bsz, d = c.shape
    n = w_ada.shape[1]
    tn = 1024
    return pl.pallas_call(
        _mod_kernel,
        grid=(n // tn,),
        in_specs=[pl.BlockSpec((bsz, d), lambda j: (0, 0)),
                  pl.BlockSpec((d, tn), lambda j: (0, j)),
                  pl.BlockSpec((1, tn), lambda j: (0, j))],
        out_specs=pl.BlockSpec((bsz, tn), lambda j: (0, j)),
        out_shape=jax.ShapeDtypeStruct((bsz, n), F32),
        compiler_params=_cparams(("arbitrary",)),
        name="adaln_mod",
    )(c, w_ada, b_ada.reshape(1, n))


def _row_chunks(total, size):
    return [slice(r, r + size) for r in range(0, total, size)]


def _inproj_kernel(x_ref, sc_ref, sh_ref, w_ref, b_ref, g_ref, be_ref, o_ref, h_scr, *, seg, rc):
    j = pl.program_id(1)
    chunks = _row_chunks(x_ref.shape[0], rc)

    def emit(act, modulate=False):
        for rows in chunks:
            if modulate:
                h_scr[rows, :] = (x_ref[rows, :] * (1.0 + sc_ref[0]) + sh_ref[0]).astype(BF16)
            acc = jnp.dot(h_scr[rows, :], w_ref[...], preferred_element_type=F32) + b_ref[...]
            o_ref[rows, :] = act(acc).astype(o_ref.dtype)

    @pl.when(j == 0)
    def _():
        emit(jax.nn.sigmoid, modulate=True)

    @pl.when(jnp.logical_and(j > 0, j < seg[0]))
    def _():
        emit(jax.nn.sigmoid)

    @pl.when(jnp.logical_and(j >= seg[0], j < seg[1]))
    def _():
        emit(_gelu)

    @pl.when(jnp.logical_and(j >= seg[1], j < seg[2]))
    def _():
        emit(lambda a: _layer_norm(_gelu(a), g_ref[...], be_ref[...]))

    @pl.when(j >= seg[2])
    def _():
        emit(lambda a: a)


def _inproj_call(xt, mod3, w_in, b_in, ln_g, ln_b, seq, sgu_w, hy_w):
    t, d = xt.shape
    n = w_in.shape[1]
    tm = 1024
    tn = sgu_w
    assert hy_w == sgu_w and seq % tm == 0 and d % tn == 0
    nblk = n // tn
    n_gate = 2 * d // tn
    seg = (n_gate, n_gate + 1, n_gate + 2)
    per_b = seq // tm
    wcol = lambda i, j: (0, (j + (nblk - n_gate)) % nblk)
    return pl.pallas_call(
        functools.partial(_inproj_kernel, seg=seg, rc=256),
        grid=(t // tm, n // tn),
        in_specs=[pl.BlockSpec((tm, d), lambda i, j: (i, 0)),
                  pl.BlockSpec((1, 1, d), lambda i, j: (i // per_b, 0, 1)),
                  pl.BlockSpec((1, 1, d), lambda i, j: (i // per_b, 0, 0)),
                  pl.BlockSpec((d, tn), wcol),
                  pl.BlockSpec((1, tn), wcol),
                  pl.BlockSpec((1, tn), lambda i, j: (0, 0)),
                  pl.BlockSpec((1, tn), lambda i, j: (0, 0))],
        out_specs=pl.BlockSpec((tm, tn), lambda i, j: (i, j)),
        out_shape=jax.ShapeDtypeStruct((t, n), BF16),
        scratch_shapes=[pltpu.VMEM((tm, d), BF16)],
        compiler_params=_cparams(("parallel", "arbitrary")),
        name="in_proj",
    )(xt, mod3, mod3, w_in, b_in.reshape(1, n), ln_g.reshape(1, tn), ln_b.reshape(1, tn))


def _filter_mlp_kernel(f_ref, w1_ref, b1_ref, w2_ref, b2_ref, fr_ref, o_ref):
    hp = lax.Precision.HIGHEST
    fr = fr_ref[...]
    h = jnp.sin(fr * (jnp.dot(f_ref[...], w1_ref[...], preferred_element_type=F32, precision=hp)
                      + b1_ref[...]))
    o_ref[...] = jnp.sin(fr * (jnp.dot(h, w2_ref[...], preferred_element_type=F32, precision=hp)
                               + b2_ref[...]))


def _filter_mlp_call(feats, w1, b1, w2, b2, freq):
    seq = feats.shape[0]
    hid = w1.shape[1]
    return pl.pallas_call(
        _filter_mlp_kernel,
        out_shape=jax.ShapeDtypeStruct((seq, hid), F32),
        name="hyena_filter_mlp",
    )(feats, w1, b1.reshape(1, hid), w2, b2.reshape(1, hid), freq.reshape(1, hid))


DFT_GROUPS = 9
DFT_SLABS = 8
DFT_COMPS = 16
DFT_J = 32


def _packed_start(g, n2):
    return {0: 0, 8: n2}.get(g, 2 * n2 * g)


def _store_block_order(dst_ref, val, n2):
    step = DFT_SLABS * DFT_J
    for k in range(n2 // DFT_J):
        for t1 in range(DFT_SLABS):
            src = n2 * t1 + DFT_J * k
            dst = step * k + DFT_J * t1
            dst_ref[dst:dst + DFT_J, :] = val[src:src + DFT_J, :].astype(dst_ref.dtype)


def _stage1_fwd(s1_ref, src_ref, dst_ref, n2):
    step = DFT_SLABS * DFT_J
    for k in range(n2 // DFT_J):
        comps = jnp.dot(s1_ref[...], src_ref[step * k:step * (k + 1), :], preferred_element_type=F32)
        for c in range(DFT_COMPS):
            dst_ref[n2 * c + DFT_J * k:n2 * c + DFT_J * (k + 1), :] = (
                comps[DFT_J * c:DFT_J * (c + 1), :].astype(dst_ref.dtype))


def _stage2_fwd(g, tr_ref, tc_ref, src_ref, n2):
    start = _packed_start(g, n2)
    if g in (0, 8):
        return jnp.dot(tr_ref[0 if g == 0 else 1], src_ref[start:start + n2, :],
                       preferred_element_type=F32)
    return jnp.dot(tc_ref[g - 1], src_ref[start:start + 2 * n2, :], preferred_element_type=F32)


def _dft_stage_tables(seq):
    n = 2 * seq
    n2 = n // 16
    f2 = jnp.arange(n2, dtype=jnp.int32)[:, None]
    t2 = jnp.arange(n2, dtype=jnp.int32)[None, :]

    def cs(g):
        ang = ((t2 * (g + 16 * f2)) & (n - 1)).astype(F32) * (2.0 * math.pi / n)
        return jnp.cos(ang), jnp.sin(ang)

    real = jnp.stack([jnp.concatenate([c, -s], axis=0) for c, s in (cs(0), cs(8))])
    cplx = jnp.stack([jnp.block([[c, s], [-s, c]]) for c, s in (cs(g) for g in range(1, 8))])

    t1 = jnp.arange(DFT_SLABS, dtype=jnp.int32)
    rows = [jnp.ones((DFT_SLABS,), F32), jnp.where(t1 % 2 == 0, 1.0, -1.0).astype(F32)]
    for g in range(1, 8):
        ang = ((g * t1) % 16).astype(F32) * (math.pi / 8.0)
        rows += [jnp.cos(ang), -jnp.sin(ang)]
    s1 = jnp.kron(jnp.stack(rows), jnp.eye(DFT_J, dtype=F32))
    return tuple(t.astype(BF16) for t in (s1, real, cplx, s1.T, jnp.swapaxes(real, 1, 2),
                                          jnp.swapaxes(cplx, 1, 2)))


def _filter_spec_kernel(h_ref, wf_ref, wb_ref, t_ref, dl_ref, skip_ref, s1_ref, tr_ref, tc_ref,
                        k_ref, e_scr, o_scr, ge_scr, go_scr):
    seq = h_ref.shape[0]
    n2 = seq // 8
    window = jnp.exp(-t_ref[...] * dl_ref[...])
    _store_block_order(e_scr, _dot_bf16x3(h_ref[...], wf_ref[...] + wb_ref[...]) * window, n2)
    _store_block_order(o_scr, _dot_bf16x3(h_ref[...], wf_ref[...] - wb_ref[...]) * window, n2)
    _stage1_fwd(s1_ref, e_scr, ge_scr, n2)
    _stage1_fwd(s1_ref, o_scr, go_scr, n2)
    for g in range(DFT_GROUPS):
        wgt = (1.0 if g in (0, 8) else 2.0) / (2 * seq)
        rows = slice(2 * n2 * g, 2 * n2 * g + n2)
        rows_im = slice(2 * n2 * g + n2, 2 * n2 * (g + 1))
        k_ref[0, rows, :] = (_stage2_fwd(g, tr_ref, tc_ref, ge_scr, n2)[:n2] + skip_ref[0]) * wgt
        k_ref[0, rows_im, :] = _stage2_fwd(g, tr_ref, tc_ref, go_scr, n2)[n2:] * wgt


def _filter_spec_call(h2, w3, tcol, deltas, skip, t_s1, t_real, t_cplx, width):
    seq, hid = h2.shape
    n2 = seq // 8
    cb = MXU_DIM
    ncb = width // cb
    per_dir = HY_ORDER * ncb
    rows = DFT_GROUPS * 2 * n2
    const = lambda n, k: (0, 0, 0)
    return pl.pallas_call(
        _filter_spec_kernel,
        grid=(HY_ORDER, ncb),
        in_specs=[_resident((seq, hid), lambda n, k: (0, 0)),
                  pl.BlockSpec((hid, cb), lambda n, k: (0, n * ncb + k)),
                  pl.BlockSpec((hid, cb), lambda n, k: (0, per_dir + n * ncb + k)),
                  _resident((seq, 1), lambda n, k: (0, 0)),
                  pl.BlockSpec((1, cb), lambda n, k: (0, k)),
                  pl.BlockSpec((1, 1, cb), lambda n, k: (n, 0, k)),
                  _resident(t_s1.shape, lambda n, k: (0, 0)),
                  _resident(t_real.shape, const), _resident(t_cplx.shape, const)],
        out_specs=pl.BlockSpec((1, rows, cb), lambda n, k: (n, 0, k)),
        out_shape=jax.ShapeDtypeStruct((HY_ORDER, rows, width), F32),
        scratch_shapes=[pltpu.VMEM((seq, cb), BF16), pltpu.VMEM((seq, cb), BF16),
                        pltpu.VMEM((2 * seq, cb), BF16), pltpu.VMEM((2 * seq, cb), BF16)],
        compiler_params=_cparams(("arbitrary", "arbitrary")),
        name="hyena_filter_spectrum",
    )(h2, w3, w3, tcol, deltas, skip.reshape(HY_ORDER, 1, width), t_s1, t_real, t_cplx)


SUBLANES = 8


def _short_conv(z, w, b):
    seq = z.shape[0]
    prev = pltpu.roll(z, 1, 0)
    nxt = pltpu.roll(z, seq - 1, 0)
    row = lax.broadcasted_iota(jnp.int32, (SUBLANES, z.shape[1]), 0)
    prev = jnp.concatenate([jnp.where(row == 0, 0.0, prev[:SUBLANES]), prev[SUBLANES:]], axis=0)
    nxt = jnp.concatenate([nxt[:-SUBLANES], jnp.where(row == SUBLANES - 1, 0.0, nxt[-SUBLANES:])],
                          axis=0)
    return w[0:1] * prev + w[1:2] * z + w[2:3] * nxt + b


def _hyena_kernel(v_ref, x1_ref, x2_ref, wv_ref, w1_ref, w2_ref, bv_ref, b1_ref, b2_ref,
                  k_ref, s1_ref, tr_ref, tc_ref, s1t_ref, ur_ref, uc_ref, o_ref,
                  y16_scr, gate_scr, g_scr, yc_scr, h_scr):
    seq, cb = v_ref.shape
    n2 = seq // DFT_SLABS
    step = DFT_SLABS * DFT_J

    conv3 = lambda z_ref, w_ref, b_ref: _short_conv(z_ref[...].astype(F32), w_ref[...], b_ref[...])
    _store_block_order(y16_scr, conv3(v_ref, wv_ref, bv_ref), n2)
    _store_block_order(gate_scr.at[0], conv3(x1_ref, w1_ref, b1_ref), n2)
    _store_block_order(gate_scr.at[1], conv3(x2_ref, w2_ref, b2_ref), n2)

    for n in range(HY_ORDER):
        last = n + 1 == HY_ORDER
        _stage1_fwd(s1_ref, y16_scr, g_scr, n2)
        for g in range(DFT_GROUPS):
            x = _stage2_fwd(g, tr_ref, tc_ref, g_scr, n2)
            xr, xi = x[:n2], x[n2:]
            base = 2 * n2 * g
            kr = k_ref[n, base:base + n2, :]
            ki = k_ref[n, base + n2:base + 2 * n2, :]
            yc_scr[base:base + n2, :] = (xr * kr - xi * ki).astype(BF16)
            yc_scr[base + n2:base + 2 * n2, :] = (xr * ki + xi * kr).astype(BF16)
        for g in range(DFT_GROUPS):
            ycat = yc_scr[2 * n2 * g:2 * n2 * (g + 1), :]
            if g in (0, 8):
                parts = [(0 if g == 0 else 1,
                          jnp.dot(ur_ref[0 if g == 0 else 1], ycat, preferred_element_type=F32))]
            else:
                h = jnp.dot(uc_ref[g - 1], ycat, preferred_element_type=F32)
                parts = [(2 * g, h[:n2]), (2 * g + 1, h[n2:])]
            for c, hp in parts:
                for k in range(n2 // DFT_J):
                    dst = DFT_COMPS * DFT_J * k + DFT_J * c
                    h_scr[dst:dst + DFT_J, :] = hp[DFT_J * k:DFT_J * (k + 1), :].astype(BF16)
        for k in range(n2 // DFT_J):
            rows = slice(step * k, step * (k + 1))
            conv = jnp.dot(s1t_ref[...], h_scr[DFT_COMPS * DFT_J * k:DFT_COMPS * DFT_J * (k + 1), :],
                           preferred_element_type=F32)
            y_new = (gate_scr[n, rows, :] * conv).astype(BF16)
            if last:
                for t1 in range(DFT_SLABS):
                    dst = n2 * t1 + DFT_J * k
                    o_ref[dst:dst + DFT_J, :] = y_new[DFT_J * t1:DFT_J * (t1 + 1), :]
            else:
                y16_scr[rows, :] = y_new


def _hyena_call(z, conv_w, conv_b, kspec, tables, bsz, seq, width, col0):
    cb = MXU_DIM
    ncb = width // cb
    base = col0 // cb
    zspec = lambda part: pl.BlockSpec((seq, cb), lambda k, b: (b, base + part * ncb + k))
    wspec = lambda part: pl.BlockSpec((HY_SHORT, cb), lambda k, b: (0, part * ncb + k))
    bspec = lambda part: pl.BlockSpec((1, cb), lambda k, b: (0, part * ncb + k))
    cb_all = conv_b.reshape(1, -1)
    const = lambda t: (lambda k, b: (0,) * t.ndim)
    return pl.pallas_call(
        _hyena_kernel,
        grid=(ncb, bsz),
        in_specs=[zspec(0), zspec(1), zspec(2), wspec(0), wspec(1), wspec(2),
                  bspec(0), bspec(1), bspec(2),
                  _resident((HY_ORDER, kspec.shape[1], cb), lambda k, b: (0, 0, k))]
                 + [_resident(t.shape, const(t)) for t in tables],
        out_specs=pl.BlockSpec((seq, cb), lambda k, b: (b, k)),
        out_shape=jax.ShapeDtypeStruct((bsz * seq, width), BF16),
        scratch_shapes=[pltpu.VMEM((seq, cb), BF16),
                        pltpu.VMEM((HY_ORDER, seq, cb), F32),
                        pltpu.VMEM((2 * seq, cb), BF16),
                        pltpu.VMEM((kspec.shape[1], cb), BF16),
                        pltpu.VMEM((2 * seq, cb), BF16)],
        compiler_params=_cparams(("arbitrary", "arbitrary")),
        name="hyena_mixer",
    )(z, z, z, conv_w, conv_w, conv_w, cb_all, cb_all, cb_all, kspec, *tables)


def _merge_kernel(u_ref, vn_ref, yh_ref, ga_ref, gb_ref, ws_ref, bs_ref, wa_ref, wb_ref, o_ref,
                  s_scr):
    tm = u_ref.shape[0]
    gd = u_ref.shape[1] // SGU_GROUPS
    for n in range(tm // SGU_CHUNK):
        rows = slice(n * SGU_CHUNK, (n + 1) * SGU_CHUNK)
        for g in range(SGU_GROUPS):
            cols = slice(g * gd, (g + 1) * gd)
            mixed = jnp.dot(ws_ref[g], vn_ref[rows, cols], preferred_element_type=F32)
            s_scr[rows, cols] = (u_ref[rows, cols].astype(F32)
                                 * (mixed + bs_ref[:, cols])).astype(BF16)
    y_a = jnp.dot(s_scr[...], wa_ref[...], preferred_element_type=F32)
    y_b = jnp.dot(yh_ref[...], wb_ref[...], preferred_element_type=F32)
    o_ref[...] = (ga_ref[...].astype(F32) * y_a + gb_ref[...].astype(F32) * y_b).astype(o_ref.dtype)


def _merge_call(z, yh, sgu_w16, sgu_bfull, w_a, w_b, d):
    t = z.shape[0]
    wdt = w_a.shape[0]
    tm = 512
    u_blk = 2 * d // wdt
    return pl.pallas_call(
        _merge_kernel,
        grid=(t // tm,),
        in_specs=[pl.BlockSpec((tm, wdt), lambda i: (i, u_blk)),
                  pl.BlockSpec((tm, wdt), lambda i: (i, u_blk + 1)),
                  pl.BlockSpec((tm, wdt), lambda i: (i, 0)),
                  pl.BlockSpec((tm, d), lambda i: (i, 0)),
                  pl.BlockSpec((tm, d), lambda i: (i, 1)),
                  _resident(sgu_w16.shape, lambda i: (0, 0, 0)),
                  _resident(sgu_bfull.shape, lambda i: (0, 0)),
                  _resident(w_a.shape, lambda i: (0, 0)),
                  _resident(w_b.shape, lambda i: (0, 0))],
        out_specs=pl.BlockSpec((tm, d), lambda i: (i, 0)),
        out_shape=jax.ShapeDtypeStruct((t, d), BF16),
        scratch_shapes=[pltpu.VMEM((tm, wdt), BF16)],
        compiler_params=_cparams(("parallel",)),
        name="gated_merge",
    )(z, z, yh, z, z, sgu_w16, sgu_bfull, w_a, w_b)


def _oproj_kernel(m_ref, x_ref, g1_ref, w_ref, b_ref, lg_ref, lb_ref, o_ref, *, alpha, rc):
    for rows in _row_chunks(m_ref.shape[0], rc):
        mix = jnp.dot(m_ref[rows, :], w_ref[...], preferred_element_type=F32) + b_ref[...]
        o_ref[rows, :] = _layer_norm(alpha * x_ref[rows, :] + g1_ref[0] * mix,
                                     lg_ref[...], lb_ref[...])


def _oproj_call(m, xt, mod3, w_o, b_o, ln_g, ln_b, seq, alpha):
    t, d = xt.shape
    tm = 512
    per_b = seq // tm
    row = lambda i: (0, 0)
    return pl.pallas_call(
        functools.partial(_oproj_kernel, alpha=alpha, rc=256),
        grid=(t // tm,),
        in_specs=[pl.BlockSpec((tm, d), lambda i: (i, 0)),
                  pl.BlockSpec((tm, d), lambda i: (i, 0)),
                  pl.BlockSpec((1, 1, d), lambda i: (i // per_b, 0, 2)),
                  _resident((d, d), row),
                  pl.BlockSpec((1, d), row), pl.BlockSpec((1, d), row), pl.BlockSpec((1, d), row)],
        out_specs=pl.BlockSpec((tm, d), lambda i: (i, 0)),
        out_shape=jax.ShapeDtypeStruct((t, d), F32),
        compiler_params=_cparams(("parallel",)),
        name="out_proj_ln",
    )(m, xt, mod3, w_o, b_o.reshape(1, d), ln_g.reshape(1, d), ln_b.reshape(1, d))


def _mlp_kernel(x_ref, sc_ref, sh_ref, g2_ref, w1_ref, b1_ref, w2_ref, b2_ref, lg_ref, lb_ref,
                o_ref, h_scr, acc_scr, *, alpha, rc):
    j = pl.program_id(1)
    last = pl.num_programs(1) - 1
    chunks = _row_chunks(x_ref.shape[0], rc)

    def partial_out(rows):
        hid = jnp.dot(h_scr[rows, :], w1_ref[...], preferred_element_type=F32) + b1_ref[...]
        hid = jnp.square(jnp.maximum(hid, 0.0)).astype(BF16)
        return jnp.dot(hid, w2_ref[...], preferred_element_type=F32)

    @pl.when(j == 0)
    def _():
        for rows in chunks:
            h_scr[rows, :] = (x_ref[rows, :] * (1.0 + sc_ref[0]) + sh_ref[0]).astype(BF16)
            acc_scr[rows, :] = partial_out(rows)

    @pl.when(jnp.logical_and(j > 0, j < last))
    def _():
        acc_scr[...] += partial_out(slice(None))

    @pl.when(j == last)
    def _():
        for rows in chunks:
            f = acc_scr[rows, :] + partial_out(rows) + b2_ref[...]
            o_ref[rows, :] = _layer_norm(alpha * x_ref[rows, :] + g2_ref[0] * f,
                                         lg_ref[...], lb_ref[...])


def _mlp_call(x1, mod3, w_m1, b_m1, w_m2, b_m2, ln_g, ln_b, seq, alpha):
    t, d = x1.shape
    dff = w_m1.shape[1]
    tm = 512
    tf = 1024
    per_b = seq // tm
    row = lambda i, j: (0, 0)
    return pl.pallas_call(
        functools.partial(_mlp_kernel, alpha=alpha, rc=256),
        grid=(t // tm, dff // tf),
        in_specs=[pl.BlockSpec((tm, d), lambda i, j: (i, 0)),
                  pl.BlockSpec((1, 1, d), lambda i, j: (i // per_b, 0, 4)),
                  pl.BlockSpec((1, 1, d), lambda i, j: (i // per_b, 0, 3)),
                  pl.BlockSpec((1, 1, d), lambda i, j: (i // per_b, 0, 5)),
                  pl.BlockSpec((d, tf), lambda i, j: (0, j)),
                  pl.BlockSpec((1, tf), lambda i, j: (0, j)),
                  pl.BlockSpec((tf, d), lambda i, j: (j, 0)),
                  pl.BlockSpec((1, d), row), pl.BlockSpec((1, d), row), pl.BlockSpec((1, d), row)],
        out_specs=pl.BlockSpec((tm, d), lambda i, j: (i, 0)),
        out_shape=jax.ShapeDtypeStruct((t, d), F32),
        scratch_shapes=[pltpu.VMEM((tm, d), BF16), pltpu.VMEM((tm, d), F32)],
        compiler_params=_cparams(("parallel", "arbitrary")),
        name="mlp_ln",
    )(x1, mod3, mod3, mod3, w_m1, b_m1.reshape(1, dff), w_m2, b_m2.reshape(1, d),
      ln_g.reshape(1, d), ln_b.reshape(1, d))


def _filter_features(seq):
    bands_n = (HY_EMB - 1) // 2
    t = jnp.linspace(0.0, 1.0, seq, dtype=F32)[:, None]
    omega = 2.0 * math.pi * jnp.arange(seq, dtype=F32)[:, None] / seq
    bands = jnp.linspace(1e-4, bands_n - 1, bands_n, dtype=F32)[None, :]
    feats = jnp.concatenate([t, jnp.cos(bands * omega), -jnp.sin(bands * omega)], axis=-1)
    return t, feats


def _decay_rates(width):
    min_decay = math.log(HY_DECAY_TARGET) / HY_SLOW_DECAY
    max_decay = math.log(HY_DECAY_TARGET) / HY_FAST_DECAY
    return jnp.abs(jnp.linspace(min_decay, max_decay, width, dtype=F32))[None, :]


def kernel(x, c, w_ada, b_ada, w_in, b_in, sgu_ln_g, sgu_ln_b, sgu_w, sgu_b, hy_conv_w, hy_conv_b, hy_w1, hy_b1, hy_w2, hy_b2, hy_freq, hy_w3, hy_skip, w_branch_a, w_branch_b, w_o, b_o, ln1_g, ln1_b, w_m1, b_m1, w_m2, b_m2, ln2_g, ln2_b):
    bsz, seq, d = x.shape
    depth = w_ada.shape[0]
    alpha = (2.0 * depth) ** 0.25
    sgu_width = w_branch_a.shape[1]
    hy_width = w_branch_b.shape[1]
    assert seq % SGU_CHUNK == 0 and sgu_w.shape[-1] == SGU_CHUNK

    assert seq % (8 * MXU_DIM) == 0
    tables = _dft_stage_tables(seq)
    tcol, feats = _filter_features(seq)
    deltas = _decay_rates(hy_width)

    xt = x.reshape(bsz * seq, d)
    for l in range(depth):
        mod3 = _mod_call(c, w_ada[l], b_ada[l]).reshape(bsz, 1, N_MOD * d)

        z = _inproj_call(xt, mod3, w_in[l].astype(BF16), b_in[l], sgu_ln_g[l], sgu_ln_b[l],
                         seq, sgu_width, hy_width)

        h2 = _filter_mlp_call(feats, hy_w1[l], hy_b1[l], hy_w2[l], hy_b2[l], hy_freq[l])
        kspec = _filter_spec_call(h2, hy_w3[l], tcol, deltas, hy_skip[l], *tables[:3], hy_width)
        yh = _hyena_call(z, hy_conv_w[l], hy_conv_b[l], kspec, tables,
                         bsz, seq, hy_width, 2 * d + 2 * sgu_width)

        gd = sgu_width // SGU_GROUPS
        bs_full = jnp.repeat(sgu_b[l].T, gd, axis=1)
        m = _merge_call(z, yh, sgu_w[l].astype(BF16), bs_full, w_branch_a[l].astype(BF16),
                        w_branch_b[l].astype(BF16), d)

        x1 = _oproj_call(m, xt, mod3, w_o[l].astype(BF16), b_o[l], ln1_g[l], ln1_b[l], seq, alpha)
        xt = _mlp_call(x1, mod3, w_m1[l].astype(BF16), b_m1[l], w_m2[l].astype(BF16), b_m2[l],
                       ln2_g[l], ln2_b[l], seq, alpha)
    return xt.reshape(bsz, seq, d)
```

```python
import functools
import math

import jax
import jax.numpy as jnp
from jax import lax
from jax.experimental import pallas as pl
from jax.experimental.pallas import tpu as pltpu

F32 = jnp.float32
BF16 = jnp.bfloat16

SGU_CHUNK = 128
SGU_GROUPS = 8
HY_ORDER = 2
HY_SHORT = 3
HY_EMB = 33
HY_DECAY_TARGET = 1e-2
HY_FAST_DECAY = 0.3
HY_SLOW_DECAY = 1.5
N_MOD = 6
LN_EPS = 1e-5

LANE = 128
SUBLANES = 8
MXU_DIM = 256
VMEM_LIMIT = 48 * 1024 * 1024


def _cparams(sem):
    return pltpu.CompilerParams(dimension_semantics=sem, vmem_limit_bytes=VMEM_LIMIT)


def _resident(block_shape, index_map):
    return pl.BlockSpec(block_shape, index_map, pipeline_mode=pl.Buffered(1))


def _row_chunks(total, size):
    return [slice(r, r + size) for r in range(0, total, size)]


def _gelu(x):
    return 0.5 * x * (1.0 + lax.erf(x * (1.0 / math.sqrt(2.0))))


def _layer_norm(x, g, b):
    mu = jnp.mean(x, axis=-1, keepdims=True)
    xc = x - mu
    var = jnp.mean(xc * xc, axis=-1, keepdims=True)
    return xc * lax.rsqrt(var + LN_EPS) * g + b


def _dot_bf16x3(a, b):
    m = a.shape[0]
    a_hi = a.astype(BF16)
    b_hi = b.astype(BF16)
    a_hi32 = a_hi.astype(F32)
    b_lo = (b - b_hi.astype(F32)).astype(BF16)
    stacked = jnp.concatenate([a_hi32, a - a_hi32], axis=0).astype(BF16)
    top = jnp.dot(stacked, b_hi, preferred_element_type=F32)
    return top[:m] + top[m:] + jnp.dot(a_hi, b_lo, preferred_element_type=F32)


def _mod_kernel(c_ref, w_ref, b_ref, o_ref):
    c = c_ref[...]
    cond = c * jax.nn.sigmoid(c)
    o_ref[...] = _dot_bf16x3(cond, w_ref[...]) + b_ref[...]


def _mod_call(c, w_ada, b_ada):
    bsz, d = c.shape
    n = w_ada.shape[1]
    tn = 1024
    return pl.pallas_call(
        _mod_kernel,
        grid=(n // tn,),
        in_specs=[pl.BlockSpec((bsz, d), lambda j: (0, 0)),
                  pl.BlockSpec((d, tn), lambda j: (0, j)),
                  pl.BlockSpec((1, tn), lambda j: (0, j))],
        out_specs=pl.BlockSpec((bsz, tn), lambda j: (0, j)),
        out_shape=jax.ShapeDtypeStruct((bsz, n), F32),
        compiler_params=_cparams(("arbitrary",)),
        name="adaln_mod",
    )(c, w_ada, b_ada.reshape(1, n))


def _inproj_kernel(x_ref, sc_ref, sh_ref, w_ref, b_ref, g_ref, be_ref, o_ref, h_scr, *, seg, rc):
    j = pl.program_id(1)
    chunks = _row_chunks(x_ref.shape[0], rc)

    def emit(act, modulate=False):
        for rows in chunks:
            if modulate:
                h_scr[rows, :] = (x_ref[rows, :] * (1.0 + sc_ref[0]) + sh_ref[0]).astype(BF16)
            acc = jnp.dot(h_scr[rows, :], w_ref[...].astype(BF16),
                          preferred_element_type=F32) + b_ref[...]
            o_ref[rows, :] = act(acc).astype(o_ref.dtype)

    @pl.when(j == 0)
    def _():
        emit(jax.nn.sigmoid, modulate=True)

    @pl.when(jnp.logical_and(j > 0, j < seg[0]))
    def _():
        emit(jax.nn.sigmoid)

    @pl.when(jnp.logical_and(j >= seg[0], j < seg[1]))
    def _():
        emit(_gelu)

    @pl.when(jnp.logical_and(j >= seg[1], j < seg[2]))
    def _():
        emit(lambda a: _layer_norm(_gelu(a), g_ref[...], be_ref[...]))

    @pl.when(j >= seg[2])
    def _():
        emit(lambda a: a)


def _inproj_call(xt, mod3, w_in, b_in, ln_g, ln_b, seq, sgu_w, hy_w):
    t, d = xt.shape
    n = w_in.shape[1]
    tm = 1024
    tn = sgu_w
    assert hy_w == sgu_w and seq % tm == 0 and d % tn == 0
    nblk = n // tn
    n_gate = 2 * d // tn
    seg = (n_gate, n_gate + 1, n_gate + 2)
    per_b = seq // tm
    wcol = lambda i, j: (0, (j + (nblk - n_gate)) % nblk)
    return pl.pallas_call(
        functools.partial(_inproj_kernel, seg=seg, rc=256),
        grid=(t // tm, n // tn),
        in_specs=[pl.BlockSpec((tm, d), lambda i, j: (i, 0)),
                  pl.BlockSpec((1, 1, d), lambda i, j: (i // per_b, 0, 1)),
                  pl.BlockSpec((1, 1, d), lambda i, j: (i // per_b, 0, 0)),
                  pl.BlockSpec((d, tn), wcol),
                  pl.BlockSpec((1, tn), wcol),
                  pl.BlockSpec((1, tn), lambda i, j: (0, 0)),
                  pl.BlockSpec((1, tn), lambda i, j: (0, 0))],
        out_specs=pl.BlockSpec((tm, tn), lambda i, j: (i, j)),
        out_shape=jax.ShapeDtypeStruct((t, n), BF16),
        scratch_shapes=[pltpu.VMEM((tm, d), BF16)],
        compiler_params=_cparams(("parallel", "arbitrary")),
        name="in_proj",
    )(xt, mod3, mod3, w_in, b_in.reshape(1, n), ln_g.reshape(1, tn), ln_b.reshape(1, tn))


def _filter_mlp_kernel(f_ref, w1_ref, b1_ref, w2_ref, b2_ref, fr_ref, o_ref):
    hp = lax.Precision.HIGHEST
    fr = fr_ref[...]
    h = jnp.sin(fr * (jnp.dot(f_ref[...], w1_ref[...], preferred_element_type=F32, precision=hp)
                      + b1_ref[...]))
    o_ref[...] = jnp.sin(fr * (jnp.dot(h, w2_ref[...], preferred_element_type=F32, precision=hp)
                               + b2_ref[...]))


def _filter_mlp_call(feats, w1, b1, w2, b2, freq):
    seq = feats.shape[0]
    hid = w1.shape[1]
    return pl.pallas_call(
        _filter_mlp_kernel,
        out_shape=jax.ShapeDtypeStruct((seq, hid), F32),
        name="hyena_filter_mlp",
    )(feats, w1, b1.reshape(1, hid), w2, b2.reshape(1, hid), freq.reshape(1, hid))


DFT_GROUPS = 9
DFT_SLABS = 8
DFT_COMPS = 16
DFT_J = 32


def _packed_start(g, n2):
    return {0: 0, 8: n2}.get(g, 2 * n2 * g)


def _store_slab_block_order(dst_ref, val, t1, n2):
    step = DFT_SLABS * DFT_J
    for k in range(n2 // DFT_J):
        dst = step * k + DFT_J * t1
        dst_ref[dst:dst + DFT_J, :] = val[DFT_J * k:DFT_J * (k + 1), :].astype(dst_ref.dtype)


def _store_block_order(dst_ref, val, n2):
    for t1 in range(DFT_SLABS):
        _store_slab_block_order(dst_ref, val[n2 * t1:n2 * (t1 + 1), :], t1, n2)


def _stage1_fwd(s1_ref, src_ref, dst_ref, n2):
    step = DFT_SLABS * DFT_J
    for k in range(n2 // DFT_J):
        comps = jnp.dot(s1_ref[...], src_ref[step * k:step * (k + 1), :], preferred_element_type=F32)
        for c in range(DFT_COMPS):
            dst_ref[n2 * c + DFT_J * k:n2 * c + DFT_J * (k + 1), :] = (
                comps[DFT_J * c:DFT_J * (c + 1), :].astype(dst_ref.dtype))


def _stage2_fwd(g, tr_ref, tc_ref, src_ref, n2):
    start = _packed_start(g, n2)
    if g in (0, 8):
        return jnp.dot(tr_ref[0 if g == 0 else 1], src_ref[start:start + n2, :],
                       preferred_element_type=F32)
    return jnp.dot(tc_ref[g - 1], src_ref[start:start + 2 * n2, :], preferred_element_type=F32)


def _dft_stage_tables(seq):
    n = 2 * seq
    n2 = n // 16
    f2 = jnp.arange(n2, dtype=jnp.int32)[:, None]
    t2 = jnp.arange(n2, dtype=jnp.int32)[None, :]

    def cs(g):
        ang = ((t2 * (g + 16 * f2)) & (n - 1)).astype(F32) * (2.0 * math.pi / n)
        return jnp.cos(ang), jnp.sin(ang)

    real = jnp.stack([jnp.concatenate([c, -s], axis=0) for c, s in (cs(0), cs(8))])
    cplx = jnp.stack([jnp.block([[c, s], [-s, c]]) for c, s in (cs(g) for g in range(1, 8))])

    t1 = jnp.arange(DFT_SLABS, dtype=jnp.int32)
    rows = [jnp.ones((DFT_SLABS,), F32), jnp.where(t1 % 2 == 0, 1.0, -1.0).astype(F32)]
    for g in range(1, 8):
        ang = ((g * t1) % 16).astype(F32) * (math.pi / 8.0)
        rows += [jnp.cos(ang), -jnp.sin(ang)]
    s1 = jnp.kron(jnp.stack(rows), jnp.eye(DFT_J, dtype=F32))
    return tuple(t.astype(BF16) for t in (s1, real, cplx, s1.T, jnp.swapaxes(real, 1, 2),
                                          jnp.swapaxes(cplx, 1, 2)))


def _filter_spec_kernel(h_ref, wf_ref, wb_ref, t_ref, dl_ref, skip_ref, s1_ref, tr_ref, tc_ref,
                        k_ref, e_scr, o_scr, ge_scr, go_scr):
    seq = h_ref.shape[0]
    n2 = seq // 8
    window = jnp.exp(-t_ref[...] * dl_ref[...])
    _store_block_order(e_scr, _dot_bf16x3(h_ref[...], wf_ref[...] + wb_ref[...]) * window, n2)
    _store_block_order(o_scr, _dot_bf16x3(h_ref[...], wf_ref[...] - wb_ref[...]) * window, n2)
    _stage1_fwd(s1_ref, e_scr, ge_scr, n2)
    _stage1_fwd(s1_ref, o_scr, go_scr, n2)
    for g in range(DFT_GROUPS):
        wgt = (1.0 if g in (0, 8) else 2.0) / (2 * seq)
        rows = slice(2 * n2 * g, 2 * n2 * g + n2)
        rows_im = slice(2 * n2 * g + n2, 2 * n2 * (g + 1))
        k_ref[0, rows, :] = ((_stage2_fwd(g, tr_ref, tc_ref, ge_scr, n2)[:n2] + skip_ref[0])
                             * wgt).astype(k_ref.dtype)
        k_ref[0, rows_im, :] = (_stage2_fwd(g, tr_ref, tc_ref, go_scr, n2)[n2:]
                                * wgt).astype(k_ref.dtype)


def _filter_spec_call(h2, w3, tcol, deltas, skip, t_s1, t_real, t_cplx, width):
    seq, hid = h2.shape
    n2 = seq // 8
    cb = MXU_DIM
    ncb = width // cb
    per_dir = HY_ORDER * ncb
    rows = DFT_GROUPS * 2 * n2
    const = lambda n, k: (0, 0, 0)
    return pl.pallas_call(
        _filter_spec_kernel,
        grid=(HY_ORDER, ncb),
        in_specs=[_resident((seq, hid), lambda n, k: (0, 0)),
                  pl.BlockSpec((hid, cb), lambda n, k: (0, n * ncb + k)),
                  pl.BlockSpec((hid, cb), lambda n, k: (0, per_dir + n * ncb + k)),
                  _resident((seq, 1), lambda n, k: (0, 0)),
                  pl.BlockSpec((1, cb), lambda n, k: (0, k)),
                  pl.BlockSpec((1, 1, cb), lambda n, k: (n, 0, k)),
                  _resident(t_s1.shape, lambda n, k: (0, 0)),
                  _resident(t_real.shape, const), _resident(t_cplx.shape, const)],
        out_specs=pl.BlockSpec((1, rows, cb), lambda n, k: (n, 0, k)),
        out_shape=jax.ShapeDtypeStruct((HY_ORDER, rows, width), BF16),
        scratch_shapes=[pltpu.VMEM((seq, cb), BF16), pltpu.VMEM((seq, cb), BF16),
                        pltpu.VMEM((2 * seq, cb), BF16), pltpu.VMEM((2 * seq, cb), BF16)],
        compiler_params=_cparams(("arbitrary", "arbitrary")),
        name="hyena_filter_spectrum",
    )(h2, w3, w3, tcol, deltas, skip.reshape(HY_ORDER, 1, width), t_s1, t_real, t_cplx)


def _short_conv(z, w, b):
    seq = z.shape[0]
    prev = pltpu.roll(z, 1, 0)
    nxt = pltpu.roll(z, seq - 1, 0)
    row = lax.broadcasted_iota(jnp.int32, (SUBLANES, z.shape[1]), 0)
    prev = jnp.concatenate([jnp.where(row == 0, 0.0, prev[:SUBLANES]), prev[SUBLANES:]], axis=0)
    nxt = jnp.concatenate([nxt[:-SUBLANES], jnp.where(row == SUBLANES - 1, 0.0, nxt[-SUBLANES:])],
                          axis=0)
    return w[0:1] * prev + w[1:2] * z + w[2:3] * nxt + b


def _hyena_kernel(v_ref, x1_ref, x2_ref, wv_ref, w1_ref, w2_ref, bv_ref, b1_ref, b2_ref,
                  k_ref, s1_ref, tr_ref, tc_ref, s1t_ref, ur_ref, uc_ref, o_ref,
                  y16_scr, gate_scr, g_scr, yc_scr, h_scr):
    seq, cb = v_ref.shape
    n2 = seq // DFT_SLABS
    step = DFT_SLABS * DFT_J

    conv3 = lambda z_ref, w_ref, b_ref: _short_conv(z_ref[...].astype(F32), w_ref[...], b_ref[...])
    _store_block_order(y16_scr, conv3(v_ref, wv_ref, bv_ref), n2)
    _store_block_order(gate_scr.at[0], conv3(x1_ref, w1_ref, b1_ref), n2)
    _store_block_order(gate_scr.at[1], conv3(x2_ref, w2_ref, b2_ref), n2)

    for n in range(HY_ORDER):
        last = n + 1 == HY_ORDER
        _stage1_fwd(s1_ref, y16_scr, g_scr, n2)
        for g in range(DFT_GROUPS):
            x = _stage2_fwd(g, tr_ref, tc_ref, g_scr, n2).astype(BF16)
            xr, xi = x[:n2], x[n2:]
            base = 2 * n2 * g
            kr = k_ref[n, base:base + n2, :]
            ki = k_ref[n, base + n2:base + 2 * n2, :]
            yc_scr[base:base + n2, :] = xr * kr - xi * ki
            yc_scr[base + n2:base + 2 * n2, :] = xr * ki + xi * kr
        for g in range(DFT_GROUPS):
            ycat = yc_scr[2 * n2 * g:2 * n2 * (g + 1), :]
            if g in (0, 8):
                parts = [(0 if g == 0 else 1,
                          jnp.dot(ur_ref[0 if g == 0 else 1], ycat, preferred_element_type=F32))]
            else:
                h = jnp.dot(uc_ref[g - 1], ycat, preferred_element_type=F32)
                parts = [(2 * g, h[:n2]), (2 * g + 1, h[n2:])]
            for c, hp in parts:
                for k in range(n2 // DFT_J):
                    dst = DFT_COMPS * DFT_J * k + DFT_J * c
                    h_scr[dst:dst + DFT_J, :] = hp[DFT_J * k:DFT_J * (k + 1), :].astype(BF16)
        for k in range(n2 // DFT_J):
            rows = slice(step * k, step * (k + 1))
            conv = jnp.dot(s1t_ref[...], h_scr[DFT_COMPS * DFT_J * k:DFT_COMPS * DFT_J * (k + 1), :],
                           preferred_element_type=F32)
            y_new = (gate_scr[n, rows, :] * conv).astype(BF16)
            if last:
                for t1 in range(DFT_SLABS):
                    dst = n2 * t1 + DFT_J * k
                    o_ref[dst:dst + DFT_J, :] = y_new[DFT_J * t1:DFT_J * (t1 + 1), :]
            else:
                y16_scr[rows, :] = y_new


def _hyena_call(z, conv_w, conv_b, kspec, tables, bsz, seq, width, col0):
    cb = MXU_DIM
    ncb = width // cb
    base = col0 // cb
    zspec = lambda part: pl.BlockSpec((seq, cb), lambda k, b: (b, base + part * ncb + k))
    wspec = lambda part: pl.BlockSpec((HY_SHORT, cb), lambda k, b: (0, part * ncb + k))
    bspec = lambda part: pl.BlockSpec((1, cb), lambda k, b: (0, part * ncb + k))
    cb_all = conv_b.reshape(1, -1)
    const = lambda t: (lambda k, b: (0,) * t.ndim)
    return pl.pallas_call(
        _hyena_kernel,
        grid=(ncb, bsz),
        in_specs=[zspec(0), zspec(1), zspec(2), wspec(0), wspec(1), wspec(2),
                  bspec(0), bspec(1), bspec(2),
                  _resident((HY_ORDER, kspec.shape[1], cb), lambda k, b: (0, 0, k))]
                 + [_resident(t.shape, const(t)) for t in tables],
        out_specs=pl.BlockSpec((seq, cb), lambda k, b: (b, k)),
        out_shape=jax.ShapeDtypeStruct((bsz * seq, width), BF16),
        scratch_shapes=[pltpu.VMEM((seq, cb), BF16),
                        pltpu.VMEM((HY_ORDER, seq, cb), F32),
                        pltpu.VMEM((2 * seq, cb), BF16),
                        pltpu.VMEM((kspec.shape[1], cb), BF16),
                        pltpu.VMEM((2 * seq, cb), BF16)],
        compiler_params=_cparams(("arbitrary", "arbitrary")),
        name="hyena_mixer",
    )(z, z, z, conv_w, conv_w, conv_w, cb_all, cb_all, cb_all, kspec, *tables)


def _merge_kernel(u_ref, vn_ref, yh_ref, ga_ref, gb_ref, ws_ref, bs_ref, wa_ref, wb_ref, o_ref,
                  s_scr):
    tm = u_ref.shape[0]
    gd = u_ref.shape[1] // SGU_GROUPS
    for n in range(tm // SGU_CHUNK):
        rows = slice(n * SGU_CHUNK, (n + 1) * SGU_CHUNK)
        for g in range(SGU_GROUPS):
            cols = slice(g * gd, (g + 1) * gd)
            mixed = jnp.dot(ws_ref[g], vn_ref[rows, cols], preferred_element_type=F32)
            s_scr[rows, cols] = (u_ref[rows, cols].astype(F32)
                                 * (mixed + bs_ref[:, cols])).astype(BF16)
    y_a = jnp.dot(s_scr[...], wa_ref[...].astype(BF16), preferred_element_type=F32)
    y_b = jnp.dot(yh_ref[...], wb_ref[...].astype(BF16), preferred_element_type=F32)
    o_ref[...] = (ga_ref[...].astype(F32) * y_a + gb_ref[...].astype(F32) * y_b).astype(o_ref.dtype)


def _merge_call(z, yh, sgu_w16, sgu_bfull, w_a, w_b, d):
    t = z.shape[0]
    wdt = w_a.shape[0]
    tm = 512
    u_blk = 2 * d // wdt
    return pl.pallas_call(
        _merge_kernel,
        grid=(t // tm,),
        in_specs=[pl.BlockSpec((tm, wdt), lambda i: (i, u_blk)),
                  pl.BlockSpec((tm, wdt), lambda i: (i, u_blk + 1)),
                  pl.BlockSpec((tm, wdt), lambda i: (i, 0)),
                  pl.BlockSpec((tm, d), lambda i: (i, 0)),
                  pl.BlockSpec((tm, d), lambda i: (i, 1)),
                  _resident(sgu_w16.shape, lambda i: (0, 0, 0)),
                  _resident(sgu_bfull.shape, lambda i: (0, 0)),
                  _resident(w_a.shape, lambda i: (0, 0)),
                  _resident(w_b.shape, lambda i: (0, 0))],
        out_specs=pl.BlockSpec((tm, d), lambda i: (i, 0)),
        out_shape=jax.ShapeDtypeStruct((t, d), BF16),
        scratch_shapes=[pltpu.VMEM((tm, wdt), BF16)],
        compiler_params=_cparams(("parallel",)),
        name="gated_merge",
    )(z, z, yh, z, z, sgu_w16, sgu_bfull, w_a, w_b)


def _oproj_kernel(m_ref, x_ref, g1_ref, w_ref, b_ref, lg_ref, lb_ref, o_ref, *, alpha, rc):
    for rows in _row_chunks(m_ref.shape[0], rc):
        mix = jnp.dot(m_ref[rows, :], w_ref[...].astype(BF16),
                      preferred_element_type=F32) + b_ref[...]
        o_ref[rows, :] = _layer_norm(alpha * x_ref[rows, :] + g1_ref[0] * mix,
                                     lg_ref[...], lb_ref[...])


def _oproj_call(m, xt, mod3, w_o, b_o, ln_g, ln_b, seq, alpha):
    t, d = xt.shape
    tm = 512
    per_b = seq // tm
    row = lambda i: (0, 0)
    return pl.pallas_call(
        functools.partial(_oproj_kernel, alpha=alpha, rc=256),
        grid=(t // tm,),
        in_specs=[pl.BlockSpec((tm, d), lambda i: (i, 0)),
                  pl.BlockSpec((tm, d), lambda i: (i, 0)),
                  pl.BlockSpec((1, 1, d), lambda i: (i // per_b, 0, 2)),
                  _resident((d, d), row),
                  pl.BlockSpec((1, d), row), pl.BlockSpec((1, d), row), pl.BlockSpec((1, d), row)],
        out_specs=pl.BlockSpec((tm, d), lambda i: (i, 0)),
        out_shape=jax.ShapeDtypeStruct((t, d), F32),
        compiler_params=_cparams(("parallel",)),
        name="out_proj_ln",
    )(m, xt, mod3, w_o, b_o.reshape(1, d), ln_g.reshape(1, d), ln_b.reshape(1, d))


def _mlp_kernel(x_ref, sc_ref, sh_ref, g2_ref, w1_ref, b1_ref, w2_ref, b2_ref, lg_ref, lb_ref,
                o_ref, h_scr, acc_scr, *, alpha, rc):
    j = pl.program_id(1)
    last = pl.num_programs(1) - 1
    chunks = _row_chunks(x_ref.shape[0], rc)

    def partial_out(rows):
        hid = jnp.dot(h_scr[rows, :], w1_ref[...], preferred_element_type=F32) + b1_ref[...]
        hid = jnp.square(jnp.maximum(hid, 0.0)).astype(BF16)
        return jnp.dot(hid, w2_ref[...], preferred_element_type=F32)

    @pl.when(j == 0)
    def _():
        for rows in chunks:
            h_scr[rows, :] = (x_ref[rows, :] * (1.0 + sc_ref[0]) + sh_ref[0]).astype(BF16)
            acc_scr[rows, :] = partial_out(rows)

    @pl.when(jnp.logical_and(j > 0, j < last))
    def _():
        acc_scr[...] += partial_out(slice(None))

    @pl.when(j == last)
    def _():
        for rows in chunks:
            f = acc_scr[rows, :] + partial_out(rows) + b2_ref[...]
            o_ref[rows, :] = _layer_norm(alpha * x_ref[rows, :] + g2_ref[0] * f,
                                         lg_ref[...], lb_ref[...])


def _mlp_call(x1, mod3, w_m1, b_m1, w_m2, b_m2, ln_g, ln_b, seq, alpha):
    t, d = x1.shape
    dff = w_m1.shape[1]
    tm = 512
    tf = 1024
    per_b = seq // tm
    row = lambda i, j: (0, 0)
    return pl.pallas_call(
        functools.partial(_mlp_kernel, alpha=alpha, rc=256),
        grid=(t // tm, dff // tf),
        in_specs=[pl.BlockSpec((tm, d), lambda i, j: (i, 0)),
                  pl.BlockSpec((1, 1, d), lambda i, j: (i // per_b, 0, 4)),
                  pl.BlockSpec((1, 1, d), lambda i, j: (i // per_b, 0, 3)),
                  pl.BlockSpec((1, 1, d), lambda i, j: (i // per_b, 0, 5)),
                  pl.BlockSpec((d, tf), lambda i, j: (0, j)),
                  pl.BlockSpec((1, tf), lambda i, j: (0, j)),
                  pl.BlockSpec((tf, d), lambda i, j: (j, 0)),
                  pl.BlockSpec((1, d), row), pl.BlockSpec((1, d), row), pl.BlockSpec((1, d), row)],
        out_specs=pl.BlockSpec((tm, d), lambda i, j: (i, 0)),
        out_shape=jax.ShapeDtypeStruct((t, d), F32),
        scratch_shapes=[pltpu.VMEM((tm, d), BF16), pltpu.VMEM((tm, d), F32)],
        compiler_params=_cparams(("parallel", "arbitrary")),
        name="mlp_ln",
    )(x1, mod3, mod3, mod3, w_m1, b_m1.reshape(1, dff), w_m2, b_m2.reshape(1, d),
      ln_g.reshape(1, d), ln_b.reshape(1, d))


def _filter_features(seq):
    bands_n = (HY_EMB - 1) // 2
    t = jnp.linspace(0.0, 1.0, seq, dtype=F32)[:, None]
    omega = 2.0 * math.pi * jnp.arange(seq, dtype=F32)[:, None] / seq
    bands = jnp.linspace(1e-4, bands_n - 1, bands_n, dtype=F32)[None, :]
    feats = jnp.concatenate([t, jnp.cos(bands * omega), -jnp.sin(bands * omega)], axis=-1)
    return t, feats


def _decay_rates(width):
    min_decay = math.log(HY_DECAY_TARGET) / HY_SLOW_DECAY
    max_decay = math.log(HY_DECAY_TARGET) / HY_FAST_DECAY
    return jnp.abs(jnp.linspace(min_decay, max_decay, width, dtype=F32))[None, :]


def kernel(x, c, w_ada, b_ada, w_in, b_in, sgu_ln_g, sgu_ln_b, sgu_w, sgu_b, hy_conv_w, hy_conv_b, hy_w1, hy_b1, hy_w2, hy_b2, hy_freq, hy_w3, hy_skip, w_branch_a, w_branch_b, w_o, b_o, ln1_g, ln1_b, w_m1, b_m1, w_m2, b_m2, ln2_g, ln2_b):
    bsz, seq, d = x.shape
    depth = w_ada.shape[0]
    alpha = (2.0 * depth) ** 0.25
    sgu_width = w_branch_a.shape[1]
    hy_width = w_branch_b.shape[1]
    assert seq % SGU_CHUNK == 0 and sgu_w.shape[-1] == SGU_CHUNK

    assert seq % (8 * MXU_DIM) == 0
    tables = _dft_stage_tables(seq)
    tcol, feats = _filter_features(seq)
    deltas = _decay_rates(hy_width)

    xt = x.reshape(bsz * seq, d)
    for l in range(depth):
        mod3 = _mod_call(c, w_ada[l], b_ada[l]).reshape(bsz, 1, N_MOD * d)

        z = _inproj_call(xt, mod3, w_in[l], b_in[l], sgu_ln_g[l], sgu_ln_b[l],
                         seq, sgu_width, hy_width)

        h2 = _filter_mlp_call(feats, hy_w1[l], hy_b1[l], hy_w2[l], hy_b2[l], hy_freq[l])
        kspec = _filter_spec_call(h2, hy_w3[l], tcol, deltas, hy_skip[l], *tables[:3], hy_width)
        yh = _hyena_call(z, hy_conv_w[l], hy_conv_b[l], kspec, tables,
                         bsz, seq, hy_width, 2 * d + 2 * sgu_width)

        gd = sgu_width // SGU_GROUPS
        bs_full = jnp.repeat(sgu_b[l].T, gd, axis=1)
        m = _merge_call(z, yh, sgu_w[l].astype(BF16), bs_full, w_branch_a[l], w_branch_b[l], d)

        x1 = _oproj_call(m, xt, mod3, w_o[l], b_o[l], ln1_g[l], ln1_b[l], seq, alpha)
        xt = _mlp_call(x1, mod3, w_m1[l].astype(BF16), b_m1[l], w_m2[l].astype(BF16), b_m2[l],
                       ln2_g[l], ln2_b[l], seq, alpha)
    return xt.reshape(bsz, seq, d)
```

```python
import functools
import math

import jax
import jax.numpy as jnp
from jax import lax
from jax.experimental import pallas as pl
from jax.experimental.pallas import tpu as pltpu

F32 = jnp.float32
BF16 = jnp.bfloat16

SGU_CHUNK = 128
SGU_GROUPS = 8
HY_ORDER = 2
HY_SHORT = 3
HY_EMB = 33
HY_DECAY_TARGET = 1e-2
HY_FAST_DECAY = 0.3
HY_SLOW_DECAY = 1.5
N_MOD = 6
LN_EPS = 1e-5

LANE = 128
SUBLANES = 8
MXU_DIM = 256
VMEM_LIMIT = 48 * 1024 * 1024


def _cparams(sem):
    return pltpu.CompilerParams(dimension_semantics=sem, vmem_limit_bytes=VMEM_LIMIT)


def _resident(block_shape, index_map):
    return pl.BlockSpec(block_shape, index_map, pipeline_mode=pl.Buffered(1))


def _row_chunks(total, size):
    return [slice(r, r + size) for r in range(0, total, size)]


def _gelu(x):
    return 0.5 * x * (1.0 + lax.erf(x * (1.0 / math.sqrt(2.0))))


def _layer_norm(x, g, b):
    mu = jnp.mean(x, axis=-1, keepdims=True)
    xc = x - mu
    var = jnp.mean(xc * xc, axis=-1, keepdims=True)
    return xc * lax.rsqrt(var + LN_EPS) * g + b


def _dot_bf16x3(a, b):
    m = a.shape[0]
    a_hi = a.astype(BF16)
    b_hi = b.astype(BF16)
    a_hi32 = a_hi.astype(F32)
    b_lo = (b - b_hi.astype(F32)).astype(BF16)
    stacked = jnp.concatenate([a_hi32, a - a_hi32], axis=0).astype(BF16)
    top = jnp.dot(stacked, b_hi, preferred_element_type=F32)
    return top[:m] + top[m:] + jnp.dot(a_hi, b_lo, preferred_element_type=F32)


def _mod_kernel(c_ref, w_ref, b_ref, o_ref):
    c = c_ref[...]
    cond = c * jax.nn.sigmoid(c)
    o_ref[...] = _dot_bf16x3(cond, w_ref[...]) + b_ref[...]


def _mod_call(c, w_ada, b_ada):
    bsz, d = c.shape
    n = w_ada.shape[1]
    tn = 1024
    return pl.pallas_call(
        _mod_kernel,
        grid=(n // tn,),
        in_specs=[pl.BlockSpec((bsz, d), lambda j: (0, 0)),
                  pl.BlockSpec((d, tn), lambda j: (0, j)),
                  pl.BlockSpec((1, tn), lambda j: (0, j))],
        out_specs=pl.BlockSpec((bsz, tn), lambda j: (0, j)),
        out_shape=jax.ShapeDtypeStruct((bsz, n), F32),
        compiler_params=_cparams(("arbitrary",)),
        name="adaln_mod",
    )(c, w_ada, b_ada.reshape(1, n))


def _inproj_kernel(x_ref, sc_ref, sh_ref, w_ref, b_ref, g_ref, be_ref, o_ref, h_scr, *, seg, rc):
    j = pl.program_id(1)
    chunks = _row_chunks(x_ref.shape[0], rc)

    def emit(act, modulate=False):
        for rows in chunks:
            if modulate:
                h_scr[rows, :] = (x_ref[rows, :] * (1.0 + sc_ref[0]) + sh_ref[0]).astype(BF16)
            acc = jnp.dot(h_scr[rows, :], w_ref[...].astype(BF16),
                          preferred_element_type=F32) + b_ref[...]
            o_ref[rows, :] = act(acc).astype(o_ref.dtype)

    @pl.when(j == 0)
    def _():
        emit(jax.nn.sigmoid, modulate=True)

    @pl.when(jnp.logical_and(j > 0, j < seg[0]))
    def _():
        emit(jax.nn.sigmoid)

    @pl.when(jnp.logical_and(j >= seg[0], j < seg[1]))
    def _():
        emit(_gelu)

    @pl.when(jnp.logical_and(j >= seg[1], j < seg[2]))
    def _():
        emit(lambda a: _layer_norm(_gelu(a), g_ref[...], be_ref[...]))

    @pl.when(j >= seg[2])
    def _():
        emit(lambda a: a)


def _inproj_call(xt, mod3, w_in, b_in, ln_g, ln_b, seq, sgu_w, hy_w):
    t, d = xt.shape
    n = w_in.shape[1]
    tm = 1024
    tn = sgu_w
    assert hy_w == sgu_w and seq % tm == 0 and d % tn == 0
    nblk = n // tn
    n_gate = 2 * d // tn
    seg = (n_gate, n_gate + 1, n_gate + 2)
    per_b = seq // tm
    wcol = lambda i, j: (0, (j + (nblk - n_gate)) % nblk)
    return pl.pallas_call(
        functools.partial(_inproj_kernel, seg=seg, rc=256),
        grid=(t // tm, n // tn),
        in_specs=[pl.BlockSpec((tm, d), lambda i, j: (i, 0)),
                  pl.BlockSpec((1, 1, d), lambda i, j: (i // per_b, 0, 1)),
                  pl.BlockSpec((1, 1, d), lambda i, j: (i // per_b, 0, 0)),
                  pl.BlockSpec((d, tn), wcol),
                  pl.BlockSpec((1, tn), wcol),
                  pl.BlockSpec((1, tn), lambda i, j: (0, 0)),
                  pl.BlockSpec((1, tn), lambda i, j: (0, 0))],
        out_specs=pl.BlockSpec((tm, tn), lambda i, j: (i, j)),
        out_shape=jax.ShapeDtypeStruct((t, n), BF16),
        scratch_shapes=[pltpu.VMEM((tm, d), BF16)],
        compiler_params=_cparams(("parallel", "arbitrary")),
        name="in_proj",
    )(xt, mod3, mod3, w_in, b_in.reshape(1, n), ln_g.reshape(1, tn), ln_b.reshape(1, tn))


def _filter_mlp_kernel(f_ref, w1_ref, b1_ref, w2_ref, b2_ref, fr_ref, o_ref):
    hp = lax.Precision.HIGHEST
    fr = fr_ref[...]
    h = jnp.sin(fr * (jnp.dot(f_ref[...], w1_ref[...], preferred_element_type=F32, precision=hp)
                      + b1_ref[...]))
    o_ref[...] = jnp.sin(fr * (jnp.dot(h, w2_ref[...], preferred_element_type=F32, precision=hp)
                               + b2_ref[...]))


def _filter_mlp_call(feats, w1, b1, w2, b2, freq):
    seq = feats.shape[0]
    hid = w1.shape[1]
    return pl.pallas_call(
        _filter_mlp_kernel,
        out_shape=jax.ShapeDtypeStruct((seq, hid), F32),
        name="hyena_filter_mlp",
    )(feats, w1, b1.reshape(1, hid), w2, b2.reshape(1, hid), freq.reshape(1, hid))


DFT_GROUPS = 9
DFT_SLABS = 8
DFT_COMPS = 16
DFT_J = 32


def _packed_start(g, n2):
    return {0: 0, 8: n2}.get(g, 2 * n2 * g)


def _store_slab_block_order(dst_ref, val, t1, n2):
    step = DFT_SLABS * DFT_J
    for k in range(n2 // DFT_J):
        dst = step * k + DFT_J * t1
        dst_ref[dst:dst + DFT_J, :] = val[DFT_J * k:DFT_J * (k + 1), :].astype(dst_ref.dtype)


def _store_block_order(dst_ref, val, n2):
    for t1 in range(DFT_SLABS):
        _store_slab_block_order(dst_ref, val[n2 * t1:n2 * (t1 + 1), :], t1, n2)


def _stage1_fwd(s1_ref, src_ref, dst_ref, n2):
    step = DFT_SLABS * DFT_J
    for k in range(n2 // DFT_J):
        comps = jnp.dot(s1_ref[...], src_ref[step * k:step * (k + 1), :], preferred_element_type=F32)
        for c in range(DFT_COMPS):
            dst_ref[n2 * c + DFT_J * k:n2 * c + DFT_J * (k + 1), :] = (
                comps[DFT_J * c:DFT_J * (c + 1), :].astype(dst_ref.dtype))


def _stage2_fwd(g, tr_ref, tc_ref, src_ref, n2):
    start = _packed_start(g, n2)
    if g in (0, 8):
        return jnp.dot(tr_ref[0 if g == 0 else 1], src_ref[start:start + n2, :],
                       preferred_element_type=F32)
    return jnp.dot(tc_ref[g - 1], src_ref[start:start + 2 * n2, :], preferred_element_type=F32)


def _dft_stage_tables(seq):
    n = 2 * seq
    n2 = n // 16
    f2 = jnp.arange(n2, dtype=jnp.int32)[:, None]
    t2 = jnp.arange(n2, dtype=jnp.int32)[None, :]

    def cs(g):
        ang = ((t2 * (g + 16 * f2)) & (n - 1)).astype(F32) * (2.0 * math.pi / n)
        return jnp.cos(ang), jnp.sin(ang)

    real = jnp.stack([jnp.concatenate([c, -s], axis=0) for c, s in (cs(0), cs(8))])
    cplx = jnp.stack([jnp.block([[c, s], [-s, c]]) for c, s in (cs(g) for g in range(1, 8))])

    t1 = jnp.arange(DFT_SLABS, dtype=jnp.int32)
    rows = [jnp.ones((DFT_SLABS,), F32), jnp.where(t1 % 2 == 0, 1.0, -1.0).astype(F32)]
    for g in range(1, 8):
        ang = ((g * t1) % 16).astype(F32) * (math.pi / 8.0)
        rows += [jnp.cos(ang), -jnp.sin(ang)]
    s1 = jnp.kron(jnp.stack(rows), jnp.eye(DFT_J, dtype=F32))
    return tuple(t.astype(BF16) for t in (s1, real, cplx, s1.T, jnp.swapaxes(real, 1, 2),
                                          jnp.swapaxes(cplx, 1, 2)))


def _filter_spec_kernel(h_ref, wf_ref, wb_ref, t_ref, dl_ref, skip_ref, s1_ref, tr_ref, tc_ref,
                        k_ref, e_scr, o_scr, ge_scr, go_scr):
    seq = h_ref.shape[0]
    n2 = seq // 8
    window = jnp.exp(-t_ref[...] * dl_ref[...])
    _store_block_order(e_scr, _dot_bf16x3(h_ref[...], wf_ref[...] + wb_ref[...]) * window, n2)
    _store_block_order(o_scr, _dot_bf16x3(h_ref[...], wf_ref[...] - wb_ref[...]) * window, n2)
    _stage1_fwd(s1_ref, e_scr, ge_scr, n2)
    _stage1_fwd(s1_ref, o_scr, go_scr, n2)
    for g in range(DFT_GROUPS):
        wgt = (1.0 if g in (0, 8) else 2.0) / (2 * seq)
        rows = slice(2 * n2 * g, 2 * n2 * g + n2)
        rows_im = slice(2 * n2 * g + n2, 2 * n2 * (g + 1))
        k_ref[0, rows, :] = ((_stage2_fwd(g, tr_ref, tc_ref, ge_scr, n2)[:n2] + skip_ref[0])
                             * wgt).astype(k_ref.dtype)
        k_ref[0, rows_im, :] = (_stage2_fwd(g, tr_ref, tc_ref, go_scr, n2)[n2:]
                                * wgt).astype(k_ref.dtype)


def _filter_spec_call(h2, w3, tcol, deltas, skip, t_s1, t_real, t_cplx, width):
    seq, hid = h2.shape
    n2 = seq // 8
    cb = MXU_DIM
    ncb = width // cb
    per_dir = HY_ORDER * ncb
    rows = DFT_GROUPS * 2 * n2
    const = lambda n, k: (0, 0, 0)
    return pl.pallas_call(
        _filter_spec_kernel,
        grid=(HY_ORDER, ncb),
        in_specs=[_resident((seq, hid), lambda n, k: (0, 0)),
                  pl.BlockSpec((hid, cb), lambda n, k: (0, n * ncb + k)),
                  pl.BlockSpec((hid, cb), lambda n, k: (0, per_dir + n * ncb + k)),
                  _resident((seq, 1), lambda n, k: (0, 0)),
                  pl.BlockSpec((1, cb), lambda n, k: (0, k)),
                  pl.BlockSpec((1, 1, cb), lambda n, k: (n, 0, k)),
                  _resident(t_s1.shape, lambda n, k: (0, 0)),
                  _resident(t_real.shape, const), _resident(t_cplx.shape, const)],
        out_specs=pl.BlockSpec((1, rows, cb), lambda n, k: (n, 0, k)),
        out_shape=jax.ShapeDtypeStruct((HY_ORDER, rows, width), BF16),
        scratch_shapes=[pltpu.VMEM((seq, cb), BF16), pltpu.VMEM((seq, cb), BF16),
                        pltpu.VMEM((2 * seq, cb), BF16), pltpu.VMEM((2 * seq, cb), BF16)],
        compiler_params=_cparams(("arbitrary", "arbitrary")),
        name="hyena_filter_spectrum",
    )(h2, w3, w3, tcol, deltas, skip.reshape(HY_ORDER, 1, width), t_s1, t_real, t_cplx)


def _short_conv(z, w, b):
    seq = z.shape[0]
    prev = pltpu.roll(z, 1, 0)
    nxt = pltpu.roll(z, seq - 1, 0)
    row = lax.broadcasted_iota(jnp.int32, (SUBLANES, z.shape[1]), 0)
    prev = jnp.concatenate([jnp.where(row == 0, 0.0, prev[:SUBLANES]), prev[SUBLANES:]], axis=0)
    nxt = jnp.concatenate([nxt[:-SUBLANES], jnp.where(row == SUBLANES - 1, 0.0, nxt[-SUBLANES:])],
                          axis=0)
    return w[0:1] * prev + w[1:2] * z + w[2:3] * nxt + b


def _hyena_kernel(v_ref, x1_ref, x2_ref, wv_ref, w1_ref, w2_ref, bv_ref, b1_ref, b2_ref,
                  k_ref, s1_ref, tr_ref, tc_ref, s1t_ref, ur_ref, uc_ref, o_ref,
                  y16_scr, gate_scr, g_scr, yc_scr, h_scr):
    seq, cb = v_ref.shape
    n2 = seq // DFT_SLABS
    step = DFT_SLABS * DFT_J

    conv3 = lambda z_ref, w_ref, b_ref: _short_conv(z_ref[...].astype(F32), w_ref[...], b_ref[...])
    _store_block_order(y16_scr, conv3(v_ref, wv_ref, bv_ref), n2)
    _store_block_order(gate_scr.at[0], conv3(x1_ref, w1_ref, b1_ref), n2)
    _store_block_order(gate_scr.at[1], conv3(x2_ref, w2_ref, b2_ref), n2)

    for n in range(HY_ORDER):
        last = n + 1 == HY_ORDER
        _stage1_fwd(s1_ref, y16_scr, g_scr, n2)
        for g in range(DFT_GROUPS):
            x = _stage2_fwd(g, tr_ref, tc_ref, g_scr, n2).astype(BF16)
            xr, xi = x[:n2], x[n2:]
            base = 2 * n2 * g
            kr = k_ref[n, base:base + n2, :]
            ki = k_ref[n, base + n2:base + 2 * n2, :]
            yc_scr[base:base + n2, :] = xr * kr - xi * ki
            yc_scr[base + n2:base + 2 * n2, :] = xr * ki + xi * kr
        for g in range(DFT_GROUPS):
            ycat = yc_scr[2 * n2 * g:2 * n2 * (g + 1), :]
            if g in (0, 8):
                parts = [(0 if g == 0 else 1,
                          jnp.dot(ur_ref[0 if g == 0 else 1], ycat, preferred_element_type=F32))]
            else:
                h = jnp.dot(uc_ref[g - 1], ycat, preferred_element_type=F32)
                parts = [(2 * g, h[:n2]), (2 * g + 1, h[n2:])]
            for c, hp in parts:
                for k in range(n2 // DFT_J):
                    dst = DFT_COMPS * DFT_J * k + DFT_J * c
                    h_scr[dst:dst + DFT_J, :] = hp[DFT_J * k:DFT_J * (k + 1), :].astype(BF16)
        for k in range(n2 // DFT_J):
            rows = slice(step * k, step * (k + 1))
            conv = jnp.dot(s1t_ref[...], h_scr[DFT_COMPS * DFT_J * k:DFT_COMPS * DFT_J * (k + 1), :],
                           preferred_element_type=F32)
            y_new = (gate_scr[n, rows, :] * conv).astype(BF16)
            if last:
                for t1 in range(DFT_SLABS):
                    dst = n2 * t1 + DFT_J * k
                    o_ref[dst:dst + DFT_J, :] = y_new[DFT_J * t1:DFT_J * (t1 + 1), :]
            else:
                y16_scr[rows, :] = y_new


def _hyena_call(z, conv_w, conv_b, kspec, tables, bsz, seq, width, col0):
    cb = MXU_DIM
    ncb = width // cb
    base = col0 // cb
    zspec = lambda part: pl.BlockSpec((seq, cb), lambda k, b: (b, base + part * ncb + k))
    wspec = lambda part: pl.BlockSpec((HY_SHORT, cb), lambda k, b: (0, part * ncb + k))
    bspec = lambda part: pl.BlockSpec((1, cb), lambda k, b: (0, part * ncb + k))
    cb_all = conv_b.reshape(1, -1)
    const = lambda t: (lambda k, b: (0,) * t.ndim)
    return pl.pallas_call(
        _hyena_kernel,
        grid=(ncb, bsz),
        in_specs=[zspec(0), zspec(1), zspec(2), wspec(0), wspec(1), wspec(2),
                  bspec(0), bspec(1), bspec(2),
                  _resident((HY_ORDER, kspec.shape[1], cb), lambda k, b: (0, 0, k))]
                 + [_resident(t.shape, const(t)) for t in tables],
        out_specs=pl.BlockSpec((seq, cb), lambda k, b: (b, k)),
        out_shape=jax.ShapeDtypeStruct((bsz * seq, width), BF16),
        scratch_shapes=[pltpu.VMEM((seq, cb), BF16),
                        pltpu.VMEM((HY_ORDER, seq, cb), F32),
                        pltpu.VMEM((2 * seq, cb), BF16),
                        pltpu.VMEM((kspec.shape[1], cb), BF16),
                        pltpu.VMEM((2 * seq, cb), BF16)],
        compiler_params=_cparams(("arbitrary", "arbitrary")),
        name="hyena_mixer",
    )(z, z, z, conv_w, conv_w, conv_w, cb_all, cb_all, cb_all, kspec, *tables)


def _merge_kernel(u_ref, vn_ref, yh_ref, ga_ref, gb_ref, ws_ref, bs_ref, wa_ref, wb_ref, o_ref,
                  s_scr):
    tm = u_ref.shape[0]
    gd = u_ref.shape[1] // SGU_GROUPS
    for n in range(tm // SGU_CHUNK):
        rows = slice(n * SGU_CHUNK, (n + 1) * SGU_CHUNK)
        for g in range(SGU_GROUPS):
            cols = slice(g * gd, (g + 1) * gd)
            mixed = jnp.dot(ws_ref[g], vn_ref[rows, cols], preferred_element_type=F32)
            s_scr[rows, cols] = (u_ref[rows, cols].astype(F32)
                                 * (mixed + bs_ref[:, cols])).astype(BF16)
    y_a = jnp.dot(s_scr[...], wa_ref[...].astype(BF16), preferred_element_type=F32)
    y_b = jnp.dot(yh_ref[...], wb_ref[...].astype(BF16), preferred_element_type=F32)
    o_ref[...] = (ga_ref[...].astype(F32) * y_a + gb_ref[...].astype(F32) * y_b).astype(o_ref.dtype)


def _merge_call(z, yh, sgu_w16, sgu_bfull, w_a, w_b, d):
    t = z.shape[0]
    wdt = w_a.shape[0]
    tm = 512
    u_blk = 2 * d // wdt
    return pl.pallas_call(
        _merge_kernel,
        grid=(t // tm,),
        in_specs=[pl.BlockSpec((tm, wdt), lambda i: (i, u_blk)),
                  pl.BlockSpec((tm, wdt), lambda i: (i, u_blk + 1)),
                  pl.BlockSpec((tm, wdt), lambda i: (i, 0)),
                  pl.BlockSpec((tm, d), lambda i: (i, 0)),
                  pl.BlockSpec((tm, d), lambda i: (i, 1)),
                  _resident(sgu_w16.shape, lambda i: (0, 0, 0)),
                  _resident(sgu_bfull.shape, lambda i: (0, 0)),
                  _resident(w_a.shape, lambda i: (0, 0)),
                  _resident(w_b.shape, lambda i: (0, 0))],
        out_specs=pl.BlockSpec((tm, d), lambda i: (i, 0)),
        out_shape=jax.ShapeDtypeStruct((t, d), BF16),
        scratch_shapes=[pltpu.VMEM((tm, wdt), BF16)],
        compiler_params=_cparams(("parallel",)),
        name="gated_merge",
    )(z, z, yh, z, z, sgu_w16, sgu_bfull, w_a, w_b)


def _oproj_kernel(m_ref, x_ref, g1_ref, w_ref, b_ref, lg_ref, lb_ref, o_ref, *, alpha, rc):
    for rows in _row_chunks(m_ref.shape[0], rc):
        mix = jnp.dot(m_ref[rows, :], w_ref[...].astype(BF16),
                      preferred_element_type=F32) + b_ref[...]
        o_ref[rows, :] = _layer_norm(alpha * x_ref[rows, :] + g1_ref[0] * mix,
                                     lg_ref[...], lb_ref[...])


def _oproj_call(m, xt, mod3, w_o, b_o, ln_g, ln_b, seq, alpha):
    t, d = xt.shape
    tm = 512
    per_b = seq // tm
    row = lambda i: (0, 0)
    return pl.pallas_call(
        functools.partial(_oproj_kernel, alpha=alpha, rc=256),
        grid=(t // tm,),
        in_specs=[pl.BlockSpec((tm, d), lambda i: (i, 0)),
                  pl.BlockSpec((tm, d), lambda i: (i, 0)),
                  pl.BlockSpec((1, 1, d), lambda i: (i // per_b, 0, 2)),
                  _resident((d, d), row),
                  pl.BlockSpec((1, d), row), pl.BlockSpec((1, d), row), pl.BlockSpec((1, d), row)],
        out_specs=pl.BlockSpec((tm, d), lambda i: (i, 0)),
        out_shape=jax.ShapeDtypeStruct((t, d), F32),
        compiler_params=_cparams(("parallel",)),
        name="out_proj_ln",
    )(m, xt, mod3, w_o, b_o.reshape(1, d), ln_g.reshape(1, d), ln_b.reshape(1, d))


def _mlp_kernel(x_ref, sc_ref, sh_ref, g2_ref, w1_ref, b1_ref, w2_ref, b2_ref, lg_ref, lb_ref,
                o_ref, h_scr, *, alpha, rc):
    acc_scr = o_ref
    j = pl.program_id(1)
    last = pl.num_programs(1) - 1
    chunks = _row_chunks(x_ref.shape[0], rc)

    def partial_out(rows):
        hid = jnp.dot(h_scr[rows, :], w1_ref[...], preferred_element_type=F32) + b1_ref[...]
        hid = jnp.square(jnp.maximum(hid, 0.0)).astype(BF16)
        return jnp.dot(hid, w2_ref[...], preferred_element_type=F32)

    @pl.when(j == 0)
    def _():
        for rows in chunks:
            h_scr[rows, :] = (x_ref[rows, :] * (1.0 + sc_ref[0]) + sh_ref[0]).astype(BF16)
            acc_scr[rows, :] = partial_out(rows)

    @pl.when(jnp.logical_and(j > 0, j < last))
    def _():
        acc_scr[...] += partial_out(slice(None))

    @pl.when(j == last)
    def _():
        for rows in chunks:
            f = acc_scr[rows, :] + partial_out(rows) + b2_ref[...]
            o_ref[rows, :] = _layer_norm(alpha * x_ref[rows, :] + g2_ref[0] * f,
                                         lg_ref[...], lb_ref[...])


def _mlp_call(x1, mod3, w_m1, b_m1, w_m2, b_m2, ln_g, ln_b, seq, alpha):
    t, d = x1.shape
    dff = w_m1.shape[1]
    tm = 1024
    tf = 512
    per_b = seq // tm
    row = lambda i, j: (0, 0)
    return pl.pallas_call(
        functools.partial(_mlp_kernel, alpha=alpha, rc=256),
        grid=(t // tm, dff // tf),
        in_specs=[pl.BlockSpec((tm, d), lambda i, j: (i, 0)),
                  pl.BlockSpec((1, 1, d), lambda i, j: (i // per_b, 0, 4)),
                  pl.BlockSpec((1, 1, d), lambda i, j: (i // per_b, 0, 3)),
                  pl.BlockSpec((1, 1, d), lambda i, j: (i // per_b, 0, 5)),
                  pl.BlockSpec((d, tf), lambda i, j: (0, j)),
                  pl.BlockSpec((1, tf), lambda i, j: (0, j)),
                  pl.BlockSpec((tf, d), lambda i, j: (j, 0)),
                  pl.BlockSpec((1, d), row), pl.BlockSpec((1, d), row), pl.BlockSpec((1, d), row)],
        out_specs=pl.BlockSpec((tm, d), lambda i, j: (i, 0)),
        out_shape=jax.ShapeDtypeStruct((t, d), F32),
        scratch_shapes=[pltpu.VMEM((tm, d), BF16)],
        compiler_params=_cparams(("parallel", "arbitrary")),
        name="mlp_ln",
    )(x1, mod3, mod3, mod3, w_m1, b_m1.reshape(1, dff), w_m2, b_m2.reshape(1, d),
      ln_g.reshape(1, d), ln_b.reshape(1, d))


def _filter_features(seq):
    bands_n = (HY_EMB - 1) // 2
    t = jnp.linspace(0.0, 1.0, seq, dtype=F32)[:, None]
    omega = 2.0 * math.pi * jnp.arange(seq, dtype=F32)[:, None] / seq
    bands = jnp.linspace(1e-4, bands_n - 1, bands_n, dtype=F32)[None, :]
    feats = jnp.concatenate([t, jnp.cos(bands * omega), -jnp.sin(bands * omega)], axis=-1)
    return t, feats


def _decay_rates(width):
    min_decay = math.log(HY_DECAY_TARGET) / HY_SLOW_DECAY
    max_decay = math.log(HY_DECAY_TARGET) / HY_FAST_DECAY
    return jnp.abs(jnp.linspace(min_decay, max_decay, width, dtype=F32))[None, :]


def kernel(x, c, w_ada, b_ada, w_in, b_in, sgu_ln_g, sgu_ln_b, sgu_w, sgu_b, hy_conv_w, hy_conv_b, hy_w1, hy_b1, hy_w2, hy_b2, hy_freq, hy_w3, hy_skip, w_branch_a, w_branch_b, w_o, b_o, ln1_g, ln1_b, w_m1, b_m1, w_m2, b_m2, ln2_g, ln2_b):
    bsz, seq, d = x.shape
    depth = w_ada.shape[0]
    alpha = (2.0 * depth) ** 0.25
    sgu_width = w_branch_a.shape[1]
    hy_width = w_branch_b.shape[1]
    assert seq % SGU_CHUNK == 0 and sgu_w.shape[-1] == SGU_CHUNK

    assert seq % (8 * MXU_DIM) == 0
    tables = _dft_stage_tables(seq)
    tcol, feats = _filter_features(seq)
    deltas = _decay_rates(hy_width)

    xt = x.reshape(bsz * seq, d)
    for l in range(depth):
        mod3 = _mod_call(c, w_ada[l], b_ada[l]).reshape(bsz, 1, N_MOD * d)

        z = _inproj_call(xt, mod3, w_in[l], b_in[l], sgu_ln_g[l], sgu_ln_b[l],
                         seq, sgu_width, hy_width)

        h2 = _filter_mlp_call(feats, hy_w1[l], hy_b1[l], hy_w2[l], hy_b2[l], hy_freq[l])
        kspec = _filter_spec_call(h2, hy_w3[l], tcol, deltas, hy_skip[l], *tables[:3], hy_width)
        yh = _hyena_call(z, hy_conv_w[l], hy_conv_b[l], kspec, tables,
                         bsz, seq, hy_width, 2 * d + 2 * sgu_width)

        gd = sgu_width // SGU_GROUPS
        bs_full = jnp.repeat(sgu_b[l].T, gd, axis=1)
        m = _merge_call(z, yh, sgu_w[l].astype(BF16), bs_full, w_branch_a[l], w_branch_b[l], d)

        x1 = _oproj_call(m, xt, mod3, w_o[l], b_o[l], ln1_g[l], ln1_b[l], seq, alpha)
        xt = _mlp_call(x1, mod3, w_m1[l].astype(BF16), b_m1[l], w_m2[l].astype(BF16), b_m2[l],
                       ln2_g[l], ln2_b[l], seq, alpha)
    return xt.reshape(bsz, seq, d)
```

```python
import functools
import math

import jax
import jax.numpy as jnp
from jax import lax
from jax.experimental import pallas as pl
from jax.experimental.pallas import tpu as pltpu

F32 = jnp.float32
BF16 = jnp.bfloat16

SGU_CHUNK = 128
SGU_GROUPS = 8
HY_ORDER = 2
HY_SHORT = 3
HY_EMB = 33
HY_DECAY_TARGET = 1e-2
HY_FAST_DECAY = 0.3
HY_SLOW_DECAY = 1.5
N_MOD = 6
LN_EPS = 1e-5

LANE = 128
SUBLANES = 8
MXU_DIM = 256
VMEM_LIMIT = 48 * 1024 * 1024


def _cparams(sem):
    return pltpu.CompilerParams(dimension_semantics=sem, vmem_limit_bytes=VMEM_LIMIT)


def _resident(block_shape, index_map):
    return pl.BlockSpec(block_shape, index_map, pipeline_mode=pl.Buffered(1))


def _row_chunks(total, size):
    return [slice(r, r + size) for r in range(0, total, size)]


def _gelu(x):
    return 0.5 * x * (1.0 + lax.erf(x * (1.0 / math.sqrt(2.0))))


def _layer_norm(x, g, b):
    mu = jnp.mean(x, axis=-1, keepdims=True)
    xc = x - mu
    var = jnp.mean(xc * xc, axis=-1, keepdims=True)
    return xc * lax.rsqrt(var + LN_EPS) * g + b


def _dot_bf16x3(a, b):
    m = a.shape[0]
    a_hi = a.astype(BF16)
    b_hi = b.astype(BF16)
    a_hi32 = a_hi.astype(F32)
    b_lo = (b - b_hi.astype(F32)).astype(BF16)
    stacked = jnp.concatenate([a_hi32, a - a_hi32], axis=0).astype(BF16)
    top = jnp.dot(stacked, b_hi, preferred_element_type=F32)
    return top[:m] + top[m:] + jnp.dot(a_hi, b_lo, preferred_element_type=F32)


def _mod_kernel(c_ref, w_ref, b_ref, o_ref):
    c = c_ref[...]
    cond = c * jax.nn.sigmoid(c)
    o_ref[...] = _dot_bf16x3(cond, w_ref[...]) + b_ref[...]


def _mod_call(c, w_ada, b_ada):
    bsz, d = c.shape
    n = w_ada.shape[1]
    tn = 1024
    return pl.pallas_call(
        _mod_kernel,
        grid=(n // tn,),
        in_specs=[pl.BlockSpec((bsz, d), lambda j: (0, 0)),
                  pl.BlockSpec((d, tn), lambda j: (0, j)),
                  pl.BlockSpec((1, tn), lambda j: (0, j))],
        out_specs=pl.BlockSpec((bsz, tn), lambda j: (0, j)),
        out_shape=jax.ShapeDtypeStruct((bsz, n), F32),
        compiler_params=_cparams(("arbitrary",)),
        name="adaln_mod",
    )(c, w_ada, b_ada.reshape(1, n))


def _inproj_kernel(x_ref, sc_ref, sh_ref, wa_ref, wb_ref, ba_ref, bb_ref, g_ref, be_ref,
                   o_ref, h_scr, *, kinds, rc):
    j = pl.program_id(1)
    tn = wa_ref.shape[1]
    chunks = _row_chunks(x_ref.shape[0], rc)
    acts = {"sigmoid": jax.nn.sigmoid, "gelu": _gelu, "raw": lambda a: a,
            "gelu_ln": lambda a: _layer_norm(_gelu(a), g_ref[...], be_ref[...])}

    def emit(step):
        halves = ((wa_ref, ba_ref, slice(0, tn), kinds[2 * step]),
                  (wb_ref, bb_ref, slice(tn, 2 * tn), kinds[2 * step + 1]))
        for rows in chunks:
            if step == 0:
                h_scr[rows, :] = (x_ref[rows, :] * (1.0 + sc_ref[0]) + sh_ref[0]).astype(BF16)
            for w_ref, b_ref, cols, kind in halves:
                if kind == "pad":
                    o_ref[rows, cols] = jnp.zeros((rc, tn), o_ref.dtype)
                else:
                    acc = jnp.dot(h_scr[rows, :], w_ref[...], preferred_element_type=F32) + b_ref[...]
                    o_ref[rows, cols] = acts[kind](acc).astype(o_ref.dtype)

    for step in range(len(kinds) // 2):
        pl.when(j == step)(functools.partial(emit, step))


def _inproj_call(xt, mod3, w_in, b_in, ln_g, ln_b, seq, sgu_w, hy_w):
    t, d = xt.shape
    n = w_in.shape[1]
    tm = 1024
    tn = sgu_w
    assert hy_w == sgu_w and seq % tm == 0 and d % tn == 0
    nblk = n // tn
    n_gate = 2 * d // tn
    kinds = ["sigmoid"] * n_gate + ["gelu", "gelu_ln"] + ["raw"] * (HY_ORDER + 1)
    kinds += ["pad"] * (len(kinds) % 2)
    assert len(kinds) - nblk in (0, 1)
    per_b = seq // tm
    wcol = lambda c: (0, (jnp.minimum(c, nblk - 1) + (nblk - n_gate)) % nblk)
    col_a = lambda i, j: wcol(2 * j)
    col_b = lambda i, j: wcol(2 * j + 1)
    return pl.pallas_call(
        functools.partial(_inproj_kernel, kinds=tuple(kinds), rc=256),
        grid=(t // tm, len(kinds) // 2),
        in_specs=[pl.BlockSpec((tm, d), lambda i, j: (i, 0)),
                  pl.BlockSpec((1, 1, d), lambda i, j: (i // per_b, 0, 1)),
                  pl.BlockSpec((1, 1, d), lambda i, j: (i // per_b, 0, 0)),
                  pl.BlockSpec((d, tn), col_a), pl.BlockSpec((d, tn), col_b),
                  pl.BlockSpec((1, tn), col_a), pl.BlockSpec((1, tn), col_b),
                  pl.BlockSpec((1, tn), lambda i, j: (0, 0)),
                  pl.BlockSpec((1, tn), lambda i, j: (0, 0))],
        out_specs=pl.BlockSpec((tm, 2 * tn), lambda i, j: (i, j)),
        out_shape=jax.ShapeDtypeStruct((t, len(kinds) * tn), BF16),
        scratch_shapes=[pltpu.VMEM((tm, d), BF16)],
        compiler_params=_cparams(("parallel", "arbitrary")),
        name="in_proj",
    )(xt, mod3, mod3, w_in, w_in, b_in.reshape(1, n), b_in.reshape(1, n),
      ln_g.reshape(1, tn), ln_b.reshape(1, tn))


def _filter_mlp_kernel(f_ref, w1_ref, b1_ref, w2_ref, b2_ref, fr_ref, o_ref):
    hp = lax.Precision.HIGHEST
    fr = fr_ref[...]
    h = jnp.sin(fr * (jnp.dot(f_ref[...], w1_ref[...], preferred_element_type=F32, precision=hp)
                      + b1_ref[...]))
    o_ref[...] = jnp.sin(fr * (jnp.dot(h, w2_ref[...], preferred_element_type=F32, precision=hp)
                               + b2_ref[...]))


def _filter_mlp_call(feats, w1, b1, w2, b2, freq):
    seq = feats.shape[0]
    hid = w1.shape[1]
    return pl.pallas_call(
        _filter_mlp_kernel,
        out_shape=jax.ShapeDtypeStruct((seq, hid), F32),
        name="hyena_filter_mlp",
    )(feats, w1, b1.reshape(1, hid), w2, b2.reshape(1, hid), freq.reshape(1, hid))


DFT_GROUPS = 9
DFT_SLABS = 8
DFT_COMPS = 16
DFT_J = 32


def _packed_start(g, n2):
    return {0: 0, 8: n2}.get(g, 2 * n2 * g)


def _store_slab_block_order(dst_ref, val, t1, n2):
    step = DFT_SLABS * DFT_J
    for k in range(n2 // DFT_J):
        dst = step * k + DFT_J * t1
        dst_ref[dst:dst + DFT_J, :] = val[DFT_J * k:DFT_J * (k + 1), :].astype(dst_ref.dtype)


def _store_block_order(dst_ref, val, n2):
    for t1 in range(DFT_SLABS):
        _store_slab_block_order(dst_ref, val[n2 * t1:n2 * (t1 + 1), :], t1, n2)


def _stage1_fwd(s1_ref, src_ref, dst_ref, n2):
    step = DFT_SLABS * DFT_J
    for k in range(n2 // DFT_J):
        comps = jnp.dot(s1_ref[...], src_ref[step * k:step * (k + 1), :], preferred_element_type=F32)
        for c in range(DFT_COMPS):
            dst_ref[n2 * c + DFT_J * k:n2 * c + DFT_J * (k + 1), :] = (
                comps[DFT_J * c:DFT_J * (c + 1), :].astype(dst_ref.dtype))


def _stage2_fwd(g, tr_ref, tc_ref, src_ref, n2):
    start = _packed_start(g, n2)
    if g in (0, 8):
        return jnp.dot(tr_ref[0 if g == 0 else 1], src_ref[start:start + n2, :],
                       preferred_element_type=F32)
    return jnp.dot(tc_ref[g - 1], src_ref[start:start + 2 * n2, :], preferred_element_type=F32)


def _dft_stage_tables(seq):
    n = 2 * seq
    n2 = n // 16
    f2 = jnp.arange(n2, dtype=jnp.int32)[:, None]
    t2 = jnp.arange(n2, dtype=jnp.int32)[None, :]

    def cs(g):
        ang = ((t2 * (g + 16 * f2)) & (n - 1)).astype(F32) * (2.0 * math.pi / n)
        return jnp.cos(ang), jnp.sin(ang)

    real = jnp.stack([jnp.concatenate([c, -s], axis=0) for c, s in (cs(0), cs(8))])
    cplx = jnp.stack([jnp.block([[c, s], [-s, c]]) for c, s in (cs(g) for g in range(1, 8))])

    t1 = jnp.arange(DFT_SLABS, dtype=jnp.int32)
    rows = [jnp.ones((DFT_SLABS,), F32), jnp.where(t1 % 2 == 0, 1.0, -1.0).astype(F32)]
    for g in range(1, 8):
        ang = ((g * t1) % 16).astype(F32) * (math.pi / 8.0)
        rows += [jnp.cos(ang), -jnp.sin(ang)]
    s1 = jnp.kron(jnp.stack(rows), jnp.eye(DFT_J, dtype=F32))
    return tuple(t.astype(BF16) for t in (s1, real, cplx, s1.T, jnp.swapaxes(real, 1, 2),
                                          jnp.swapaxes(cplx, 1, 2)))


def _filter_spec_kernel(h_ref, wf_ref, wb_ref, t_ref, dl_ref, skip_ref, s1_ref, tr_ref, tc_ref,
                        k_ref, e_scr, o_scr, ge_scr, go_scr):
    seq = h_ref.shape[0]
    n2 = seq // 8
    window = jnp.exp(-t_ref[...] * dl_ref[...])
    _store_block_order(e_scr, _dot_bf16x3(h_ref[...], wf_ref[...] + wb_ref[...]) * window, n2)
    _store_block_order(o_scr, _dot_bf16x3(h_ref[...], wf_ref[...] - wb_ref[...]) * window, n2)
    _stage1_fwd(s1_ref, e_scr, ge_scr, n2)
    _stage1_fwd(s1_ref, o_scr, go_scr, n2)
    for g in range(DFT_GROUPS):
        wgt = (1.0 if g in (0, 8) else 2.0) / (2 * seq)
        rows = slice(2 * n2 * g, 2 * n2 * g + n2)
        rows_im = slice(2 * n2 * g + n2, 2 * n2 * (g + 1))
        k_ref[0, rows, :] = ((_stage2_fwd(g, tr_ref, tc_ref, ge_scr, n2)[:n2] + skip_ref[0])
                             * wgt).astype(k_ref.dtype)
        k_ref[0, rows_im, :] = (_stage2_fwd(g, tr_ref, tc_ref, go_scr, n2)[n2:]
                                * wgt).astype(k_ref.dtype)


def _filter_spec_call(h2, w3, tcol, deltas, skip, t_s1, t_real, t_cplx, width):
    seq, hid = h2.shape
    n2 = seq // 8
    cb = MXU_DIM
    ncb = width // cb
    per_dir = HY_ORDER * ncb
    rows = DFT_GROUPS * 2 * n2
    const = lambda n, k: (0, 0, 0)
    return pl.pallas_call(
        _filter_spec_kernel,
        grid=(HY_ORDER, ncb),
        in_specs=[_resident((seq, hid), lambda n, k: (0, 0)),
                  pl.BlockSpec((hid, cb), lambda n, k: (0, n * ncb + k)),
                  pl.BlockSpec((hid, cb), lambda n, k: (0, per_dir + n * ncb + k)),
                  _resident((seq, 1), lambda n, k: (0, 0)),
                  pl.BlockSpec((1, cb), lambda n, k: (0, k)),
                  pl.BlockSpec((1, 1, cb), lambda n, k: (n, 0, k)),
                  _resident(t_s1.shape, lambda n, k: (0, 0)),
                  _resident(t_real.shape, const), _resident(t_cplx.shape, const)],
        out_specs=pl.BlockSpec((1, rows, cb), lambda n, k: (n, 0, k)),
        out_shape=jax.ShapeDtypeStruct((HY_ORDER, rows, width), BF16),
        scratch_shapes=[pltpu.VMEM((seq, cb), BF16), pltpu.VMEM((seq, cb), BF16),
                        pltpu.VMEM((2 * seq, cb), BF16), pltpu.VMEM((2 * seq, cb), BF16)],
        compiler_params=_cparams(("arbitrary", "arbitrary")),
        name="hyena_filter_spectrum",
    )(h2, w3, w3, tcol, deltas, skip.reshape(HY_ORDER, 1, width), t_s1, t_real, t_cplx)


def _short_conv(z, w, b):
    seq = z.shape[0]
    prev = pltpu.roll(z, 1, 0)
    nxt = pltpu.roll(z, seq - 1, 0)
    row = lax.broadcasted_iota(jnp.int32, (SUBLANES, z.shape[1]), 0)
    prev = jnp.concatenate([jnp.where(row == 0, 0.0, prev[:SUBLANES]), prev[SUBLANES:]], axis=0)
    nxt = jnp.concatenate([nxt[:-SUBLANES], jnp.where(row == SUBLANES - 1, 0.0, nxt[-SUBLANES:])],
                          axis=0)
    return w[0:1] * prev + w[1:2] * z + w[2:3] * nxt + b


def _hyena_kernel(v_ref, x1_ref, x2_ref, wv_ref, w1_ref, w2_ref, bv_ref, b1_ref, b2_ref,
                  k_ref, s1_ref, tr_ref, tc_ref, s1t_ref, ur_ref, uc_ref, o_ref,
                  y16_scr, gate_scr, g_scr, yc_scr, h_scr):
    seq, cb = v_ref.shape
    n2 = seq // DFT_SLABS
    step = DFT_SLABS * DFT_J

    conv3 = lambda z_ref, w_ref, b_ref: _short_conv(z_ref[...].astype(F32), w_ref[...], b_ref[...])
    _store_block_order(y16_scr, conv3(v_ref, wv_ref, bv_ref), n2)
    _store_block_order(gate_scr.at[0], conv3(x1_ref, w1_ref, b1_ref), n2)
    _store_block_order(gate_scr.at[1], conv3(x2_ref, w2_ref, b2_ref), n2)

    for n in range(HY_ORDER):
        last = n + 1 == HY_ORDER
        _stage1_fwd(s1_ref, y16_scr, g_scr, n2)
        for g in range(DFT_GROUPS):
            x = _stage2_fwd(g, tr_ref, tc_ref, g_scr, n2).astype(BF16)
            xr, xi = x[:n2], x[n2:]
            base = 2 * n2 * g
            kr = k_ref[n, base:base + n2, :]
            ki = k_ref[n, base + n2:base + 2 * n2, :]
            yc_scr[base:base + n2, :] = xr * kr - xi * ki
            yc_scr[base + n2:base + 2 * n2, :] = xr * ki + xi * kr
        for g in range(DFT_GROUPS):
            ycat = yc_scr[2 * n2 * g:2 * n2 * (g + 1), :]
            if g in (0, 8):
                parts = [(0 if g == 0 else 1,
                          jnp.dot(ur_ref[0 if g == 0 else 1], ycat, preferred_element_type=F32))]
            else:
                h = jnp.dot(uc_ref[g - 1], ycat, preferred_element_type=F32)
                parts = [(2 * g, h[:n2]), (2 * g + 1, h[n2:])]
            for c, hp in parts:
                for k in range(n2 // DFT_J):
                    dst = DFT_COMPS * DFT_J * k + DFT_J * c
                    h_scr[dst:dst + DFT_J, :] = hp[DFT_J * k:DFT_J * (k + 1), :].astype(BF16)
        for k in range(n2 // DFT_J):
            rows = slice(step * k, step * (k + 1))
            conv = jnp.dot(s1t_ref[...], h_scr[DFT_COMPS * DFT_J * k:DFT_COMPS * DFT_J * (k + 1), :],
                           preferred_element_type=F32)
            y_new = (gate_scr[n, rows, :] * conv).astype(BF16)
            if last:
                for t1 in range(DFT_SLABS):
                    dst = n2 * t1 + DFT_J * k
                    o_ref[dst:dst + DFT_J, :] = y_new[DFT_J * t1:DFT_J * (t1 + 1), :]
            else:
                y16_scr[rows, :] = y_new


def _hyena_call(z, conv_w, conv_b, kspec, tables, bsz, seq, width, col0):
    cb = MXU_DIM
    ncb = width // cb
    base = col0 // cb
    zspec = lambda part: pl.BlockSpec((seq, cb), lambda k, b: (b, base + part * ncb + k))
    wspec = lambda part: pl.BlockSpec((HY_SHORT, cb), lambda k, b: (0, part * ncb + k))
    bspec = lambda part: pl.BlockSpec((1, cb), lambda k, b: (0, part * ncb + k))
    cb_all = conv_b.reshape(1, -1)
    const = lambda t: (lambda k, b: (0,) * t.ndim)
    return pl.pallas_call(
        _hyena_kernel,
        grid=(ncb, bsz),
        in_specs=[zspec(0), zspec(1), zspec(2), wspec(0), wspec(1), wspec(2),
                  bspec(0), bspec(1), bspec(2),
                  _resident((HY_ORDER, kspec.shape[1], cb), lambda k, b: (0, 0, k))]
                 + [_resident(t.shape, const(t)) for t in tables],
        out_specs=pl.BlockSpec((seq, cb), lambda k, b: (b, k)),
        out_shape=jax.ShapeDtypeStruct((bsz * seq, width), BF16),
        scratch_shapes=[pltpu.VMEM((seq, cb), BF16),
                        pltpu.VMEM((HY_ORDER, seq, cb), F32),
                        pltpu.VMEM((2 * seq, cb), BF16),
                        pltpu.VMEM((kspec.shape[1], cb), BF16),
                        pltpu.VMEM((2 * seq, cb), BF16)],
        compiler_params=_cparams(("arbitrary", "arbitrary")),
        name="hyena_mixer",
    )(z, z, z, conv_w, conv_w, conv_w, cb_all, cb_all, cb_all, kspec, *tables)


def _merge_kernel(u_ref, vn_ref, yh_ref, ga_ref, gb_ref, ws_ref, bs_ref, wa_ref, wb_ref, o_ref,
                  s_scr):
    tm = u_ref.shape[0]
    gd = u_ref.shape[1] // SGU_GROUPS
    for n in range(tm // SGU_CHUNK):
        rows = slice(n * SGU_CHUNK, (n + 1) * SGU_CHUNK)
        for g in range(SGU_GROUPS):
            cols = slice(g * gd, (g + 1) * gd)
            mixed = jnp.dot(ws_ref[g], vn_ref[rows, cols], preferred_element_type=F32)
            s_scr[rows, cols] = (u_ref[rows, cols].astype(F32)
                                 * (mixed + bs_ref[:, cols])).astype(BF16)
    y_a = jnp.dot(s_scr[...], wa_ref[...].astype(BF16), preferred_element_type=F32)
    y_b = jnp.dot(yh_ref[...], wb_ref[...].astype(BF16), preferred_element_type=F32)
    o_ref[...] = (ga_ref[...].astype(F32) * y_a + gb_ref[...].astype(F32) * y_b).astype(o_ref.dtype)


def _merge_call(z, yh, sgu_w16, sgu_bfull, w_a, w_b, d):
    t = z.shape[0]
    wdt = w_a.shape[0]
    tm = 512
    u_blk = 2 * d // wdt
    return pl.pallas_call(
        _merge_kernel,
        grid=(t // tm,),
        in_specs=[pl.BlockSpec((tm, wdt), lambda i: (i, u_blk)),
                  pl.BlockSpec((tm, wdt), lambda i: (i, u_blk + 1)),
                  pl.BlockSpec((tm, wdt), lambda i: (i, 0)),
                  pl.BlockSpec((tm, d), lambda i: (i, 0)),
                  pl.BlockSpec((tm, d), lambda i: (i, 1)),
                  _resident(sgu_w16.shape, lambda i: (0, 0, 0)),
                  _resident(sgu_bfull.shape, lambda i: (0, 0)),
                  _resident(w_a.shape, lambda i: (0, 0)),
                  _resident(w_b.shape, lambda i: (0, 0))],
        out_specs=pl.BlockSpec((tm, d), lambda i: (i, 0)),
        out_shape=jax.ShapeDtypeStruct((t, d), BF16),
        scratch_shapes=[pltpu.VMEM((tm, wdt), BF16)],
        compiler_params=_cparams(("parallel",)),
        name="gated_merge",
    )(z, z, yh, z, z, sgu_w16, sgu_bfull, w_a, w_b)


def _oproj_kernel(m_ref, x_ref, g1_ref, w_ref, b_ref, lg_ref, lb_ref, o_ref, *, alpha, rc):
    for rows in _row_chunks(m_ref.shape[0], rc):
        mix = jnp.dot(m_ref[rows, :], w_ref[...].astype(BF16),
                      preferred_element_type=F32) + b_ref[...]
        o_ref[rows, :] = _layer_norm(alpha * x_ref[rows, :] + g1_ref[0] * mix,
                                     lg_ref[...], lb_ref[...])


def _oproj_call(m, xt, mod3, w_o, b_o, ln_g, ln_b, seq, alpha):
    t, d = xt.shape
    tm = 512
    per_b = seq // tm
    row = lambda i: (0, 0)
    return pl.pallas_call(
        functools.partial(_oproj_kernel, alpha=alpha, rc=256),
        grid=(t // tm,),
        in_specs=[pl.BlockSpec((tm, d), lambda i: (i, 0)),
                  pl.BlockSpec((tm, d), lambda i: (i, 0)),
                  pl.BlockSpec((1, 1, d), lambda i: (i // per_b, 0, 2)),
                  _resident((d, d), row),
                  pl.BlockSpec((1, d), row), pl.BlockSpec((1, d), row), pl.BlockSpec((1, d), row)],
        out_specs=pl.BlockSpec((tm, d), lambda i: (i, 0)),
        out_shape=jax.ShapeDtypeStruct((t, d), F32),
        compiler_params=_cparams(("parallel",)),
        name="out_proj_ln",
    )(m, xt, mod3, w_o, b_o.reshape(1, d), ln_g.reshape(1, d), ln_b.reshape(1, d))


def _mlp_kernel(x_ref, sc_ref, sh_ref, g2_ref, w1_ref, b1_ref, w2_ref, b2_ref, lg_ref, lb_ref,
                o_ref, h_scr, *, alpha, rc):
    acc_scr = o_ref
    j = pl.program_id(1)
    last = pl.num_programs(1) - 1
    chunks = _row_chunks(x_ref.shape[0], rc)

    def partial_out(rows):
        hid = jnp.dot(h_scr[rows, :], w1_ref[...], preferred_element_type=F32) + b1_ref[...]
        hid = jnp.square(jnp.maximum(hid, 0.0)).astype(BF16)
        return jnp.dot(hid, w2_ref[...], preferred_element_type=F32)

    @pl.when(j == 0)
    def _():
        for rows in chunks:
            h_scr[rows, :] = (x_ref[rows, :] * (1.0 + sc_ref[0]) + sh_ref[0]).astype(BF16)
            acc_scr[rows, :] = partial_out(rows)

    @pl.when(jnp.logical_and(j > 0, j < last))
    def _():
        acc_scr[...] += partial_out(slice(None))

    @pl.when(j == last)
    def _():
        for rows in chunks:
            f = acc_scr[rows, :] + partial_out(rows) + b2_ref[...]
            o_ref[rows, :] = _layer_norm(alpha * x_ref[rows, :] + g2_ref[0] * f,
                                         lg_ref[...], lb_ref[...])


def _mlp_call(x1, mod3, w_m1, b_m1, w_m2, b_m2, ln_g, ln_b, seq, alpha):
    t, d = x1.shape
    dff = w_m1.shape[1]
    tm = 512
    tf = 1024
    per_b = seq // tm
    row = lambda i, j: (0, 0)
    return pl.pallas_call(
        functools.partial(_mlp_kernel, alpha=alpha, rc=256),
        grid=(t // tm, dff // tf),
        in_specs=[pl.BlockSpec((tm, d), lambda i, j: (i, 0)),
                  pl.BlockSpec((1, 1, d), lambda i, j: (i // per_b, 0, 4)),
                  pl.BlockSpec((1, 1, d), lambda i, j: (i // per_b, 0, 3)),
                  pl.BlockSpec((1, 1, d), lambda i, j: (i // per_b, 0, 5)),
                  pl.BlockSpec((d, tf), lambda i, j: (0, j)),
                  pl.BlockSpec((1, tf), lambda i, j: (0, j)),
                  pl.BlockSpec((tf, d), lambda i, j: (j, 0)),
                  pl.BlockSpec((1, d), row), pl.BlockSpec((1, d), row), pl.BlockSpec((1, d), row)],
        out_specs=pl.BlockSpec((tm, d), lambda i, j: (i, 0)),
        out_shape=jax.ShapeDtypeStruct((t, d), F32),
        scratch_shapes=[pltpu.VMEM((tm, d), BF16)],
        compiler_params=_cparams(("parallel", "arbitrary")),
        name="mlp_ln",
    )(x1, mod3, mod3, mod3, w_m1, b_m1.reshape(1, dff), w_m2, b_m2.reshape(1, d),
      ln_g.reshape(1, d), ln_b.reshape(1, d))


def _filter_features(seq):
    bands_n = (HY_EMB - 1) // 2
    t = jnp.linspace(0.0, 1.0, seq, dtype=F32)[:, None]
    omega = 2.0 * math.pi * jnp.arange(seq, dtype=F32)[:, None] / seq
    bands = jnp.linspace(1e-4, bands_n - 1, bands_n, dtype=F32)[None, :]
    feats = jnp.concatenate([t, jnp.cos(bands * omega), -jnp.sin(bands * omega)], axis=-1)
    return t, feats


def _decay_rates(width):
    min_decay = math.log(HY_DECAY_TARGET) / HY_SLOW_DECAY
    max_decay = math.log(HY_DECAY_TARGET) / HY_FAST_DECAY
    return jnp.abs(jnp.linspace(min_decay, max_decay, width, dtype=F32))[None, :]


def kernel(x, c, w_ada, b_ada, w_in, b_in, sgu_ln_g, sgu_ln_b, sgu_w, sgu_b, hy_conv_w, hy_conv_b, hy_w1, hy_b1, hy_w2, hy_b2, hy_freq, hy_w3, hy_skip, w_branch_a, w_branch_b, w_o, b_o, ln1_g, ln1_b, w_m1, b_m1, w_m2, b_m2, ln2_g, ln2_b):
    bsz, seq, d = x.shape
    depth = w_ada.shape[0]
    alpha = (2.0 * depth) ** 0.25
    sgu_width = w_branch_a.shape[1]
    hy_width = w_branch_b.shape[1]
    assert seq % SGU_CHUNK == 0 and sgu_w.shape[-1] == SGU_CHUNK

    assert seq % (8 * MXU_DIM) == 0
    tables = _dft_stage_tables(seq)
    tcol, feats = _filter_features(seq)
    deltas = _decay_rates(hy_width)

    xt = x.reshape(bsz * seq, d)
    for l in range(depth):
        mod3 = _mod_call(c, w_ada[l], b_ada[l]).reshape(bsz, 1, N_MOD * d)

        z = _inproj_call(xt, mod3, w_in[l].astype(BF16), b_in[l], sgu_ln_g[l], sgu_ln_b[l],
                         seq, sgu_width, hy_width)

        h2 = _filter_mlp_call(feats, hy_w1[l], hy_b1[l], hy_w2[l], hy_b2[l], hy_freq[l])
        kspec = _filter_spec_call(h2, hy_w3[l], tcol, deltas, hy_skip[l], *tables[:3], hy_width)
        yh = _hyena_call(z, hy_conv_w[l], hy_conv_b[l], kspec, tables,
                         bsz, seq, hy_width, 2 * d + 2 * sgu_width)

        gd = sgu_width // SGU_GROUPS
        bs_full = jnp.repeat(sgu_b[l].T, gd, axis=1)
        m = _merge_call(z, yh, sgu_w[l].astype(BF16), bs_full, w_branch_a[l], w_branch_b[l], d)

        x1 = _oproj_call(m, xt, mod3, w_o[l], b_o[l], ln1_g[l], ln1_b[l], seq, alpha)
        xt = _mlp_call(x1, mod3, w_m1[l].astype(BF16), b_m1[l], w_m2[l].astype(BF16), b_m2[l],
                       ln2_g[l], ln2_b[l], seq, alpha)
    return xt.reshape(bsz, seq, d)
```

```python
import functools
import math

import jax
import jax.numpy as jnp
from jax import lax
from jax.experimental import pallas as pl
from jax.experimental.pallas import tpu as pltpu

F32 = jnp.float32
BF16 = jnp.bfloat16

SGU_CHUNK = 128
SGU_GROUPS = 8
HY_ORDER = 2
HY_SHORT = 3
HY_EMB = 33
HY_DECAY_TARGET = 1e-2
HY_FAST_DECAY = 0.3
HY_SLOW_DECAY = 1.5
N_MOD = 6
LN_EPS = 1e-5

LANE = 128
SUBLANES = 8
MXU_DIM = 256
VMEM_LIMIT = 48 * 1024 * 1024


def _cparams(sem):
    return pltpu.CompilerParams(dimension_semantics=sem, vmem_limit_bytes=VMEM_LIMIT)


def _resident(block_shape, index_map):
    return pl.BlockSpec(block_shape, index_map, pipeline_mode=pl.Buffered(1))


def _row_chunks(total, size):
    return [slice(r, r + size) for r in range(0, total, size)]


def _gelu(x):
    return 0.5 * x * (1.0 + lax.erf(x * (1.0 / math.sqrt(2.0))))


def _layer_norm(x, g, b):
    mu = jnp.mean(x, axis=-1, keepdims=True)
    xc = x - mu
    var = jnp.mean(xc * xc, axis=-1, keepdims=True)
    return xc * lax.rsqrt(var + LN_EPS) * g + b


def _dot_bf16x3(a, b):
    m = a.shape[0]
    a_hi = a.astype(BF16)
    b_hi = b.astype(BF16)
    a_hi32 = a_hi.astype(F32)
    b_lo = (b - b_hi.astype(F32)).astype(BF16)
    stacked = jnp.concatenate([a_hi32, a - a_hi32], axis=0).astype(BF16)
    top = jnp.dot(stacked, b_hi, preferred_element_type=F32)
    return top[:m] + top[m:] + jnp.dot(a_hi, b_lo, preferred_element_type=F32)


def _mod_kernel(c_ref, w_ref, b_ref, o_ref):
    c = c_ref[...]
    cond = c * jax.nn.sigmoid(c)
    o_ref[...] = _dot_bf16x3(cond, w_ref[...]) + b_ref[...]


def _mod_call(c, w_ada, b_ada):
    bsz, d = c.shape
    n = w_ada.shape[1]
    tn = 1024
    return pl.pallas_call(
        _mod_kernel,
        grid=(n // tn,),
        in_specs=[pl.BlockSpec((bsz, d), lambda j: (0, 0)),
                  pl.BlockSpec((d, tn), lambda j: (0, j)),
                  pl.BlockSpec((1, tn), lambda j: (0, j))],
        out_specs=pl.BlockSpec((bsz, tn), lambda j: (0, j)),
        out_shape=jax.ShapeDtypeStruct((bsz, n), F32),
        compiler_params=_cparams(("arbitrary",)),
        name="adaln_mod",
    )(c, w_ada, b_ada.reshape(1, n))


def _inproj_kernel(x_ref, sc_ref, sh_ref, w_ref, b_ref, g_ref, be_ref, o_ref, h_scr, *, seg, rc):
    j = pl.program_id(1)
    chunks = _row_chunks(x_ref.shape[0], rc)

    def emit(act, modulate=False):
        for rows in chunks:
            if modulate:
                h_scr[rows, :] = (x_ref[rows, :] * (1.0 + sc_ref[0]) + sh_ref[0]).astype(BF16)
            acc = jnp.dot(h_scr[rows, :], w_ref[...].astype(BF16),
                          preferred_element_type=F32) + b_ref[...]
            o_ref[rows, :] = act(acc).astype(o_ref.dtype)

    @pl.when(j == 0)
    def _():
        emit(jax.nn.sigmoid, modulate=True)

    @pl.when(jnp.logical_and(j > 0, j < seg[0]))
    def _():
        emit(jax.nn.sigmoid)

    @pl.when(jnp.logical_and(j >= seg[0], j < seg[1]))
    def _():
        emit(_gelu)

    @pl.when(jnp.logical_and(j >= seg[1], j < seg[2]))
    def _():
        emit(lambda a: _layer_norm(_gelu(a), g_ref[...], be_ref[...]))

    @pl.when(j >= seg[2])
    def _():
        emit(lambda a: a)


def _inproj_call(xt, mod3, w_in, b_in, ln_g, ln_b, seq, sgu_w, hy_w):
    t, d = xt.shape
    n = w_in.shape[1]
    tm = 1024
    tn = sgu_w
    assert hy_w == sgu_w and seq % tm == 0 and d % tn == 0
    nblk = n // tn
    n_gate = 2 * d // tn
    seg = (n_gate, n_gate + 1, n_gate + 2)
    per_b = seq // tm
    wcol = lambda i, j: (0, (j + (nblk - n_gate)) % nblk)
    return pl.pallas_call(
        functools.partial(_inproj_kernel, seg=seg, rc=256),
        grid=(t // tm, n // tn),
        in_specs=[pl.BlockSpec((tm, d), lambda i, j: (i, 0)),
                  pl.BlockSpec((1, 1, d), lambda i, j: (i // per_b, 0, 1)),
                  pl.BlockSpec((1, 1, d), lambda i, j: (i // per_b, 0, 0)),
                  pl.BlockSpec((d, tn), wcol),
                  pl.BlockSpec((1, tn), wcol),
                  pl.BlockSpec((1, tn), lambda i, j: (0, 0)),
                  pl.BlockSpec((1, tn), lambda i, j: (0, 0))],
        out_specs=pl.BlockSpec((tm, tn), lambda i, j: (i, j)),
        out_shape=jax.ShapeDtypeStruct((t, n), BF16),
        scratch_shapes=[pltpu.VMEM((tm, d), BF16)],
        compiler_params=_cparams(("parallel", "arbitrary")),
        name="in_proj",
    )(xt, mod3, mod3, w_in, b_in.reshape(1, n), ln_g.reshape(1, tn), ln_b.reshape(1, tn))


def _filter_mlp_kernel(f_ref, w1_ref, b1_ref, w2_ref, b2_ref, fr_ref, o_ref):
    hp = lax.Precision.HIGHEST
    fr = fr_ref[...]
    h = jnp.sin(fr * (jnp.dot(f_ref[...], w1_ref[...], preferred_element_type=F32, precision=hp)
                      + b1_ref[...]))
    o_ref[...] = jnp.sin(fr * (jnp.dot(h, w2_ref[...], preferred_element_type=F32, precision=hp)
                               + b2_ref[...]))


def _filter_mlp_call(feats, w1, b1, w2, b2, freq):
    seq = feats.shape[0]
    hid = w1.shape[1]
    return pl.pallas_call(
        _filter_mlp_kernel,
        out_shape=jax.ShapeDtypeStruct((seq, hid), F32),
        name="hyena_filter_mlp",
    )(feats, w1, b1.reshape(1, hid), w2, b2.reshape(1, hid), freq.reshape(1, hid))


DFT_GROUPS = 9
DFT_SLABS = 8
DFT_COMPS = 16
DFT_J = 32


def _packed_start(g, n2):
    return {0: 0, 8: n2}.get(g, 2 * n2 * g)


def _store_slab_block_order(dst_ref, val, t1, n2):
    step = DFT_SLABS * DFT_J
    for k in range(n2 // DFT_J):
        dst = step * k + DFT_J * t1
        dst_ref[dst:dst + DFT_J, :] = val[DFT_J * k:DFT_J * (k + 1), :].astype(dst_ref.dtype)


def _store_block_order(dst_ref, val, n2):
    for t1 in range(DFT_SLABS):
        _store_slab_block_order(dst_ref, val[n2 * t1:n2 * (t1 + 1), :], t1, n2)


def _stage1_fwd(s1_ref, src_ref, dst_ref, n2):
    step = DFT_SLABS * DFT_J
    for k in range(n2 // DFT_J):
        comps = jnp.dot(s1_ref[...], src_ref[step * k:step * (k + 1), :], preferred_element_type=F32)
        for c in range(DFT_COMPS):
            dst_ref[n2 * c + DFT_J * k:n2 * c + DFT_J * (k + 1), :] = (
                comps[DFT_J * c:DFT_J * (c + 1), :].astype(dst_ref.dtype))


def _stage2_fwd_part(g, part, tr_ref, tc_ref, src_ref, n2):
    start = _packed_start(g, n2)
    rows = slice(n2 * part, n2 * (part + 1))
    if g in (0, 8):
        return jnp.dot(tr_ref[0 if g == 0 else 1, rows, :], src_ref[start:start + n2, :],
                       preferred_element_type=F32)
    return jnp.dot(tc_ref[g - 1, rows, :], src_ref[start:start + 2 * n2, :],
                   preferred_element_type=F32)


def _dft_stage_tables(seq):
    n = 2 * seq
    n2 = n // 16
    f2 = jnp.arange(n2, dtype=jnp.int32)[:, None]
    t2 = jnp.arange(n2, dtype=jnp.int32)[None, :]

    def cs(g):
        ang = ((t2 * (g + 16 * f2)) & (n - 1)).astype(F32) * (2.0 * math.pi / n)
        return jnp.cos(ang), jnp.sin(ang)

    real = jnp.stack([jnp.concatenate([c, -s], axis=0) for c, s in (cs(0), cs(8))])
    cplx = jnp.stack([jnp.block([[c, s], [-s, c]]) for c, s in (cs(g) for g in range(1, 8))])
    gauss = jnp.stack([jnp.stack([c, s - c, c + s]) for c, s in (cs(g) for g in range(1, 8))])

    t1 = jnp.arange(DFT_SLABS, dtype=jnp.int32)
    rows = [jnp.ones((DFT_SLABS,), F32), jnp.where(t1 % 2 == 0, 1.0, -1.0).astype(F32)]
    for g in range(1, 8):
        ang = ((g * t1) % 16).astype(F32) * (math.pi / 8.0)
        rows += [jnp.cos(ang), -jnp.sin(ang)]
    s1 = jnp.kron(jnp.stack(rows), jnp.eye(DFT_J, dtype=F32))
    fwd = (s1, real, cplx, gauss)
    inv = (s1.T, jnp.swapaxes(real, 1, 2), jnp.swapaxes(gauss, 2, 3))
    return tuple(t.astype(BF16) for t in fwd), tuple(t.astype(BF16) for t in inv)


def _filter_spec_kernel(h_ref, wf_ref, wb_ref, t_ref, dl_ref, skip_ref, s1_ref, tr_ref, tc_ref,
                        k_ref, e_scr, o_scr, ge_scr, go_scr):
    seq = h_ref.shape[0]
    n2 = seq // 8
    window = jnp.exp(-t_ref[...] * dl_ref[...])
    _store_block_order(e_scr, _dot_bf16x3(h_ref[...], wf_ref[...] + wb_ref[...]) * window, n2)
    _store_block_order(o_scr, _dot_bf16x3(h_ref[...], wf_ref[...] - wb_ref[...]) * window, n2)
    _stage1_fwd(s1_ref, e_scr, ge_scr, n2)
    _stage1_fwd(s1_ref, o_scr, go_scr, n2)
    for g in range(DFT_GROUPS):
        wgt = (1.0 if g in (0, 8) else 2.0) / (2 * seq)
        rows = slice(2 * n2 * g, 2 * n2 * g + n2)
        rows_im = slice(2 * n2 * g + n2, 2 * n2 * (g + 1))
        k_ref[0, rows, :] = ((_stage2_fwd_part(g, 0, tr_ref, tc_ref, ge_scr, n2) + skip_ref[0])
                             * wgt).astype(k_ref.dtype)
        k_ref[0, rows_im, :] = (_stage2_fwd_part(g, 1, tr_ref, tc_ref, go_scr, n2)
                                * wgt).astype(k_ref.dtype)


def _filter_spec_call(h2, w3, tcol, deltas, skip, t_s1, t_real, t_cplx, width):
    seq, hid = h2.shape
    n2 = seq // 8
    cb = MXU_DIM
    ncb = width // cb
    per_dir = HY_ORDER * ncb
    rows = DFT_GROUPS * 2 * n2
    const = lambda n, k: (0, 0, 0)
    return pl.pallas_call(
        _filter_spec_kernel,
        grid=(HY_ORDER, ncb),
        in_specs=[_resident((seq, hid), lambda n, k: (0, 0)),
                  pl.BlockSpec((hid, cb), lambda n, k: (0, n * ncb + k)),
                  pl.BlockSpec((hid, cb), lambda n, k: (0, per_dir + n * ncb + k)),
                  _resident((seq, 1), lambda n, k: (0, 0)),
                  pl.BlockSpec((1, cb), lambda n, k: (0, k)),
                  pl.BlockSpec((1, 1, cb), lambda n, k: (n, 0, k)),
                  _resident(t_s1.shape, lambda n, k: (0, 0)),
                  _resident(t_real.shape, const), _resident(t_cplx.shape, const)],
        out_specs=pl.BlockSpec((1, rows, cb), lambda n, k: (n, 0, k)),
        out_shape=jax.ShapeDtypeStruct((HY_ORDER, rows, width), BF16),
        scratch_shapes=[pltpu.VMEM((seq, cb), BF16), pltpu.VMEM((seq, cb), BF16),
                        pltpu.VMEM((2 * seq, cb), BF16), pltpu.VMEM((2 * seq, cb), BF16)],
        compiler_params=_cparams(("arbitrary", "arbitrary")),
        name="hyena_filter_spectrum",
    )(h2, w3, w3, tcol, deltas, skip.reshape(HY_ORDER, 1, width), t_s1, t_real, t_cplx)


def _short_conv(z, w, b):
    seq = z.shape[0]
    prev = pltpu.roll(z, 1, 0)
    nxt = pltpu.roll(z, seq - 1, 0)
    row = lax.broadcasted_iota(jnp.int32, (SUBLANES, z.shape[1]), 0)
    prev = jnp.concatenate([jnp.where(row == 0, 0.0, prev[:SUBLANES]), prev[SUBLANES:]], axis=0)
    nxt = jnp.concatenate([nxt[:-SUBLANES], jnp.where(row == SUBLANES - 1, 0.0, nxt[-SUBLANES:])],
                          axis=0)
    return w[0:1] * prev + w[1:2] * z + w[2:3] * nxt + b


def _hyena_kernel(v_ref, x1_ref, x2_ref, wv_ref, w1_ref, w2_ref, bv_ref, b1_ref, b2_ref,
                  k_ref, s1_ref, tr_ref, tg_ref, s1t_ref, ur_ref, ug_ref, o_ref,
                  y16_scr, gate_scr, g_scr, yc_scr, h_scr):
    seq, cb = v_ref.shape
    n2 = seq // DFT_SLABS
    step = DFT_SLABS * DFT_J
    dot = functools.partial(jnp.dot, preferred_element_type=F32)

    conv3 = lambda z_ref, w_ref, b_ref: _short_conv(z_ref[...].astype(F32), w_ref[...], b_ref[...])
    _store_block_order(y16_scr, conv3(v_ref, wv_ref, bv_ref), n2)
    _store_block_order(gate_scr.at[0], conv3(x1_ref, w1_ref, b1_ref), n2)
    _store_block_order(gate_scr.at[1], conv3(x2_ref, w2_ref, b2_ref), n2)

    for n in range(HY_ORDER):
        last = n + 1 == HY_ORDER
        _stage1_fwd(s1_ref, y16_scr, g_scr, n2)
        for g in range(DFT_GROUPS):
            start = _packed_start(g, n2)
            if g in (0, 8):
                x = dot(tr_ref[0 if g == 0 else 1], g_scr[start:start + n2, :])
                xr, xi = x[:n2], x[n2:]
            else:
                a = g_scr[start:start + n2, :]
                b = g_scr[start + n2:start + 2 * n2, :]
                p1 = dot(tg_ref[g - 1, 0], a + b)
                xr = p1 + dot(tg_ref[g - 1, 1], b)
                xi = p1 - dot(tg_ref[g - 1, 2], a)
            xr, xi = xr.astype(BF16), xi.astype(BF16)
            base = 2 * n2 * g
            kr = k_ref[n, base:base + n2, :]
            ki = k_ref[n, base + n2:base + 2 * n2, :]
            yc_scr[base:base + n2, :] = xr * kr - xi * ki
            yc_scr[base + n2:base + 2 * n2, :] = xr * ki + xi * kr
        for g in range(DFT_GROUPS):
            base = 2 * n2 * g
            if g in (0, 8):
                parts = [(0 if g == 0 else 1,
                          dot(ur_ref[0 if g == 0 else 1], yc_scr[base:base + 2 * n2, :]))]
            else:
                a = yc_scr[base:base + n2, :]
                b = yc_scr[base + n2:base + 2 * n2, :]
                q1 = dot(ug_ref[g - 1, 0], a + b)
                parts = [(2 * g, q1 - dot(ug_ref[g - 1, 2], b)),
                         (2 * g + 1, q1 + dot(ug_ref[g - 1, 1], a))]
            for c, hp in parts:
                for k in range(n2 // DFT_J):
                    dst = DFT_COMPS * DFT_J * k + DFT_J * c
                    h_scr[dst:dst + DFT_J, :] = hp[DFT_J * k:DFT_J * (k + 1), :].astype(BF16)
        for k in range(n2 // DFT_J):
            rows = slice(step * k, step * (k + 1))
            conv = jnp.dot(s1t_ref[...], h_scr[DFT_COMPS * DFT_J * k:DFT_COMPS * DFT_J * (k + 1), :],
                           preferred_element_type=F32)
            y_new = (gate_scr[n, rows, :] * conv).astype(BF16)
            if last:
                for t1 in range(DFT_SLABS):
                    dst = n2 * t1 + DFT_J * k
                    o_ref[dst:dst + DFT_J, :] = y_new[DFT_J * t1:DFT_J * (t1 + 1), :]
            else:
                y16_scr[rows, :] = y_new


def _hyena_call(z, conv_w, conv_b, kspec, tables, bsz, seq, width, col0):
    cb = MXU_DIM
    ncb = width // cb
    base = col0 // cb
    zspec = lambda part: pl.BlockSpec((seq, cb), lambda k, b: (b, base + part * ncb + k))
    wspec = lambda part: pl.BlockSpec((HY_SHORT, cb), lambda k, b: (0, part * ncb + k))
    bspec = lambda part: pl.BlockSpec((1, cb), lambda k, b: (0, part * ncb + k))
    cb_all = conv_b.reshape(1, -1)
    const = lambda t: (lambda k, b: (0,) * t.ndim)
    return pl.pallas_call(
        _hyena_kernel,
        grid=(ncb, bsz),
        in_specs=[zspec(0), zspec(1), zspec(2), wspec(0), wspec(1), wspec(2),
                  bspec(0), bspec(1), bspec(2),
                  _resident((HY_ORDER, kspec.shape[1], cb), lambda k, b: (0, 0, k))]
                 + [_resident(t.shape, const(t)) for t in tables],
        out_specs=pl.BlockSpec((seq, cb), lambda k, b: (b, k)),
        out_shape=jax.ShapeDtypeStruct((bsz * seq, width), BF16),
        scratch_shapes=[pltpu.VMEM((seq, cb), BF16),
                        pltpu.VMEM((HY_ORDER, seq, cb), F32),
                        pltpu.VMEM((2 * seq, cb), BF16),
                        pltpu.VMEM((kspec.shape[1], cb), BF16),
                        pltpu.VMEM((2 * seq, cb), BF16)],
        compiler_params=_cparams(("arbitrary", "arbitrary")),
        name="hyena_mixer",
    )(z, z, z, conv_w, conv_w, conv_w, cb_all, cb_all, cb_all, kspec, *tables)


def _merge_kernel(u_ref, vn_ref, yh_ref, ga_ref, gb_ref, ws_ref, bs_ref, wa_ref, wb_ref, o_ref,
                  s_scr):
    tm = u_ref.shape[0]
    gd = u_ref.shape[1] // SGU_GROUPS
    for n in range(tm // SGU_CHUNK):
        rows = slice(n * SGU_CHUNK, (n + 1) * SGU_CHUNK)
        for g in range(SGU_GROUPS):
            cols = slice(g * gd, (g + 1) * gd)
            mixed = jnp.dot(ws_ref[g], vn_ref[rows, cols], preferred_element_type=F32)
            s_scr[rows, cols] = (u_ref[rows, cols].astype(F32)
                                 * (mixed + bs_ref[:, cols])).astype(BF16)
    y_a = jnp.dot(s_scr[...], wa_ref[...].astype(BF16), preferred_element_type=F32)
    y_b = jnp.dot(yh_ref[...], wb_ref[...].astype(BF16), preferred_element_type=F32)
    o_ref[...] = (ga_ref[...].astype(F32) * y_a + gb_ref[...].astype(F32) * y_b).astype(o_ref.dtype)


def _merge_call(z, yh, sgu_w16, sgu_bfull, w_a, w_b, d):
    t = z.shape[0]
    wdt = w_a.shape[0]
    tm = 512
    u_blk = 2 * d // wdt
    return pl.pallas_call(
        _merge_kernel,
        grid=(t // tm,),
        in_specs=[pl.BlockSpec((tm, wdt), lambda i: (i, u_blk)),
                  pl.BlockSpec((tm, wdt), lambda i: (i, u_blk + 1)),
                  pl.BlockSpec((tm, wdt), lambda i: (i, 0)),
                  pl.BlockSpec((tm, d), lambda i: (i, 0)),
                  pl.BlockSpec((tm, d), lambda i: (i, 1)),
                  _resident(sgu_w16.shape, lambda i: (0, 0, 0)),
                  _resident(sgu_bfull.shape, lambda i: (0, 0)),
                  _resident(w_a.shape, lambda i: (0, 0)),
                  _resident(w_b.shape, lambda i: (0, 0))],
        out_specs=pl.BlockSpec((tm, d), lambda i: (i, 0)),
        out_shape=jax.ShapeDtypeStruct((t, d), BF16),
        scratch_shapes=[pltpu.VMEM((tm, wdt), BF16)],
        compiler_params=_cparams(("parallel",)),
        name="gated_merge",
    )(z, z, yh, z, z, sgu_w16, sgu_bfull, w_a, w_b)


def _oproj_kernel(m_ref, x_ref, g1_ref, w_ref, b_ref, lg_ref, lb_ref, o_ref, *, alpha, rc):
    for rows in _row_chunks(m_ref.shape[0], rc):
        mix = jnp.dot(m_ref[rows, :], w_ref[...].astype(BF16),
                      preferred_element_type=F32) + b_ref[...]
        o_ref[rows, :] = _layer_norm(alpha * x_ref[rows, :] + g1_ref[0] * mix,
                                     lg_ref[...], lb_ref[...])


def _oproj_call(m, xt, mod3, w_o, b_o, ln_g, ln_b, seq, alpha):
    t, d = xt.shape
    tm = 512
    per_b = seq // tm
    row = lambda i: (0, 0)
    return pl.pallas_call(
        functools.partial(_oproj_kernel, alpha=alpha, rc=256),
        grid=(t // tm,),
        in_specs=[pl.BlockSpec((tm, d), lambda i: (i, 0)),
                  pl.BlockSpec((tm, d), lambda i: (i, 0)),
                  pl.BlockSpec((1, 1, d), lambda i: (i // per_b, 0, 2)),
                  _resident((d, d), row),
                  pl.BlockSpec((1, d), row), pl.BlockSpec((1, d), row), pl.BlockSpec((1, d), row)],
        out_specs=pl.BlockSpec((tm, d), lambda i: (i, 0)),
        out_shape=jax.ShapeDtypeStruct((t, d), F32),
        compiler_params=_cparams(("parallel",)),
        name="out_proj_ln",
    )(m, xt, mod3, w_o, b_o.reshape(1, d), ln_g.reshape(1, d), ln_b.reshape(1, d))


def _mlp_kernel(x_ref, sc_ref, sh_ref, g2_ref, w1_ref, b1_ref, w2_ref, b2_ref, lg_ref, lb_ref,
                o_ref, h_scr, acc_scr, *, alpha, rc):
    j = pl.program_id(1)
    last = pl.num_programs(1) - 1
    chunks = _row_chunks(x_ref.shape[0], rc)

    def partial_out(rows):
        hid = jnp.dot(h_scr[rows, :], w1_ref[...], preferred_element_type=F32) + b1_ref[...]
        hid = jnp.square(jnp.maximum(hid, 0.0)).astype(BF16)
        return jnp.dot(hid, w2_ref[...], preferred_element_type=F32)

    @pl.when(j == 0)
    def _():
        for rows in chunks:
            h_scr[rows, :] = (x_ref[rows, :] * (1.0 + sc_ref[0]) + sh_ref[0]).astype(BF16)
            acc_scr[rows, :] = partial_out(rows)

    @pl.when(jnp.logical_and(j > 0, j < last))
    def _():
        acc_scr[...] += partial_out(slice(None))

    @pl.when(j == last)
    def _():
        for rows in chunks:
            f = acc_scr[rows, :] + partial_out(rows) + b2_ref[...]
            o_ref[rows, :] = _layer_norm(alpha * x_ref[rows, :] + g2_ref[0] * f,
                                         lg_ref[...], lb_ref[...])


def _mlp_call(x1, mod3, w_m1, b_m1, w_m2, b_m2, ln_g, ln_b, seq, alpha):
    t, d = x1.shape
    dff = w_m1.shape[1]
    tm = 512
    tf = 1024
    per_b = seq // tm
    row = lambda i, j: (0, 0)
    return pl.pallas_call(
        functools.partial(_mlp_kernel, alpha=alpha, rc=256),
        grid=(t // tm, dff // tf),
        in_specs=[pl.BlockSpec((tm, d), lambda i, j: (i, 0)),
                  pl.BlockSpec((1, 1, d), lambda i, j: (i // per_b, 0, 4)),
                  pl.BlockSpec((1, 1, d), lambda i, j: (i // per_b, 0, 3)),
                  pl.BlockSpec((1, 1, d), lambda i, j: (i // per_b, 0, 5)),
                  pl.BlockSpec((d, tf), lambda i, j: (0, j)),
                  pl.BlockSpec((1, tf), lambda i, j: (0, j)),
                  pl.BlockSpec((tf, d), lambda i, j: (j, 0)),
                  pl.BlockSpec((1, d), row), pl.BlockSpec((1, d), row), pl.BlockSpec((1, d), row)],
        out_specs=pl.BlockSpec((tm, d), lambda i, j: (i, 0)),
        out_shape=jax.ShapeDtypeStruct((t, d), F32),
        scratch_shapes=[pltpu.VMEM((tm, d), BF16), pltpu.VMEM((tm, d), F32)],
        compiler_params=_cparams(("parallel", "arbitrary")),
        name="mlp_ln",
    )(x1, mod3, mod3, mod3, w_m1, b_m1.reshape(1, dff), w_m2, b_m2.reshape(1, d),
      ln_g.reshape(1, d), ln_b.reshape(1, d))


def _filter_features(seq):
    bands_n = (HY_EMB - 1) // 2
    t = jnp.linspace(0.0, 1.0, seq, dtype=F32)[:, None]
    omega = 2.0 * math.pi * jnp.arange(seq, dtype=F32)[:, None] / seq
    bands = jnp.linspace(1e-4, bands_n - 1, bands_n, dtype=F32)[None, :]
    feats = jnp.concatenate([t, jnp.cos(bands * omega), -jnp.sin(bands * omega)], axis=-1)
    return t, feats


def _decay_rates(width):
    min_decay = math.log(HY_DECAY_TARGET) / HY_SLOW_DECAY
    max_decay = math.log(HY_DECAY_TARGET) / HY_FAST_DECAY
    return jnp.abs(jnp.linspace(min_decay, max_decay, width, dtype=F32))[None, :]


def kernel(x, c, w_ada, b_ada, w_in, b_in, sgu_ln_g, sgu_ln_b, sgu_w, sgu_b, hy_conv_w, hy_conv_b, hy_w1, hy_b1, hy_w2, hy_b2, hy_freq, hy_w3, hy_skip, w_branch_a, w_branch_b, w_o, b_o, ln1_g, ln1_b, w_m1, b_m1, w_m2, b_m2, ln2_g, ln2_b):
    bsz, seq, d = x.shape
    depth = w_ada.shape[0]
    alpha = (2.0 * depth) ** 0.25
    sgu_width = w_branch_a.shape[1]
    hy_width = w_branch_b.shape[1]
    assert seq % SGU_CHUNK == 0 and sgu_w.shape[-1] == SGU_CHUNK

    assert seq % (8 * MXU_DIM) == 0
    (t_s1, t_real, t_cplx, t_gauss), inv_tables = _dft_stage_tables(seq)
    tcol, feats = _filter_features(seq)
    deltas = _decay_rates(hy_width)

    xt = x.reshape(bsz * seq, d)
    for l in range(depth):
        mod3 = _mod_call(c, w_ada[l], b_ada[l]).reshape(bsz, 1, N_MOD * d)

        z = _inproj_call(xt, mod3, w_in[l], b_in[l], sgu_ln_g[l], sgu_ln_b[l],
                         seq, sgu_width, hy_width)

        h2 = _filter_mlp_call(feats, hy_w1[l], hy_b1[l], hy_w2[l], hy_b2[l], hy_freq[l])
        kspec = _filter_spec_call(h2, hy_w3[l], tcol, deltas, hy_skip[l], t_s1, t_real, t_cplx,
                                  hy_width)
        yh = _hyena_call(z, hy_conv_w[l], hy_conv_b[l], kspec, (t_s1, t_real, t_gauss) + inv_tables,
                         bsz, seq, hy_width, 2 * d + 2 * sgu_width)

        gd = sgu_width // SGU_GROUPS
        bs_full = jnp.repeat(sgu_b[l].T, gd, axis=1)
        m = _merge_call(z, yh, sgu_w[l].astype(BF16), bs_full, w_branch_a[l], w_branch_b[l], d)

        x1 = _oproj_call(m, xt, mod3, w_o[l], b_o[l], ln1_g[l], ln1_b[l], seq, alpha)
        xt = _mlp_call(x1, mod3, w_m1[l].astype(BF16), b_m1[l], w_m2[l].astype(BF16), b_m2[l],
                       ln2_g[l], ln2_b[l], seq, alpha)
    return xt.reshape(bsz, seq, d)
```

```python
import functools
import math

import jax
import jax.numpy as jnp
from jax import lax
from jax.experimental import pallas as pl
from jax.experimental.pallas import tpu as pltpu

F32 = jnp.float32
BF16 = jnp.bfloat16

SGU_CHUNK = 128
SGU_GROUPS = 8
HY_ORDER = 2
HY_SHORT = 3
HY_EMB = 33
HY_DECAY_TARGET = 1e-2
HY_FAST_DECAY = 0.3
HY_SLOW_DECAY = 1.5
N_MOD = 6
LN_EPS = 1e-5

LANE = 128
SUBLANES = 8
MXU_DIM = 256
VMEM_LIMIT = 48 * 1024 * 1024


def _cparams(sem):
    return pltpu.CompilerParams(dimension_semantics=sem, vmem_limit_bytes=VMEM_LIMIT)


def _resident(block_shape, index_map):
    return pl.BlockSpec(block_shape, index_map, pipeline_mode=pl.Buffered(1))


def _row_chunks(total, size):
    return [slice(r, r + size) for r in range(0, total, size)]


def _gelu(x):
    return 0.5 * x * (1.0 + lax.erf(x * (1.0 / math.sqrt(2.0))))


def _layer_norm(x, g, b):
    mu = jnp.mean(x, axis=-1, keepdims=True)
    xc = x - mu
    var = jnp.mean(xc * xc, axis=-1, keepdims=True)
    return xc * lax.rsqrt(var + LN_EPS) * g + b


def _dot_bf16x3(a, b):
    m = a.shape[0]
    a_hi = a.astype(BF16)
    b_hi = b.astype(BF16)
    a_hi32 = a_hi.astype(F32)
    b_lo = (b - b_hi.astype(F32)).astype(BF16)
    stacked = jnp.concatenate([a_hi32, a - a_hi32], axis=0).astype(BF16)
    top = jnp.dot(stacked, b_hi, preferred_element_type=F32)
    return top[:m] + top[m:] + jnp.dot(a_hi, b_lo, preferred_element_type=F32)


def _mod_kernel(c_ref, w_ref, b_ref, o_ref):
    c = c_ref[...]
    cond = c * jax.nn.sigmoid(c)
    o_ref[...] = _dot_bf16x3(cond, w_ref[...]) + b_ref[...]


def _mod_call(c, w_ada, b_ada):
    bsz, d = c.shape
    n = w_ada.shape[1]
    tn = 1024
    return pl.pallas_call(
        _mod_kernel,
        grid=(n // tn,),
        in_specs=[pl.BlockSpec((bsz, d), lambda j: (0, 0)),
                  pl.BlockSpec((d, tn), lambda j: (0, j)),
                  pl.BlockSpec((1, tn), lambda j: (0, j))],
        out_specs=pl.BlockSpec((bsz, tn), lambda j: (0, j)),
        out_shape=jax.ShapeDtypeStruct((bsz, n), F32),
        compiler_params=_cparams(("arbitrary",)),
        name="adaln_mod",
    )(c, w_ada, b_ada.reshape(1, n))


def _inproj_kernel(x_ref, sc_ref, sh_ref, w_ref, b_ref, g_ref, be_ref, o_ref, h_scr, *, seg, rc):
    j = pl.program_id(1)
    chunks = _row_chunks(x_ref.shape[0], rc)

    def emit(act, modulate=False):
        for rows in chunks:
            if modulate:
                h_scr[rows, :] = (x_ref[rows, :] * (1.0 + sc_ref[0]) + sh_ref[0]).astype(BF16)
            acc = jnp.dot(h_scr[rows, :], w_ref[...], preferred_element_type=F32) + b_ref[...]
            o_ref[rows, :] = act(acc).astype(o_ref.dtype)

    @pl.when(j == 0)
    def _():
        emit(jax.nn.sigmoid, modulate=True)

    @pl.when(jnp.logical_and(j > 0, j < seg[0]))
    def _():
        emit(jax.nn.sigmoid)

    @pl.when(jnp.logical_and(j >= seg[0], j < seg[1]))
    def _():
        emit(_gelu)

    @pl.when(jnp.logical_and(j >= seg[1], j < seg[2]))
    def _():
        emit(lambda a: _layer_norm(_gelu(a), g_ref[...], be_ref[...]))

    @pl.when(j >= seg[2])
    def _():
        emit(lambda a: a)


def _inproj_call(xt, mod3, w_in, b_in, ln_g, ln_b, seq, sgu_w, hy_w):
    t, d = xt.shape
    n = w_in.shape[1]
    tm = 1024
    tn = sgu_w
    assert hy_w == sgu_w and seq % tm == 0 and d % tn == 0
    nblk = n // tn
    n_gate = 2 * d // tn
    seg = (n_gate, n_gate + 1, n_gate + 2)
    per_b = seq // tm
    wcol = lambda i, j: (0, (j + (nblk - n_gate)) % nblk)
    return pl.pallas_call(
        functools.partial(_inproj_kernel, seg=seg, rc=256),
        grid=(t // tm, n // tn),
        in_specs=[pl.BlockSpec((tm, d), lambda i, j: (i, 0)),
                  pl.BlockSpec((1, 1, d), lambda i, j: (i // per_b, 0, 1)),
                  pl.BlockSpec((1, 1, d), lambda i, j: (i // per_b, 0, 0)),
                  pl.BlockSpec((d, tn), wcol),
                  pl.BlockSpec((1, tn), wcol),
                  pl.BlockSpec((1, tn), lambda i, j: (0, 0)),
                  pl.BlockSpec((1, tn), lambda i, j: (0, 0))],
        out_specs=pl.BlockSpec((tm, tn), lambda i, j: (i, j)),
        out_shape=jax.ShapeDtypeStruct((t, n), BF16),
        scratch_shapes=[pltpu.VMEM((tm, d), BF16)],
        compiler_params=_cparams(("parallel", "arbitrary")),
        name="in_proj",
    )(xt, mod3, mod3, w_in, b_in.reshape(1, n), ln_g.reshape(1, tn), ln_b.reshape(1, tn))


def _filter_mlp_kernel(f_ref, w1_ref, b1_ref, w2_ref, b2_ref, fr_ref, o_ref):
    hp = lax.Precision.HIGHEST
    fr = fr_ref[...]
    h = jnp.sin(fr * (jnp.dot(f_ref[...], w1_ref[...], preferred_element_type=F32, precision=hp)
                      + b1_ref[...]))
    o_ref[...] = jnp.sin(fr * (jnp.dot(h, w2_ref[...], preferred_element_type=F32, precision=hp)
                               + b2_ref[...]))


def _filter_mlp_call(feats, w1, b1, w2, b2, freq):
    seq = feats.shape[0]
    hid = w1.shape[1]
    return pl.pallas_call(
        _filter_mlp_kernel,
        out_shape=jax.ShapeDtypeStruct((seq, hid), F32),
        name="hyena_filter_mlp",
    )(feats, w1, b1.reshape(1, hid), w2, b2.reshape(1, hid), freq.reshape(1, hid))


DFT_GROUPS = 9
DFT_SLABS = 8
DFT_COMPS = 16
DFT_J = 32


def _packed_start(g, n2):
    return {0: 0, 8: n2}.get(g, 2 * n2 * g)


def _store_slab_block_order(dst_ref, val, t1, n2):
    step = DFT_SLABS * DFT_J
    for k in range(n2 // DFT_J):
        dst = step * k + DFT_J * t1
        dst_ref[dst:dst + DFT_J, :] = val[DFT_J * k:DFT_J * (k + 1), :].astype(dst_ref.dtype)


def _store_block_order(dst_ref, val, n2):
    for t1 in range(DFT_SLABS):
        _store_slab_block_order(dst_ref, val[n2 * t1:n2 * (t1 + 1), :], t1, n2)


def _stage1_fwd(s1_ref, src_ref, dst_ref, n2):
    step = DFT_SLABS * DFT_J
    for k in range(n2 // DFT_J):
        comps = jnp.dot(s1_ref[...], src_ref[step * k:step * (k + 1), :], preferred_element_type=F32)
        for c in range(DFT_COMPS):
            dst_ref[n2 * c + DFT_J * k:n2 * c + DFT_J * (k + 1), :] = (
                comps[DFT_J * c:DFT_J * (c + 1), :].astype(dst_ref.dtype))


def _stage2_fwd_part(g, part, tr_ref, tc_ref, src_ref, n2):
    start = _packed_start(g, n2)
    rows = slice(n2 * part, n2 * (part + 1))
    if g in (0, 8):
        return jnp.dot(tr_ref[0 if g == 0 else 1, rows, :], src_ref[start:start + n2, :],
                       preferred_element_type=F32)
    return jnp.dot(tc_ref[g - 1, rows, :], src_ref[start:start + 2 * n2, :],
                   preferred_element_type=F32)


def _dft_stage_tables(seq):
    n = 2 * seq
    n2 = n // 16
    f2 = jnp.arange(n2, dtype=jnp.int32)[:, None]
    t2 = jnp.arange(n2, dtype=jnp.int32)[None, :]

    def cs(g):
        ang = ((t2 * (g + 16 * f2)) & (n - 1)).astype(F32) * (2.0 * math.pi / n)
        return jnp.cos(ang), jnp.sin(ang)

    real = jnp.stack([jnp.concatenate([c, -s], axis=0) for c, s in (cs(0), cs(8))])
    cplx = jnp.stack([jnp.block([[c, s], [-s, c]]) for c, s in (cs(g) for g in range(1, 8))])
    gauss = jnp.stack([jnp.stack([c, s - c, c + s]) for c, s in (cs(g) for g in range(1, 8))])

    t1 = jnp.arange(DFT_SLABS, dtype=jnp.int32)
    rows = [jnp.ones((DFT_SLABS,), F32), jnp.where(t1 % 2 == 0, 1.0, -1.0).astype(F32)]
    for g in range(1, 8):
        ang = ((g * t1) % 16).astype(F32) * (math.pi / 8.0)
        rows += [jnp.cos(ang), -jnp.sin(ang)]
    s1 = jnp.kron(jnp.stack(rows), jnp.eye(DFT_J, dtype=F32))
    fwd = (s1, real, cplx, gauss)
    inv = (s1.T, jnp.swapaxes(real, 1, 2), jnp.swapaxes(gauss, 2, 3))
    return tuple(t.astype(BF16) for t in fwd), tuple(t.astype(BF16) for t in inv)


def _filter_spec_kernel(h_ref, wf_ref, wb_ref, t_ref, dl_ref, skip_ref, s1_ref, tr_ref, tc_ref,
                        cast_ref, k_ref, cast_out, e_scr, o_scr, ge_scr, go_scr):
    seq = h_ref.shape[0]
    n2 = seq // 8
    cast_out[...] = cast_ref[...].astype(cast_out.dtype)
    window = jnp.exp(-t_ref[...] * dl_ref[...])
    _store_block_order(e_scr, _dot_bf16x3(h_ref[...], wf_ref[...] + wb_ref[...]) * window, n2)
    _store_block_order(o_scr, _dot_bf16x3(h_ref[...], wf_ref[...] - wb_ref[...]) * window, n2)
    _stage1_fwd(s1_ref, e_scr, ge_scr, n2)
    _stage1_fwd(s1_ref, o_scr, go_scr, n2)
    for g in range(DFT_GROUPS):
        wgt = (1.0 if g in (0, 8) else 2.0) / (2 * seq)
        rows = slice(2 * n2 * g, 2 * n2 * g + n2)
        rows_im = slice(2 * n2 * g + n2, 2 * n2 * (g + 1))
        k_ref[0, rows, :] = ((_stage2_fwd_part(g, 0, tr_ref, tc_ref, ge_scr, n2) + skip_ref[0])
                             * wgt).astype(k_ref.dtype)
        k_ref[0, rows_im, :] = (_stage2_fwd_part(g, 1, tr_ref, tc_ref, go_scr, n2)
                                * wgt).astype(k_ref.dtype)


def _filter_spec_call(h2, w3, tcol, deltas, skip, t_s1, t_real, t_cplx, to_cast, width):
    seq, hid = h2.shape
    n2 = seq // 8
    cb = MXU_DIM
    ncb = width // cb
    per_dir = HY_ORDER * ncb
    rows = DFT_GROUPS * 2 * n2
    const = lambda n, k: (0, 0, 0)
    steps = HY_ORDER * ncb
    assert to_cast.shape[0] % (steps * 2 * SUBLANES) == 0
    slab = pl.BlockSpec((to_cast.shape[0] // steps, to_cast.shape[1]), lambda n, k: (n * ncb + k, 0))
    return pl.pallas_call(
        _filter_spec_kernel,
        grid=(HY_ORDER, ncb),
        in_specs=[_resident((seq, hid), lambda n, k: (0, 0)),
                  pl.BlockSpec((hid, cb), lambda n, k: (0, n * ncb + k)),
                  pl.BlockSpec((hid, cb), lambda n, k: (0, per_dir + n * ncb + k)),
                  _resident((seq, 1), lambda n, k: (0, 0)),
                  pl.BlockSpec((1, cb), lambda n, k: (0, k)),
                  pl.BlockSpec((1, 1, cb), lambda n, k: (n, 0, k)),
                  _resident(t_s1.shape, lambda n, k: (0, 0)),
                  _resident(t_real.shape, const), _resident(t_cplx.shape, const), slab],
        out_specs=[pl.BlockSpec((1, rows, cb), lambda n, k: (n, 0, k)), slab],
        out_shape=[jax.ShapeDtypeStruct((HY_ORDER, rows, width), BF16),
                   jax.ShapeDtypeStruct(to_cast.shape, BF16)],
        scratch_shapes=[pltpu.VMEM((seq, cb), BF16), pltpu.VMEM((seq, cb), BF16),
                        pltpu.VMEM((2 * seq, cb), BF16), pltpu.VMEM((2 * seq, cb), BF16)],
        compiler_params=_cparams(("arbitrary", "arbitrary")),
        name="hyena_filter_spectrum",
    )(h2, w3, w3, tcol, deltas, skip.reshape(HY_ORDER, 1, width), t_s1, t_real, t_cplx, to_cast)


def _short_conv(z, w, b):
    seq = z.shape[0]
    prev = pltpu.roll(z, 1, 0)
    nxt = pltpu.roll(z, seq - 1, 0)
    row = lax.broadcasted_iota(jnp.int32, (SUBLANES, z.shape[1]), 0)
    prev = jnp.concatenate([jnp.where(row == 0, 0.0, prev[:SUBLANES]), prev[SUBLANES:]], axis=0)
    nxt = jnp.concatenate([nxt[:-SUBLANES], jnp.where(row == SUBLANES - 1, 0.0, nxt[-SUBLANES:])],
                          axis=0)
    return w[0:1] * prev + w[1:2] * z + w[2:3] * nxt + b


def _hyena_kernel(v_ref, x1_ref, x2_ref, wv_ref, w1_ref, w2_ref, bv_ref, b1_ref, b2_ref,
                  k_ref, s1_ref, tr_ref, tg_ref, s1t_ref, ur_ref, ug_ref, cast1_ref, cast2_ref,
                  o_ref, cast1_out, cast2_out, y16_scr, gate_scr, g_scr, yc_scr, h_scr):
    seq, cb = v_ref.shape
    n2 = seq // DFT_SLABS
    step = DFT_SLABS * DFT_J
    dot = functools.partial(jnp.dot, preferred_element_type=F32)
    cast1_out[...] = cast1_ref[...].astype(cast1_out.dtype)
    cast2_out[...] = cast2_ref[...].astype(cast2_out.dtype)

    conv3 = lambda z_ref, w_ref, b_ref: _short_conv(z_ref[...].astype(F32), w_ref[...], b_ref[...])
    _store_block_order(y16_scr, conv3(v_ref, wv_ref, bv_ref), n2)
    _store_block_order(gate_scr.at[0], conv3(x1_ref, w1_ref, b1_ref), n2)
    _store_block_order(gate_scr.at[1], conv3(x2_ref, w2_ref, b2_ref), n2)

    for n in range(HY_ORDER):
        last = n + 1 == HY_ORDER
        _stage1_fwd(s1_ref, y16_scr, g_scr, n2)
        for g in range(DFT_GROUPS):
            start = _packed_start(g, n2)
            if g in (0, 8):
                x = dot(tr_ref[0 if g == 0 else 1], g_scr[start:start + n2, :])
                xr, xi = x[:n2], x[n2:]
            else:
                a = g_scr[start:start + n2, :]
                b = g_scr[start + n2:start + 2 * n2, :]
                p1 = dot(tg_ref[g - 1, 0], a + b)
                xr = p1 + dot(tg_ref[g - 1, 1], b)
                xi = p1 - dot(tg_ref[g - 1, 2], a)
            xr, xi = xr.astype(BF16), xi.astype(BF16)
            base = 2 * n2 * g
            kr = k_ref[n, base:base + n2, :]
            ki = k_ref[n, base + n2:base + 2 * n2, :]
            yc_scr[base:base + n2, :] = xr * kr - xi * ki
            yc_scr[base + n2:base + 2 * n2, :] = xr * ki + xi * kr
        for g in range(DFT_GROUPS):
            base = 2 * n2 * g
            if g in (0, 8):
                parts = [(0 if g == 0 else 1,
                          dot(ur_ref[0 if g == 0 else 1], yc_scr[base:base + 2 * n2, :]))]
            else:
                a = yc_scr[base:base + n2, :]
                b = yc_scr[base + n2:base + 2 * n2, :]
                q1 = dot(ug_ref[g - 1, 0], a + b)
                parts = [(2 * g, q1 - dot(ug_ref[g - 1, 2], b)),
                         (2 * g + 1, q1 + dot(ug_ref[g - 1, 1], a))]
            for c, hp in parts:
                for k in range(n2 // DFT_J):
                    dst = DFT_COMPS * DFT_J * k + DFT_J * c
                    h_scr[dst:dst + DFT_J, :] = hp[DFT_J * k:DFT_J * (k + 1), :].astype(BF16)
        for k in range(n2 // DFT_J):
            rows = slice(step * k, step * (k + 1))
            conv = jnp.dot(s1t_ref[...], h_scr[DFT_COMPS * DFT_J * k:DFT_COMPS * DFT_J * (k + 1), :],
                           preferred_element_type=F32)
            y_new = (gate_scr[n, rows, :] * conv).astype(BF16)
            if last:
                for t1 in range(DFT_SLABS):
                    dst = n2 * t1 + DFT_J * k
                    o_ref[dst:dst + DFT_J, :] = y_new[DFT_J * t1:DFT_J * (t1 + 1), :]
            else:
                y16_scr[rows, :] = y_new


def _hyena_call(z, conv_w, conv_b, kspec, tables, to_cast, bsz, seq, width, col0):
    cb = MXU_DIM
    ncb = width // cb
    base = col0 // cb
    zspec = lambda part: pl.BlockSpec((seq, cb), lambda k, b: (b, base + part * ncb + k))
    wspec = lambda part: pl.BlockSpec((HY_SHORT, cb), lambda k, b: (0, part * ncb + k))
    bspec = lambda part: pl.BlockSpec((1, cb), lambda k, b: (0, part * ncb + k))
    cb_all = conv_b.reshape(1, -1)
    const = lambda t: (lambda k, b: (0,) * t.ndim)
    steps = ncb * bsz
    slab = lambda w: pl.BlockSpec((w.shape[0] // steps, w.shape[1]), lambda k, b: (k * bsz + b, 0))
    assert all(w.shape[0] % (steps * 2 * SUBLANES) == 0 for w in to_cast)
    return pl.pallas_call(
        _hyena_kernel,
        grid=(ncb, bsz),
        in_specs=[zspec(0), zspec(1), zspec(2), wspec(0), wspec(1), wspec(2),
                  bspec(0), bspec(1), bspec(2),
                  _resident((HY_ORDER, kspec.shape[1], cb), lambda k, b: (0, 0, k))]
                 + [_resident(t.shape, const(t)) for t in tables] + [slab(w) for w in to_cast],
        out_specs=[pl.BlockSpec((seq, cb), lambda k, b: (b, k))] + [slab(w) for w in to_cast],
        out_shape=[jax.ShapeDtypeStruct((bsz * seq, width), BF16)]
                  + [jax.ShapeDtypeStruct(w.shape, BF16) for w in to_cast],
        scratch_shapes=[pltpu.VMEM((seq, cb), BF16),
                        pltpu.VMEM((HY_ORDER, seq, cb), F32),
                        pltpu.VMEM((2 * seq, cb), BF16),
                        pltpu.VMEM((kspec.shape[1], cb), BF16),
                        pltpu.VMEM((2 * seq, cb), BF16)],
        compiler_params=_cparams(("arbitrary", "arbitrary")),
        name="hyena_mixer",
    )(z, z, z, conv_w, conv_w, conv_w, cb_all, cb_all, cb_all, kspec, *tables, *to_cast)


def _merge_kernel(u_ref, vn_ref, yh_ref, ga_ref, gb_ref, ws_ref, bs_ref, wa_ref, wb_ref, o_ref,
                  s_scr):
    tm = u_ref.shape[0]
    gd = u_ref.shape[1] // SGU_GROUPS
    for n in range(tm // SGU_CHUNK):
        rows = slice(n * SGU_CHUNK, (n + 1) * SGU_CHUNK)
        for g in range(SGU_GROUPS):
            cols = slice(g * gd, (g + 1) * gd)
            mixed = jnp.dot(ws_ref[g], vn_ref[rows, cols], preferred_element_type=F32)
            s_scr[rows, cols] = (u_ref[rows, cols].astype(F32)
                                 * (mixed + bs_ref[:, cols])).astype(BF16)
    y_a = jnp.dot(s_scr[...], wa_ref[...].astype(BF16), preferred_element_type=F32)
    y_b = jnp.dot(yh_ref[...], wb_ref[...].astype(BF16), preferred_element_type=F32)
    o_ref[...] = (ga_ref[...].astype(F32) * y_a + gb_ref[...].astype(F32) * y_b).astype(o_ref.dtype)


def _merge_call(z, yh, sgu_w16, sgu_bfull, w_a, w_b, d):
    t = z.shape[0]
    wdt = w_a.shape[0]
    tm = 512
    u_blk = 2 * d // wdt
    return pl.pallas_call(
        _merge_kernel,
        grid=(t // tm,),
        in_specs=[pl.BlockSpec((tm, wdt), lambda i: (i, u_blk)),
                  pl.BlockSpec((tm, wdt), lambda i: (i, u_blk + 1)),
                  pl.BlockSpec((tm, wdt), lambda i: (i, 0)),
                  pl.BlockSpec((tm, d), lambda i: (i, 0)),
                  pl.BlockSpec((tm, d), lambda i: (i, 1)),
                  _resident(sgu_w16.shape, lambda i: (0, 0, 0)),
                  _resident(sgu_bfull.shape, lambda i: (0, 0)),
                  _resident(w_a.shape, lambda i: (0, 0)),
                  _resident(w_b.shape, lambda i: (0, 0))],
        out_specs=pl.BlockSpec((tm, d), lambda i: (i, 0)),
        out_shape=jax.ShapeDtypeStruct((t, d), BF16),
        scratch_shapes=[pltpu.VMEM((tm, wdt), BF16)],
        compiler_params=_cparams(("parallel",)),
        name="gated_merge",
    )(z, z, yh, z, z, sgu_w16, sgu_bfull, w_a, w_b)


def _oproj_kernel(m_ref, x_ref, g1_ref, w_ref, b_ref, lg_ref, lb_ref, o_ref, *, alpha, rc):
    for rows in _row_chunks(m_ref.shape[0], rc):
        mix = jnp.dot(m_ref[rows, :], w_ref[...].astype(BF16),
                      preferred_element_type=F32) + b_ref[...]
        o_ref[rows, :] = _layer_norm(alpha * x_ref[rows, :] + g1_ref[0] * mix,
                                     lg_ref[...], lb_ref[...])


def _oproj_call(m, xt, mod3, w_o, b_o, ln_g, ln_b, seq, alpha):
    t, d = xt.shape
    tm = 512
    per_b = seq // tm
    row = lambda i: (0, 0)
    return pl.pallas_call(
        functools.partial(_oproj_kernel, alpha=alpha, rc=256),
        grid=(t // tm,),
        in_specs=[pl.BlockSpec((tm, d), lambda i: (i, 0)),
                  pl.BlockSpec((tm, d), lambda i: (i, 0)),
                  pl.BlockSpec((1, 1, d), lambda i: (i // per_b, 0, 2)),
                  _resident((d, d), row),
                  pl.BlockSpec((1, d), row), pl.BlockSpec((1, d), row), pl.BlockSpec((1, d), row)],
        out_specs=pl.BlockSpec((tm, d), lambda i: (i, 0)),
        out_shape=jax.ShapeDtypeStruct((t, d), F32),
        compiler_params=_cparams(("parallel",)),
        name="out_proj_ln",
    )(m, xt, mod3, w_o, b_o.reshape(1, d), ln_g.reshape(1, d), ln_b.reshape(1, d))


def _mlp_kernel(x_ref, sc_ref, sh_ref, g2_ref, w1_ref, b1_ref, w2_ref, b2_ref, lg_ref, lb_ref,
                o_ref, h_scr, acc_scr, *, alpha, rc):
    j = pl.program_id(1)
    last = pl.num_programs(1) - 1
    chunks = _row_chunks(x_ref.shape[0], rc)

    def partial_out(rows):
        hid = jnp.dot(h_scr[rows, :], w1_ref[...], preferred_element_type=F32) + b1_ref[...]
        hid = jnp.square(jnp.maximum(hid, 0.0)).astype(BF16)
        return jnp.dot(hid, w2_ref[...], preferred_element_type=F32)

    @pl.when(j == 0)
    def _():
        for rows in chunks:
            h_scr[rows, :] = (x_ref[rows, :] * (1.0 + sc_ref[0]) + sh_ref[0]).astype(BF16)
            acc_scr[rows, :] = partial_out(rows)

    @pl.when(jnp.logical_and(j > 0, j < last))
    def _():
        acc_scr[...] += partial_out(slice(None))

    @pl.when(j == last)
    def _():
        for rows in chunks:
            f = acc_scr[rows, :] + partial_out(rows) + b2_ref[...]
            o_ref[rows, :] = _layer_norm(alpha * x_ref[rows, :] + g2_ref[0] * f,
                                         lg_ref[...], lb_ref[...])


def _mlp_call(x1, mod3, w_m1, b_m1, w_m2, b_m2, ln_g, ln_b, seq, alpha):
    t, d = x1.shape
    dff = w_m1.shape[1]
    tm = 512
    tf = 1024
    per_b = seq // tm
    row = lambda i, j: (0, 0)
    return pl.pallas_call(
        functools.partial(_mlp_kernel, alpha=alpha, rc=256),
        grid=(t // tm, dff // tf),
        in_specs=[pl.BlockSpec((tm, d), lambda i, j: (i, 0)),
                  pl.BlockSpec((1, 1, d), lambda i, j: (i // per_b, 0, 4)),
                  pl.BlockSpec((1, 1, d), lambda i, j: (i // per_b, 0, 3)),
                  pl.BlockSpec((1, 1, d), lambda i, j: (i // per_b, 0, 5)),
                  pl.BlockSpec((d, tf), lambda i, j: (0, j)),
                  pl.BlockSpec((1, tf), lambda i, j: (0, j)),
                  pl.BlockSpec((tf, d), lambda i, j: (j, 0)),
                  pl.BlockSpec((1, d), row), pl.BlockSpec((1, d), row), pl.BlockSpec((1, d), row)],
        out_specs=pl.BlockSpec((tm, d), lambda i, j: (i, 0)),
        out_shape=jax.ShapeDtypeStruct((t, d), F32),
        scratch_shapes=[pltpu.VMEM((tm, d), BF16), pltpu.VMEM((tm, d), F32)],
        compiler_params=_cparams(("parallel", "arbitrary")),
        name="mlp_ln",
    )(x1, mod3, mod3, mod3, w_m1, b_m1.reshape(1, dff), w_m2, b_m2.reshape(1, d),
      ln_g.reshape(1, d), ln_b.reshape(1, d))


def _filter_features(seq):
    bands_n = (HY_EMB - 1) // 2
    t = jnp.linspace(0.0, 1.0, seq, dtype=F32)[:, None]
    omega = 2.0 * math.pi * jnp.arange(seq, dtype=F32)[:, None] / seq
    bands = jnp.linspace(1e-4, bands_n - 1, bands_n, dtype=F32)[None, :]
    feats = jnp.concatenate([t, jnp.cos(bands * omega), -jnp.sin(bands * omega)], axis=-1)
    return t, feats


def _decay_rates(width):
    min_decay = math.log(HY_DECAY_TARGET) / HY_SLOW_DECAY
    max_decay = math.log(HY_DECAY_TARGET) / HY_FAST_DECAY
    return jnp.abs(jnp.linspace(min_decay, max_decay, width, dtype=F32))[None, :]


def kernel(x, c, w_ada, b_ada, w_in, b_in, sgu_ln_g, sgu_ln_b, sgu_w, sgu_b, hy_conv_w, hy_conv_b, hy_w1, hy_b1, hy_w2, hy_b2, hy_freq, hy_w3, hy_skip, w_branch_a, w_branch_b, w_o, b_o, ln1_g, ln1_b, w_m1, b_m1, w_m2, b_m2, ln2_g, ln2_b):
    bsz, seq, d = x.shape
    depth = w_ada.shape[0]
    alpha = (2.0 * depth) ** 0.25
    sgu_width = w_branch_a.shape[1]
    hy_width = w_branch_b.shape[1]
    assert seq % SGU_CHUNK == 0 and sgu_w.shape[-1] == SGU_CHUNK

    assert seq % (8 * MXU_DIM) == 0
    (t_s1, t_real, t_cplx, t_gauss), inv_tables = _dft_stage_tables(seq)
    tcol, feats = _filter_features(seq)
    deltas = _decay_rates(hy_width)

    xt = x.reshape(bsz * seq, d)
    for l in range(depth):
        mod3 = _mod_call(c, w_ada[l], b_ada[l]).reshape(bsz, 1, N_MOD * d)

        h2 = _filter_mlp_call(feats, hy_w1[l], hy_b1[l], hy_w2[l], hy_b2[l], hy_freq[l])
        kspec, w_in_16 = _filter_spec_call(h2, hy_w3[l], tcol, deltas, hy_skip[l],
                                           t_s1, t_real, t_cplx, w_in[l], hy_width)

        z = _inproj_call(xt, mod3, w_in_16, b_in[l], sgu_ln_g[l], sgu_ln_b[l],
                         seq, sgu_width, hy_width)
        yh, w_m1_16, w_m2_16 = _hyena_call(z, hy_conv_w[l], hy_conv_b[l], kspec,
                                           (t_s1, t_real, t_gauss) + inv_tables, (w_m1[l], w_m2[l]),
                                           bsz, seq, hy_width, 2 * d + 2 * sgu_width)

        gd = sgu_width // SGU_GROUPS
        bs_full = jnp.repeat(sgu_b[l].T, gd, axis=1)
        m = _merge_call(z, yh, sgu_w[l].astype(BF16), bs_full, w_branch_a[l], w_branch_b[l], d)

        x1 = _oproj_call(m, xt, mod3, w_o[l], b_o[l], ln1_g[l], ln1_b[l], seq, alpha)
        xt = _mlp_call(x1, mod3, w_m1_16, b_m1[l], w_m2_16, b_m2[l], ln2_g[l], ln2_b[l], seq, alpha)
    return xt.reshape(bsz, seq, d)
```

```python
import functools
import math

import jax
import jax.numpy as jnp
import numpy as np
from jax import lax
from jax.experimental import pallas as pl
from jax.experimental.pallas import tpu as pltpu

F32 = jnp.float32
BF16 = jnp.bfloat16

SGU_CHUNK = 128
SGU_GROUPS = 8
HY_ORDER = 2
HY_SHORT = 3
HY_EMB = 33
HY_DECAY_TARGET = 1e-2
HY_FAST_DECAY = 0.3
HY_SLOW_DECAY = 1.5
N_MOD = 6
LN_EPS = 1e-5

LANE = 128
SUBLANES = 8
MXU_DIM = 256
VMEM_LIMIT = 48 * 1024 * 1024


def _cparams(sem):
    return pltpu.CompilerParams(dimension_semantics=sem, vmem_limit_bytes=VMEM_LIMIT)


def _resident(block_shape, index_map):
    return pl.BlockSpec(block_shape, index_map, pipeline_mode=pl.Buffered(1))


def _row_chunks(total, size):
    return [slice(r, r + size) for r in range(0, total, size)]


def _gelu(x):
    return 0.5 * x * (1.0 + lax.erf(x * (1.0 / math.sqrt(2.0))))


def _layer_norm(x, g, b):
    mu = jnp.mean(x, axis=-1, keepdims=True)
    xc = x - mu
    var = jnp.mean(xc * xc, axis=-1, keepdims=True)
    return xc * lax.rsqrt(var + LN_EPS) * g + b


def _dot_bf16x3(a, b):
    m = a.shape[0]
    a_hi = a.astype(BF16)
    b_hi = b.astype(BF16)
    a_hi32 = a_hi.astype(F32)
    b_lo = (b - b_hi.astype(F32)).astype(BF16)
    stacked = jnp.concatenate([a_hi32, a - a_hi32], axis=0).astype(BF16)
    top = jnp.dot(stacked, b_hi, preferred_element_type=F32)
    return top[:m] + top[m:] + jnp.dot(a_hi, b_lo, preferred_element_type=F32)


def _mod_kernel(c_ref, w_ref, b_ref, o_ref):
    c = c_ref[...]
    cond = c * jax.nn.sigmoid(c)
    o_ref[...] = _dot_bf16x3(cond, w_ref[...]) + b_ref[...]


def _mod_call(c, w_ada, b_ada):
    bsz, d = c.shape
    n = w_ada.shape[1]
    tn = 1024
    return pl.pallas_call(
        _mod_kernel,
        grid=(n // tn,),
        in_specs=[pl.BlockSpec((bsz, d), lambda j: (0, 0)),
                  pl.BlockSpec((d, tn), lambda j: (0, j)),
                  pl.BlockSpec((1, tn), lambda j: (0, j))],
        out_specs=pl.BlockSpec((bsz, tn), lambda j: (0, j)),
        out_shape=jax.ShapeDtypeStruct((bsz, n), F32),
        compiler_params=_cparams(("arbitrary",)),
        name="adaln_mod",
    )(c, w_ada, b_ada.reshape(1, n))


def _inproj_kernel(x_ref, sc_ref, sh_ref, w_ref, b_ref, g_ref, be_ref, o_ref, h_scr, *, seg, rc):
    j = pl.program_id(1)
    chunks = _row_chunks(x_ref.shape[0], rc)

    def emit(act, modulate=False):
        for rows in chunks:
            if modulate:
                h_scr[rows, :] = (x_ref[rows, :] * (1.0 + sc_ref[0]) + sh_ref[0]).astype(BF16)
            acc = jnp.dot(h_scr[rows, :], w_ref[...], preferred_element_type=F32) + b_ref[...]
            o_ref[rows, :] = act(acc).astype(o_ref.dtype)

    @pl.when(j == 0)
    def _():
        emit(jax.nn.sigmoid, modulate=True)

    @pl.when(jnp.logical_and(j > 0, j < seg[0]))
    def _():
        emit(jax.nn.sigmoid)

    @pl.when(jnp.logical_and(j >= seg[0], j < seg[1]))
    def _():
        emit(_gelu)

    @pl.when(jnp.logical_and(j >= seg[1], j < seg[2]))
    def _():
        emit(lambda a: _layer_norm(_gelu(a), g_ref[...], be_ref[...]))

    @pl.when(j >= seg[2])
    def _():
        emit(lambda a: a)


def _inproj_call(xt, mod3, w_in, b_in, ln_g, ln_b, seq, sgu_w, hy_w):
    t, d = xt.shape
    n = w_in.shape[1]
    tm = 1024
    tn = sgu_w
    assert hy_w == sgu_w and seq % tm == 0 and d % tn == 0
    nblk = n // tn
    n_gate = 2 * d // tn
    seg = (n_gate, n_gate + 1, n_gate + 2)
    per_b = seq // tm
    wcol = lambda i, j: (0, (j + (nblk - n_gate)) % nblk)
    return pl.pallas_call(
        functools.partial(_inproj_kernel, seg=seg, rc=256),
        grid=(t // tm, n // tn),
        in_specs=[pl.BlockSpec((tm, d), lambda i, j: (i, 0)),
                  pl.BlockSpec((1, 1, d), lambda i, j: (i // per_b, 0, 1)),
                  pl.BlockSpec((1, 1, d), lambda i, j: (i // per_b, 0, 0)),
                  pl.BlockSpec((d, tn), wcol),
                  pl.BlockSpec((1, tn), wcol),
                  pl.BlockSpec((1, tn), lambda i, j: (0, 0)),
                  pl.BlockSpec((1, tn), lambda i, j: (0, 0))],
        out_specs=pl.BlockSpec((tm, tn), lambda i, j: (i, j)),
        out_shape=jax.ShapeDtypeStruct((t, n), BF16),
        scratch_shapes=[pltpu.VMEM((tm, d), BF16)],
        compiler_params=_cparams(("parallel", "arbitrary")),
        name="in_proj",
    )(xt, mod3, mod3, w_in, b_in.reshape(1, n), ln_g.reshape(1, tn), ln_b.reshape(1, tn))


def _filter_mlp_kernel(f_ref, w1_ref, b1_ref, w2_ref, b2_ref, fr_ref, o_ref):
    hp = lax.Precision.HIGHEST
    fr = fr_ref[...]
    h = jnp.sin(fr * (jnp.dot(f_ref[...], w1_ref[...], preferred_element_type=F32, precision=hp)
                      + b1_ref[...]))
    o_ref[...] = jnp.sin(fr * (jnp.dot(h, w2_ref[...], preferred_element_type=F32, precision=hp)
                               + b2_ref[...]))


def _filter_mlp_call(feats, w1, b1, w2, b2, freq):
    seq = feats.shape[0]
    hid = w1.shape[1]
    return pl.pallas_call(
        _filter_mlp_kernel,
        out_shape=jax.ShapeDtypeStruct((seq, hid), F32),
        name="hyena_filter_mlp",
    )(feats, w1, b1.reshape(1, hid), w2, b2.reshape(1, hid), freq.reshape(1, hid))


DFT_GROUPS = 9
DFT_SLABS = 8
DFT_COMPS = 16
DFT_J = 32


def _packed_start(g, n2):
    return {0: 0, 8: n2}.get(g, 2 * n2 * g)


def _store_slab_block_order(dst_ref, val, t1, n2):
    step = DFT_SLABS * DFT_J
    for k in range(n2 // DFT_J):
        dst = step * k + DFT_J * t1
        dst_ref[dst:dst + DFT_J, :] = val[DFT_J * k:DFT_J * (k + 1), :].astype(dst_ref.dtype)


def _store_block_order(dst_ref, val, n2):
    for t1 in range(DFT_SLABS):
        _store_slab_block_order(dst_ref, val[n2 * t1:n2 * (t1 + 1), :], t1, n2)


def _stage1_fwd(s1_ref, src_ref, dst_ref, n2):
    step = DFT_SLABS * DFT_J
    for k in range(n2 // DFT_J):
        comps = jnp.dot(s1_ref[...], src_ref[step * k:step * (k + 1), :], preferred_element_type=F32)
        for c in range(DFT_COMPS):
            dst_ref[n2 * c + DFT_J * k:n2 * c + DFT_J * (k + 1), :] = (
                comps[DFT_J * c:DFT_J * (c + 1), :].astype(dst_ref.dtype))


def _stage2_fwd_part(g, part, tr_ref, tc_ref, src_ref, n2):
    start = _packed_start(g, n2)
    rows = slice(n2 * part, n2 * (part + 1))
    if g in (0, 8):
        return jnp.dot(tr_ref[0 if g == 0 else 1, rows, :], src_ref[start:start + n2, :],
                       preferred_element_type=F32)
    return jnp.dot(tc_ref[g - 1, rows, :], src_ref[start:start + 2 * n2, :],
                   preferred_element_type=F32)


def _dft_stage_tables(seq):
    n = 2 * seq
    n2 = n // 16
    f2 = np.arange(n2)[:, None]
    t2 = np.arange(n2)[None, :]

    def cs(g):
        ang = ((t2 * (g + 16 * f2)) % n) * (2.0 * math.pi / n)
        return np.cos(ang), np.sin(ang)

    real = np.stack([np.concatenate([c, -s], axis=0) for c, s in (cs(0), cs(8))])
    cplx = np.stack([np.block([[c, s], [-s, c]]) for c, s in (cs(g) for g in range(1, 8))])
    gauss = np.stack([np.stack([c, s - c, c + s]) for c, s in (cs(g) for g in range(1, 8))])

    t1 = np.arange(DFT_SLABS)
    rows = [np.ones(DFT_SLABS), np.where(t1 % 2 == 0, 1.0, -1.0)]
    for g in range(1, 8):
        ang = ((g * t1) % 16) * (math.pi / 8.0)
        rows += [np.cos(ang), -np.sin(ang)]
    s1 = np.kron(np.stack(rows), np.eye(DFT_J))
    fwd = (s1, real, cplx, gauss)
    inv = (s1.T, np.swapaxes(real, 1, 2), np.swapaxes(gauss, 2, 3))
    as_bf16 = lambda t: jnp.asarray(np.ascontiguousarray(t, dtype=np.float32)).astype(BF16)
    return tuple(map(as_bf16, fwd)), tuple(map(as_bf16, inv))


def _filter_spec_kernel(h_ref, wf_ref, wb_ref, t_ref, dl_ref, skip_ref, s1_ref, tr_ref, tc_ref,
                        cast_ref, k_ref, cast_out, e_scr, o_scr, ge_scr, go_scr):
    seq = h_ref.shape[0]
    n2 = seq // 8
    cast_out[...] = cast_ref[...].astype(cast_out.dtype)
    window = jnp.exp(-t_ref[...] * dl_ref[...])
    _store_block_order(e_scr, _dot_bf16x3(h_ref[...], wf_ref[...] + wb_ref[...]) * window, n2)
    _store_block_order(o_scr, _dot_bf16x3(h_ref[...], wf_ref[...] - wb_ref[...]) * window, n2)
    _stage1_fwd(s1_ref, e_scr, ge_scr, n2)
    _stage1_fwd(s1_ref, o_scr, go_scr, n2)
    for g in range(DFT_GROUPS):
        wgt = (1.0 if g in (0, 8) else 2.0) / (2 * seq)
        rows = slice(2 * n2 * g, 2 * n2 * g + n2)
        rows_im = slice(2 * n2 * g + n2, 2 * n2 * (g + 1))
        k_ref[0, rows, :] = ((_stage2_fwd_part(g, 0, tr_ref, tc_ref, ge_scr, n2) + skip_ref[0])
                             * wgt).astype(k_ref.dtype)
        k_ref[0, rows_im, :] = (_stage2_fwd_part(g, 1, tr_ref, tc_ref, go_scr, n2)
                                * wgt).astype(k_ref.dtype)


def _filter_spec_call(h2, w3, tcol, deltas, skip, t_s1, t_real, t_cplx, to_cast, width):
    seq, hid = h2.shape
    n2 = seq // 8
    cb = MXU_DIM
    ncb = width // cb
    per_dir = HY_ORDER * ncb
    rows = DFT_GROUPS * 2 * n2
    const = lambda n, k: (0, 0, 0)
    steps = HY_ORDER * ncb
    assert to_cast.shape[0] % (steps * 2 * SUBLANES) == 0
    slab = pl.BlockSpec((to_cast.shape[0] // steps, to_cast.shape[1]), lambda n, k: (n * ncb + k, 0))
    return pl.pallas_call(
        _filter_spec_kernel,
        grid=(HY_ORDER, ncb),
        in_specs=[_resident((seq, hid), lambda n, k: (0, 0)),
                  pl.BlockSpec((hid, cb), lambda n, k: (0, n * ncb + k)),
                  pl.BlockSpec((hid, cb), lambda n, k: (0, per_dir + n * ncb + k)),
                  _resident((seq, 1), lambda n, k: (0, 0)),
                  pl.BlockSpec((1, cb), lambda n, k: (0, k)),
                  pl.BlockSpec((1, 1, cb), lambda n, k: (n, 0, k)),
                  _resident(t_s1.shape, lambda n, k: (0, 0)),
                  _resident(t_real.shape, const), _resident(t_cplx.shape, const), slab],
        out_specs=[pl.BlockSpec((1, rows, cb), lambda n, k: (n, 0, k)), slab],
        out_shape=[jax.ShapeDtypeStruct((HY_ORDER, rows, width), BF16),
                   jax.ShapeDtypeStruct(to_cast.shape, BF16)],
        scratch_shapes=[pltpu.VMEM((seq, cb), BF16), pltpu.VMEM((seq, cb), BF16),
                        pltpu.VMEM((2 * seq, cb), BF16), pltpu.VMEM((2 * seq, cb), BF16)],
        compiler_params=_cparams(("arbitrary", "arbitrary")),
        name="hyena_filter_spectrum",
    )(h2, w3, w3, tcol, deltas, skip.reshape(HY_ORDER, 1, width), t_s1, t_real, t_cplx, to_cast)


def _short_conv(z, w, b):
    seq = z.shape[0]
    prev = pltpu.roll(z, 1, 0)
    nxt = pltpu.roll(z, seq - 1, 0)
    row = lax.broadcasted_iota(jnp.int32, (SUBLANES, z.shape[1]), 0)
    prev = jnp.concatenate([jnp.where(row == 0, 0.0, prev[:SUBLANES]), prev[SUBLANES:]], axis=0)
    nxt = jnp.concatenate([nxt[:-SUBLANES], jnp.where(row == SUBLANES - 1, 0.0, nxt[-SUBLANES:])],
                          axis=0)
    return w[0:1] * prev + w[1:2] * z + w[2:3] * nxt + b


def _hyena_kernel(v_ref, x1_ref, x2_ref, wv_ref, w1_ref, w2_ref, bv_ref, b1_ref, b2_ref,
                  k_ref, s1_ref, tr_ref, tg_ref, s1t_ref, ur_ref, ug_ref, cast1_ref, cast2_ref,
                  o_ref, cast1_out, cast2_out, y16_scr, gate_scr, g_scr, yc_scr, h_scr):
    seq, cb = v_ref.shape
    n2 = seq // DFT_SLABS
    step = DFT_SLABS * DFT_J
    dot = functools.partial(jnp.dot, preferred_element_type=F32)
    cast1_out[...] = cast1_ref[...].astype(cast1_out.dtype)
    cast2_out[...] = cast2_ref[...].astype(cast2_out.dtype)

    conv3 = lambda z_ref, w_ref, b_ref: _short_conv(z_ref[...].astype(F32), w_ref[...], b_ref[...])
    _store_block_order(y16_scr, conv3(v_ref, wv_ref, bv_ref), n2)
    _store_block_order(gate_scr.at[0], conv3(x1_ref, w1_ref, b1_ref), n2)
    _store_block_order(gate_scr.at[1], conv3(x2_ref, w2_ref, b2_ref), n2)

    for n in range(HY_ORDER):
        last = n + 1 == HY_ORDER
        _stage1_fwd(s1_ref, y16_scr, g_scr, n2)
        for g in range(DFT_GROUPS):
            start = _packed_start(g, n2)
            if g in (0, 8):
                x = dot(tr_ref[0 if g == 0 else 1], g_scr[start:start + n2, :])
                xr, xi = x[:n2], x[n2:]
            else:
                a = g_scr[start:start + n2, :]
                b = g_scr[start + n2:start + 2 * n2, :]
                p1 = dot(tg_ref[g - 1, 0], a + b)
                xr = p1 + dot(tg_ref[g - 1, 1], b)
                xi = p1 - dot(tg_ref[g - 1, 2], a)
            xr, xi = xr.astype(BF16), xi.astype(BF16)
            base = 2 * n2 * g
            kr = k_ref[n, base:base + n2, :]
            ki = k_ref[n, base + n2:base + 2 * n2, :]
            yc_scr[base:base + n2, :] = xr * kr - xi * ki
            yc_scr[base + n2:base + 2 * n2, :] = xr * ki + xi * kr
        for g in range(DFT_GROUPS):
            base = 2 * n2 * g
            if g in (0, 8):
                parts = [(0 if g == 0 else 1,
                          dot(ur_ref[0 if g == 0 else 1], yc_scr[base:base + 2 * n2, :]))]
            else:
                a = yc_scr[base:base + n2, :]
                b = yc_scr[base + n2:base + 2 * n2, :]
                q1 = dot(ug_ref[g - 1, 0], a + b)
                parts = [(2 * g, q1 - dot(ug_ref[g - 1, 2], b)),
                         (2 * g + 1, q1 + dot(ug_ref[g - 1, 1], a))]
            for c, hp in parts:
                for k in range(n2 // DFT_J):
                    dst = DFT_COMPS * DFT_J * k + DFT_J * c
                    h_scr[dst:dst + DFT_J, :] = hp[DFT_J * k:DFT_J * (k + 1), :].astype(BF16)
        for k in range(n2 // DFT_J):
            rows = slice(step * k, step * (k + 1))
            conv = jnp.dot(s1t_ref[...], h_scr[DFT_COMPS * DFT_J * k:DFT_COMPS * DFT_J * (k + 1), :],
                           preferred_element_type=F32)
            y_new = (gate_scr[n, rows, :] * conv).astype(BF16)
            if last:
                for t1 in range(DFT_SLABS):
                    dst = n2 * t1 + DFT_J * k
                    o_ref[dst:dst + DFT_J, :] = y_new[DFT_J * t1:DFT_J * (t1 + 1), :]
            else:
                y16_scr[rows, :] = y_new


def _hyena_call(z, conv_w, conv_b, kspec, tables, to_cast, bsz, seq, width, col0):
    cb = MXU_DIM
    ncb = width // cb
    base = col0 // cb
    zspec = lambda part: pl.BlockSpec((seq, cb), lambda k, b: (b, base + part * ncb + k))
    wspec = lambda part: pl.BlockSpec((HY_SHORT, cb), lambda k, b: (0, part * ncb + k))
    bspec = lambda part: pl.BlockSpec((1, cb), lambda k, b: (0, part * ncb + k))
    cb_all = conv_b.reshape(1, -1)
    const = lambda t: (lambda k, b: (0,) * t.ndim)
    steps = ncb * bsz
    slab = lambda w: pl.BlockSpec((w.shape[0] // steps, w.shape[1]), lambda k, b: (k * bsz + b, 0))
    assert all(w.shape[0] % (steps * 2 * SUBLANES) == 0 for w in to_cast)
    return pl.pallas_call(
        _hyena_kernel,
        grid=(ncb, bsz),
        in_specs=[zspec(0), zspec(1), zspec(2), wspec(0), wspec(1), wspec(2),
                  bspec(0), bspec(1), bspec(2),
                  _resident((HY_ORDER, kspec.shape[1], cb), lambda k, b: (0, 0, k))]
                 + [_resident(t.shape, const(t)) for t in tables] + [slab(w) for w in to_cast],
        out_specs=[pl.BlockSpec((seq, cb), lambda k, b: (b, k))] + [slab(w) for w in to_cast],
        out_shape=[jax.ShapeDtypeStruct((bsz * seq, width), BF16)]
                  + [jax.ShapeDtypeStruct(w.shape, BF16) for w in to_cast],
        scratch_shapes=[pltpu.VMEM((seq, cb), BF16),
                        pltpu.VMEM((HY_ORDER, seq, cb), F32),
                        pltpu.VMEM((2 * seq, cb), BF16),
                        pltpu.VMEM((kspec.shape[1], cb), BF16),
                        pltpu.VMEM((2 * seq, cb), BF16)],
        compiler_params=_cparams(("arbitrary", "arbitrary")),
        name="hyena_mixer",
    )(z, z, z, conv_w, conv_w, conv_w, cb_all, cb_all, cb_all, kspec, *tables, *to_cast)


def _merge_kernel(u_ref, vn_ref, yh_ref, ga_ref, gb_ref, ws_ref, bs_ref, wa_ref, wb_ref, o_ref,
                  s_scr):
    tm = u_ref.shape[0]
    gd = u_ref.shape[1] // SGU_GROUPS
    for n in range(tm // SGU_CHUNK):
        rows = slice(n * SGU_CHUNK, (n + 1) * SGU_CHUNK)
        for g in range(SGU_GROUPS):
            cols = slice(g * gd, (g + 1) * gd)
            mixed = jnp.dot(ws_ref[g], vn_ref[rows, cols], preferred_element_type=F32)
            s_scr[rows, cols] = (u_ref[rows, cols].astype(F32)
                                 * (mixed + bs_ref[:, cols])).astype(BF16)
    y_a = jnp.dot(s_scr[...], wa_ref[...].astype(BF16), preferred_element_type=F32)
    y_b = jnp.dot(yh_ref[...], wb_ref[...].astype(BF16), preferred_element_type=F32)
    o_ref[...] = (ga_ref[...].astype(F32) * y_a + gb_ref[...].astype(F32) * y_b).astype(o_ref.dtype)


def _merge_call(z, yh, sgu_w16, sgu_bfull, w_a, w_b, d):
    t = z.shape[0]
    wdt = w_a.shape[0]
    tm = 512
    u_blk = 2 * d // wdt
    return pl.pallas_call(
        _merge_kernel,
        grid=(t // tm,),
        in_specs=[pl.BlockSpec((tm, wdt), lambda i: (i, u_blk)),
                  pl.BlockSpec((tm, wdt), lambda i: (i, u_blk + 1)),
                  pl.BlockSpec((tm, wdt), lambda i: (i, 0)),
                  pl.BlockSpec((tm, d), lambda i: (i, 0)),
                  pl.BlockSpec((tm, d), lambda i: (i, 1)),
                  _resident(sgu_w16.shape, lambda i: (0, 0, 0)),
                  _resident(sgu_bfull.shape, lambda i: (0, 0)),
                  _resident(w_a.shape, lambda i: (0, 0)),
                  _resident(w_b.shape, lambda i: (0, 0))],
        out_specs=pl.BlockSpec((tm, d), lambda i: (i, 0)),
        out_shape=jax.ShapeDtypeStruct((t, d), BF16),
        scratch_shapes=[pltpu.VMEM((tm, wdt), BF16)],
        compiler_params=_cparams(("parallel",)),
        name="gated_merge",
    )(z, z, yh, z, z, sgu_w16, sgu_bfull, w_a, w_b)


def _oproj_kernel(m_ref, x_ref, g1_ref, w_ref, b_ref, lg_ref, lb_ref, o_ref, *, alpha, rc):
    for rows in _row_chunks(m_ref.shape[0], rc):
        mix = jnp.dot(m_ref[rows, :], w_ref[...].astype(BF16),
                      preferred_element_type=F32) + b_ref[...]
        o_ref[rows, :] = _layer_norm(alpha * x_ref[rows, :] + g1_ref[0] * mix,
                                     lg_ref[...], lb_ref[...])


def _oproj_call(m, xt, mod3, w_o, b_o, ln_g, ln_b, seq, alpha):
    t, d = xt.shape
    tm = 512
    per_b = seq // tm
    row = lambda i: (0, 0)
    return pl.pallas_call(
        functools.partial(_oproj_kernel, alpha=alpha, rc=256),
        grid=(t // tm,),
        in_specs=[pl.BlockSpec((tm, d), lambda i: (i, 0)),
                  pl.BlockSpec((tm, d), lambda i: (i, 0)),
                  pl.BlockSpec((1, 1, d), lambda i: (i // per_b, 0, 2)),
                  _resident((d, d), row),
                  pl.BlockSpec((1, d), row), pl.BlockSpec((1, d), row), pl.BlockSpec((1, d), row)],
        out_specs=pl.BlockSpec((tm, d), lambda i: (i, 0)),
        out_shape=jax.ShapeDtypeStruct((t, d), F32),
        compiler_params=_cparams(("parallel",)),
        name="out_proj_ln",
    )(m, xt, mod3, w_o, b_o.reshape(1, d), ln_g.reshape(1, d), ln_b.reshape(1, d))


def _mlp_kernel(x_ref, sc_ref, sh_ref, g2_ref, w1_ref, b1_ref, w2_ref, b2_ref, lg_ref, lb_ref,
                o_ref, h_scr, acc_scr, *, alpha, rc):
    j = pl.program_id(1)
    last = pl.num_programs(1) - 1
    chunks = _row_chunks(x_ref.shape[0], rc)

    def partial_out(rows):
        hid = jnp.dot(h_scr[rows, :], w1_ref[...], preferred_element_type=F32) + b1_ref[...]
        hid = jnp.square(jnp.maximum(hid, 0.0)).astype(BF16)
        return jnp.dot(hid, w2_ref[...], preferred_element_type=F32)

    @pl.when(j == 0)
    def _():
        for rows in chunks:
            h_scr[rows, :] = (x_ref[rows, :] * (1.0 + sc_ref[0]) + sh_ref[0]).astype(BF16)
            acc_scr[rows, :] = partial_out(rows)

    @pl.when(jnp.logical_and(j > 0, j < last))
    def _():
        acc_scr[...] += partial_out(slice(None))

    @pl.when(j == last)
    def _():
        for rows in chunks:
            f = acc_scr[rows, :] + partial_out(rows) + b2_ref[...]
            o_ref[rows, :] = _layer_norm(alpha * x_ref[rows, :] + g2_ref[0] * f,
                                         lg_ref[...], lb_ref[...])


def _mlp_call(x1, mod3, w_m1, b_m1, w_m2, b_m2, ln_g, ln_b, seq, alpha):
    t, d = x1.shape
    dff = w_m1.shape[1]
    tm = 512
    tf = 1024
    per_b = seq // tm
    row = lambda i, j: (0, 0)
    return pl.pallas_call(
        functools.partial(_mlp_kernel, alpha=alpha, rc=256),
        grid=(t // tm, dff // tf),
        in_specs=[pl.BlockSpec((tm, d), lambda i, j: (i, 0)),
                  pl.BlockSpec((1, 1, d), lambda i, j: (i // per_b, 0, 4)),
                  pl.BlockSpec((1, 1, d), lambda i, j: (i // per_b, 0, 3)),
                  pl.BlockSpec((1, 1, d), lambda i, j: (i // per_b, 0, 5)),
                  pl.BlockSpec((d, tf), lambda i, j: (0, j)),
                  pl.BlockSpec((1, tf), lambda i, j: (0, j)),
                  pl.BlockSpec((tf, d), lambda i, j: (j, 0)),
                  pl.BlockSpec((1, d), row), pl.BlockSpec((1, d), row), pl.BlockSpec((1, d), row)],
        out_specs=pl.BlockSpec((tm, d), lambda i, j: (i, 0)),
        out_shape=jax.ShapeDtypeStruct((t, d), F32),
        scratch_shapes=[pltpu.VMEM((tm, d), BF16), pltpu.VMEM((tm, d), F32)],
        compiler_params=_cparams(("parallel", "arbitrary")),
        name="mlp_ln",
    )(x1, mod3, mod3, mod3, w_m1, b_m1.reshape(1, dff), w_m2, b_m2.reshape(1, d),
      ln_g.reshape(1, d), ln_b.reshape(1, d))


def _filter_features(seq):
    bands_n = (HY_EMB - 1) // 2
    t = np.linspace(0.0, 1.0, seq)[:, None]
    omega = 2.0 * math.pi * np.arange(seq)[:, None] / seq
    bands = np.linspace(1e-4, bands_n - 1, bands_n)[None, :]
    feats = np.concatenate([t, np.cos(bands * omega), -np.sin(bands * omega)], axis=-1)
    return jnp.asarray(t, F32), jnp.asarray(feats, F32)


def _decay_rates(width):
    min_decay = math.log(HY_DECAY_TARGET) / HY_SLOW_DECAY
    max_decay = math.log(HY_DECAY_TARGET) / HY_FAST_DECAY
    return jnp.asarray(np.abs(np.linspace(min_decay, max_decay, width))[None, :], F32)


def kernel(x, c, w_ada, b_ada, w_in, b_in, sgu_ln_g, sgu_ln_b, sgu_w, sgu_b, hy_conv_w, hy_conv_b, hy_w1, hy_b1, hy_w2, hy_b2, hy_freq, hy_w3, hy_skip, w_branch_a, w_branch_b, w_o, b_o, ln1_g, ln1_b, w_m1, b_m1, w_m2, b_m2, ln2_g, ln2_b):
    bsz, seq, d = x.shape
    depth = w_ada.shape[0]
    alpha = (2.0 * depth) ** 0.25
    sgu_width = w_branch_a.shape[1]
    hy_width = w_branch_b.shape[1]
    assert seq % SGU_CHUNK == 0 and sgu_w.shape[-1] == SGU_CHUNK

    assert seq % (8 * MXU_DIM) == 0
    (t_s1, t_real, t_cplx, t_gauss), inv_tables = _dft_stage_tables(seq)
    tcol, feats = _filter_features(seq)
    deltas = _decay_rates(hy_width)

    xt = x.reshape(bsz * seq, d)
    for l in range(depth):
        mod3 = _mod_call(c, w_ada[l], b_ada[l]).reshape(bsz, 1, N_MOD * d)

        h2 = _filter_mlp_call(feats, hy_w1[l], hy_b1[l], hy_w2[l], hy_b2[l], hy_freq[l])
        kspec, w_in_16 = _filter_spec_call(h2, hy_w3[l], tcol, deltas, hy_skip[l],
                                           t_s1, t_real, t_cplx, w_in[l], hy_width)

        z = _inproj_call(xt, mod3, w_in_16, b_in[l], sgu_ln_g[l], sgu_ln_b[l],
                         seq, sgu_width, hy_width)
        yh, w_m1_16, w_m2_16 = _hyena_call(z, hy_conv_w[l], hy_conv_b[l], kspec,
                                           (t_s1, t_real, t_gauss) + inv_tables, (w_m1[l], w_m2[l]),
                                           bsz, seq, hy_width, 2 * d + 2 * sgu_width)

        gd = sgu_width // SGU_GROUPS
        bs_full = jnp.repeat(sgu_b[l].T, gd, axis=1)
        m = _merge_call(z, yh, sgu_w[l].astype(BF16), bs_full, w_branch_a[l], w_branch_b[l], d)

        x1 = _oproj_call(m, xt, mod3, w_o[l], b_o[l], ln1_g[l], ln1_b[l], seq, alpha)
        xt = _mlp_call(x1, mod3, w_m1_16, b_m1[l], w_m2_16, b_m2[l], ln2_g[l], ln2_b[l], seq, alpha)
    return xt.reshape(bsz, seq, d)
```

```python
import functools
import math

import jax
import jax.numpy as jnp
import numpy as np
from jax import lax
from jax.experimental import pallas as pl
from jax.experimental.pallas import tpu as pltpu

F32 = jnp.float32
BF16 = jnp.bfloat16

SGU_CHUNK = 128
SGU_GROUPS = 8
HY_ORDER = 2
HY_SHORT = 3
HY_EMB = 33
HY_DECAY_TARGET = 1e-2
HY_FAST_DECAY = 0.3
HY_SLOW_DECAY = 1.5
N_MOD = 6
LN_EPS = 1e-5

LANE = 128
SUBLANES = 8
MXU_DIM = 256
VMEM_LIMIT = 48 * 1024 * 1024

MOD_COLS = 1024
INPROJ_ROWS = 1024
TOKEN_ROWS = 512
MLP_HIDDEN = 1024
EPILOGUE_ROWS = 256


def _cparams(sem):
    return pltpu.CompilerParams(dimension_semantics=sem, vmem_limit_bytes=VMEM_LIMIT)


def _resident(block_shape, index_map):
    return pl.BlockSpec(block_shape, index_map, pipeline_mode=pl.Buffered(1))


def _row_chunks(total, size):
    return [slice(r, r + size) for r in range(0, total, size)]


def _gelu(x):
    return 0.5 * x * (1.0 + lax.erf(x * (1.0 / math.sqrt(2.0))))


def _layer_norm(x, g, b):
    mu = jnp.mean(x, axis=-1, keepdims=True)
    xc = x - mu
    var = jnp.mean(xc * xc, axis=-1, keepdims=True)
    return xc * lax.rsqrt(var + LN_EPS) * g + b


def _dot_bf16x3(a, b):
    m = a.shape[0]
    a_hi = a.astype(BF16)
    b_hi = b.astype(BF16)
    a_hi32 = a_hi.astype(F32)
    b_lo = (b - b_hi.astype(F32)).astype(BF16)
    stacked = jnp.concatenate([a_hi32, a - a_hi32], axis=0).astype(BF16)
    top = jnp.dot(stacked, b_hi, preferred_element_type=F32)
    return top[:m] + top[m:] + jnp.dot(a_hi, b_lo, preferred_element_type=F32)


def _mod_kernel(c_ref, w_ref, b_ref, o_ref):
    c = c_ref[...]
    cond = c * jax.nn.sigmoid(c)
    o_ref[...] = _dot_bf16x3(cond, w_ref[...]) + b_ref[...]


def _mod_call(c, w_ada, b_ada):
    bsz, d = c.shape
    n = w_ada.shape[1]
    tn = MOD_COLS
    return pl.pallas_call(
        _mod_kernel,
        grid=(n // tn,),
        in_specs=[pl.BlockSpec((bsz, d), lambda j: (0, 0)),
                  pl.BlockSpec((d, tn), lambda j: (0, j)),
                  pl.BlockSpec((1, tn), lambda j: (0, j))],
        out_specs=pl.BlockSpec((bsz, tn), lambda j: (0, j)),
        out_shape=jax.ShapeDtypeStruct((bsz, n), F32),
        compiler_params=_cparams(("arbitrary",)),
        name="adaln_mod",
    )(c, w_ada, b_ada.reshape(1, n))


def _inproj_kernel(x_ref, sc_ref, sh_ref, w_ref, b_ref, g_ref, be_ref, o_ref, h_scr, *, seg, rc):
    j = pl.program_id(1)
    chunks = _row_chunks(x_ref.shape[0], rc)

    def emit(act, modulate=False):
        for rows in chunks:
            if modulate:
                h_scr[rows, :] = (x_ref[rows, :] * (1.0 + sc_ref[0]) + sh_ref[0]).astype(BF16)
            acc = jnp.dot(h_scr[rows, :], w_ref[...], preferred_element_type=F32) + b_ref[...]
            o_ref[rows, :] = act(acc).astype(o_ref.dtype)

    @pl.when(j == 0)
    def _():
        emit(jax.nn.sigmoid, modulate=True)

    @pl.when(jnp.logical_and(j > 0, j < seg[0]))
    def _():
        emit(jax.nn.sigmoid)

    @pl.when(jnp.logical_and(j >= seg[0], j < seg[1]))
    def _():
        emit(_gelu)

    @pl.when(jnp.logical_and(j >= seg[1], j < seg[2]))
    def _():
        emit(lambda a: _layer_norm(_gelu(a), g_ref[...], be_ref[...]))

    @pl.when(j >= seg[2])
    def _():
        emit(lambda a: a)


def _inproj_call(xt, mod3, w_in, b_in, ln_g, ln_b, seq, sgu_w, hy_w):
    t, d = xt.shape
    n = w_in.shape[1]
    tm = INPROJ_ROWS
    tn = sgu_w
    assert hy_w == sgu_w and seq % tm == 0 and d % tn == 0
    nblk = n // tn
    n_gate = 2 * d // tn
    seg = (n_gate, n_gate + 1, n_gate + 2)
    per_b = seq // tm
    wcol = lambda i, j: (0, (j + (nblk - n_gate)) % nblk)
    return pl.pallas_call(
        functools.partial(_inproj_kernel, seg=seg, rc=EPILOGUE_ROWS),
        grid=(t // tm, n // tn),
        in_specs=[pl.BlockSpec((tm, d), lambda i, j: (i, 0)),
                  pl.BlockSpec((1, 1, d), lambda i, j: (i // per_b, 0, 1)),
                  pl.BlockSpec((1, 1, d), lambda i, j: (i // per_b, 0, 0)),
                  pl.BlockSpec((d, tn), wcol),
                  pl.BlockSpec((1, tn), wcol),
                  pl.BlockSpec((1, tn), lambda i, j: (0, 0)),
                  pl.BlockSpec((1, tn), lambda i, j: (0, 0))],
        out_specs=pl.BlockSpec((tm, tn), lambda i, j: (i, j)),
        out_shape=jax.ShapeDtypeStruct((t, n), BF16),
        scratch_shapes=[pltpu.VMEM((tm, d), BF16)],
        compiler_params=_cparams(("parallel", "arbitrary")),
        name="in_proj",
    )(xt, mod3, mod3, w_in, b_in.reshape(1, n), ln_g.reshape(1, tn), ln_b.reshape(1, tn))


def _filter_mlp_kernel(f_ref, w1_ref, b1_ref, w2_ref, b2_ref, fr_ref, o_ref):
    hp = lax.Precision.HIGHEST
    fr = fr_ref[...]
    h = jnp.sin(fr * (jnp.dot(f_ref[...], w1_ref[...], preferred_element_type=F32, precision=hp)
                      + b1_ref[...]))
    o_ref[...] = jnp.sin(fr * (jnp.dot(h, w2_ref[...], preferred_element_type=F32, precision=hp)
                               + b2_ref[...]))


def _filter_mlp_call(feats, w1, b1, w2, b2, freq):
    seq = feats.shape[0]
    hid = w1.shape[1]
    return pl.pallas_call(
        _filter_mlp_kernel,
        out_shape=jax.ShapeDtypeStruct((seq, hid), F32),
        name="hyena_filter_mlp",
    )(feats, w1, b1.reshape(1, hid), w2, b2.reshape(1, hid), freq.reshape(1, hid))


DFT_GROUPS = 9
DFT_SLABS = 8
DFT_COMPS = 16
DFT_J = 32


def _packed_start(g, n2):
    return {0: 0, 8: n2}.get(g, 2 * n2 * g)


def _store_slab_block_order(dst_ref, val, t1, n2):
    step = DFT_SLABS * DFT_J
    for k in range(n2 // DFT_J):
        dst = step * k + DFT_J * t1
        dst_ref[dst:dst + DFT_J, :] = val[DFT_J * k:DFT_J * (k + 1), :].astype(dst_ref.dtype)


def _store_block_order(dst_ref, val, n2):
    for t1 in range(DFT_SLABS):
        _store_slab_block_order(dst_ref, val[n2 * t1:n2 * (t1 + 1), :], t1, n2)


def _stage1_fwd(s1_ref, src_ref, dst_ref, n2):
    step = DFT_SLABS * DFT_J
    for k in range(n2 // DFT_J):
        comps = jnp.dot(s1_ref[...], src_ref[step * k:step * (k + 1), :], preferred_element_type=F32)
        for c in range(DFT_COMPS):
            dst_ref[n2 * c + DFT_J * k:n2 * c + DFT_J * (k + 1), :] = (
                comps[DFT_J * c:DFT_J * (c + 1), :].astype(dst_ref.dtype))


def _stage2_fwd_part(g, part, tr_ref, tc_ref, src_ref, n2):
    start = _packed_start(g, n2)
    rows = slice(n2 * part, n2 * (part + 1))
    if g in (0, 8):
        return jnp.dot(tr_ref[0 if g == 0 else 1, rows, :], src_ref[start:start + n2, :],
                       preferred_element_type=F32)
    return jnp.dot(tc_ref[g - 1, rows, :], src_ref[start:start + 2 * n2, :],
                   preferred_element_type=F32)


def _dft_stage_tables(seq):
    n = 2 * seq
    n2 = n // 16
    f2 = np.arange(n2)[:, None]
    t2 = np.arange(n2)[None, :]

    def cs(g):
        ang = ((t2 * (g + 16 * f2)) % n) * (2.0 * math.pi / n)
        return np.cos(ang), np.sin(ang)

    real = np.stack([np.concatenate([c, -s], axis=0) for c, s in (cs(0), cs(8))])
    cplx = np.stack([np.block([[c, s], [-s, c]]) for c, s in (cs(g) for g in range(1, 8))])
    gauss = np.stack([np.stack([c, s - c, c + s]) for c, s in (cs(g) for g in range(1, 8))])

    t1 = np.arange(DFT_SLABS)
    rows = [np.ones(DFT_SLABS), np.where(t1 % 2 == 0, 1.0, -1.0)]
    for g in range(1, 8):
        ang = ((g * t1) % 16) * (math.pi / 8.0)
        rows += [np.cos(ang), -np.sin(ang)]
    s1 = np.kron(np.stack(rows), np.eye(DFT_J))
    fwd = (s1, real, cplx, gauss)
    inv = (s1.T, np.swapaxes(real, 1, 2), np.swapaxes(gauss, 2, 3))
    as_bf16 = lambda t: jnp.asarray(np.ascontiguousarray(t, dtype=np.float32)).astype(BF16)
    return tuple(map(as_bf16, fwd)), tuple(map(as_bf16, inv))


def _filter_spec_kernel(h_ref, wf_ref, wb_ref, t_ref, dl_ref, skip_ref, s1_ref, tr_ref, tc_ref,
                        cast_ref, k_ref, cast_out, e_scr, o_scr, ge_scr, go_scr):
    seq = h_ref.shape[0]
    n2 = seq // 8
    cast_out[...] = cast_ref[...].astype(cast_out.dtype)
    window = jnp.exp(-t_ref[...] * dl_ref[...])
    _store_block_order(e_scr, _dot_bf16x3(h_ref[...], wf_ref[...] + wb_ref[...]) * window, n2)
    _store_block_order(o_scr, _dot_bf16x3(h_ref[...], wf_ref[...] - wb_ref[...]) * window, n2)
    _stage1_fwd(s1_ref, e_scr, ge_scr, n2)
    _stage1_fwd(s1_ref, o_scr, go_scr, n2)
    for g in range(DFT_GROUPS):
        wgt = (1.0 if g in (0, 8) else 2.0) / (2 * seq)
        rows = slice(2 * n2 * g, 2 * n2 * g + n2)
        rows_im = slice(2 * n2 * g + n2, 2 * n2 * (g + 1))
        k_ref[0, rows, :] = ((_stage2_fwd_part(g, 0, tr_ref, tc_ref, ge_scr, n2) + skip_ref[0])
                             * wgt).astype(k_ref.dtype)
        k_ref[0, rows_im, :] = (_stage2_fwd_part(g, 1, tr_ref, tc_ref, go_scr, n2)
                                * wgt).astype(k_ref.dtype)


def _filter_spec_call(h2, w3, tcol, deltas, skip, t_s1, t_real, t_cplx, to_cast, width):
    seq, hid = h2.shape
    n2 = seq // 8
    cb = MXU_DIM
    ncb = width // cb
    per_dir = HY_ORDER * ncb
    rows = DFT_GROUPS * 2 * n2
    const = lambda n, k: (0, 0, 0)
    steps = HY_ORDER * ncb
    assert to_cast.shape[0] % (steps * 2 * SUBLANES) == 0
    slab = pl.BlockSpec((to_cast.shape[0] // steps, to_cast.shape[1]), lambda n, k: (n * ncb + k, 0))
    return pl.pallas_call(
        _filter_spec_kernel,
        grid=(HY_ORDER, ncb),
        in_specs=[_resident((seq, hid), lambda n, k: (0, 0)),
                  pl.BlockSpec((hid, cb), lambda n, k: (0, n * ncb + k)),
                  pl.BlockSpec((hid, cb), lambda n, k: (0, per_dir + n * ncb + k)),
                  _resident((seq, 1), lambda n, k: (0, 0)),
                  pl.BlockSpec((1, cb), lambda n, k: (0, k)),
                  pl.BlockSpec((1, 1, cb), lambda n, k: (n, 0, k)),
                  _resident(t_s1.shape, lambda n, k: (0, 0)),
                  _resident(t_real.shape, const), _resident(t_cplx.shape, const), slab],
        out_specs=[pl.BlockSpec((1, rows, cb), lambda n, k: (n, 0, k)), slab],
        out_shape=[jax.ShapeDtypeStruct((HY_ORDER, rows, width), BF16),
                   jax.ShapeDtypeStruct(to_cast.shape, BF16)],
        scratch_shapes=[pltpu.VMEM((seq, cb), BF16), pltpu.VMEM((seq, cb), BF16),
                        pltpu.VMEM((2 * seq, cb), BF16), pltpu.VMEM((2 * seq, cb), BF16)],
        compiler_params=_cparams(("arbitrary", "arbitrary")),
        name="hyena_filter_spectrum",
    )(h2, w3, w3, tcol, deltas, skip.reshape(HY_ORDER, 1, width), t_s1, t_real, t_cplx, to_cast)


def _short_conv(z, w, b):
    seq = z.shape[0]
    prev = pltpu.roll(z, 1, 0)
    nxt = pltpu.roll(z, seq - 1, 0)
    row = lax.broadcasted_iota(jnp.int32, (SUBLANES, z.shape[1]), 0)
    prev = jnp.concatenate([jnp.where(row == 0, 0.0, prev[:SUBLANES]), prev[SUBLANES:]], axis=0)
    nxt = jnp.concatenate([nxt[:-SUBLANES], jnp.where(row == SUBLANES - 1, 0.0, nxt[-SUBLANES:])],
                          axis=0)
    return w[0:1] * prev + w[1:2] * z + w[2:3] * nxt + b


def _hyena_kernel(v_ref, x1_ref, x2_ref, wv_ref, w1_ref, w2_ref, bv_ref, b1_ref, b2_ref,
                  k_ref, s1_ref, tr_ref, tg_ref, s1t_ref, ur_ref, ug_ref, cast1_ref, cast2_ref,
                  o_ref, cast1_out, cast2_out, y16_scr, gate_scr, g_scr, yc_scr, h_scr):
    seq, cb = v_ref.shape
    n2 = seq // DFT_SLABS
    step = DFT_SLABS * DFT_J
    dot = functools.partial(jnp.dot, preferred_element_type=F32)
    cast1_out[...] = cast1_ref[...].astype(cast1_out.dtype)
    cast2_out[...] = cast2_ref[...].astype(cast2_out.dtype)

    conv3 = lambda z_ref, w_ref, b_ref: _short_conv(z_ref[...].astype(F32), w_ref[...], b_ref[...])
    _store_block_order(y16_scr, conv3(v_ref, wv_ref, bv_ref), n2)
    _store_block_order(gate_scr.at[0], conv3(x1_ref, w1_ref, b1_ref), n2)
    _store_block_order(gate_scr.at[1], conv3(x2_ref, w2_ref, b2_ref), n2)

    for n in range(HY_ORDER):
        last = n + 1 == HY_ORDER
        _stage1_fwd(s1_ref, y16_scr, g_scr, n2)
        for g in range(DFT_GROUPS):
            start = _packed_start(g, n2)
            if g in (0, 8):
                x = dot(tr_ref[0 if g == 0 else 1], g_scr[start:start + n2, :])
                xr, xi = x[:n2], x[n2:]
            else:
                a = g_scr[start:start + n2, :]
                b = g_scr[start + n2:start + 2 * n2, :]
                p1 = dot(tg_ref[g - 1, 0], a + b)
                xr = p1 + dot(tg_ref[g - 1, 1], b)
                xi = p1 - dot(tg_ref[g - 1, 2], a)
            xr, xi = xr.astype(BF16), xi.astype(BF16)
            base = 2 * n2 * g
            kr = k_ref[n, base:base + n2, :]
            ki = k_ref[n, base + n2:base + 2 * n2, :]
            yc_scr[base:base + n2, :] = xr * kr - xi * ki
            yc_scr[base + n2:base + 2 * n2, :] = xr * ki + xi * kr
        for g in range(DFT_GROUPS):
            base = 2 * n2 * g
            if g in (0, 8):
                parts = [(0 if g == 0 else 1,
                          dot(ur_ref[0 if g == 0 else 1], yc_scr[base:base + 2 * n2, :]))]
            else:
                a = yc_scr[base:base + n2, :]
                b = yc_scr[base + n2:base + 2 * n2, :]
                q1 = dot(ug_ref[g - 1, 0], a + b)
                parts = [(2 * g, q1 - dot(ug_ref[g - 1, 2], b)),
                         (2 * g + 1, q1 + dot(ug_ref[g - 1, 1], a))]
            for c, hp in parts:
                for k in range(n2 // DFT_J):
                    dst = DFT_COMPS * DFT_J * k + DFT_J * c
                    h_scr[dst:dst + DFT_J, :] = hp[DFT_J * k:DFT_J * (k + 1), :].astype(BF16)
        for k in range(n2 // DFT_J):
            rows = slice(step * k, step * (k + 1))
            conv = jnp.dot(s1t_ref[...], h_scr[DFT_COMPS * DFT_J * k:DFT_COMPS * DFT_J * (k + 1), :],
                           preferred_element_type=F32)
            y_new = (gate_scr[n, rows, :] * conv).astype(BF16)
            if last:
                for t1 in range(DFT_SLABS):
                    dst = n2 * t1 + DFT_J * k
                    o_ref[dst:dst + DFT_J, :] = y_new[DFT_J * t1:DFT_J * (t1 + 1), :]
            else:
                y16_scr[rows, :] = y_new


def _hyena_call(z, conv_w, conv_b, kspec, tables, to_cast, bsz, seq, width, col0):
    cb = MXU_DIM
    ncb = width // cb
    base = col0 // cb
    zspec = lambda part: pl.BlockSpec((seq, cb), lambda k, b: (b, base + part * ncb + k))
    wspec = lambda part: pl.BlockSpec((HY_SHORT, cb), lambda k, b: (0, part * ncb + k))
    bspec = lambda part: pl.BlockSpec((1, cb), lambda k, b: (0, part * ncb + k))
    cb_all = conv_b.reshape(1, -1)
    const = lambda t: (lambda k, b: (0,) * t.ndim)
    steps = ncb * bsz
    slab = lambda w: pl.BlockSpec((w.shape[0] // steps, w.shape[1]), lambda k, b: (k * bsz + b, 0))
    assert all(w.shape[0] % (steps * 2 * SUBLANES) == 0 for w in to_cast)
    return pl.pallas_call(
        _hyena_kernel,
        grid=(ncb, bsz),
        in_specs=[zspec(0), zspec(1), zspec(2), wspec(0), wspec(1), wspec(2),
                  bspec(0), bspec(1), bspec(2),
                  _resident((HY_ORDER, kspec.shape[1], cb), lambda k, b: (0, 0, k))]
                 + [_resident(t.shape, const(t)) for t in tables] + [slab(w) for w in to_cast],
        out_specs=[pl.BlockSpec((seq, cb), lambda k, b: (b, k))] + [slab(w) for w in to_cast],
        out_shape=[jax.ShapeDtypeStruct((bsz * seq, width), BF16)]
                  + [jax.ShapeDtypeStruct(w.shape, BF16) for w in to_cast],
        scratch_shapes=[pltpu.VMEM((seq, cb), BF16),
                        pltpu.VMEM((HY_ORDER, seq, cb), F32),
                        pltpu.VMEM((2 * seq, cb), BF16),
                        pltpu.VMEM((kspec.shape[1], cb), BF16),
                        pltpu.VMEM((2 * seq, cb), BF16)],
        compiler_params=_cparams(("arbitrary", "arbitrary")),
        name="hyena_mixer",
    )(z, z, z, conv_w, conv_w, conv_w, cb_all, cb_all, cb_all, kspec, *tables, *to_cast)


def _merge_kernel(u_ref, vn_ref, yh_ref, ga_ref, gb_ref, ws_ref, bs_ref, wa_ref, wb_ref, o_ref,
                  s_scr):
    tm = u_ref.shape[0]
    gd = u_ref.shape[1] // SGU_GROUPS
    for n in range(tm // SGU_CHUNK):
        rows = slice(n * SGU_CHUNK, (n + 1) * SGU_CHUNK)
        for g in range(SGU_GROUPS):
            cols = slice(g * gd, (g + 1) * gd)
            mixed = jnp.dot(ws_ref[g], vn_ref[rows, cols], preferred_element_type=F32)
            s_scr[rows, cols] = (u_ref[rows, cols].astype(F32)
                                 * (mixed + bs_ref[:, cols])).astype(BF16)
    y_a = jnp.dot(s_scr[...], wa_ref[...].astype(BF16), preferred_element_type=F32)
    y_b = jnp.dot(yh_ref[...], wb_ref[...].astype(BF16), preferred_element_type=F32)
    o_ref[...] = (ga_ref[...].astype(F32) * y_a + gb_ref[...].astype(F32) * y_b).astype(o_ref.dtype)


def _merge_call(z, yh, sgu_w16, sgu_bfull, w_a, w_b, d):
    t = z.shape[0]
    wdt = w_a.shape[0]
    tm = TOKEN_ROWS
    u_blk = 2 * d // wdt
    return pl.pallas_call(
        _merge_kernel,
        grid=(t // tm,),
        in_specs=[pl.BlockSpec((tm, wdt), lambda i: (i, u_blk)),
                  pl.BlockSpec((tm, wdt), lambda i: (i, u_blk + 1)),
                  pl.BlockSpec((tm, wdt), lambda i: (i, 0)),
                  pl.BlockSpec((tm, d), lambda i: (i, 0)),
                  pl.BlockSpec((tm, d), lambda i: (i, 1)),
                  _resident(sgu_w16.shape, lambda i: (0, 0, 0)),
                  _resident(sgu_bfull.shape, lambda i: (0, 0)),
                  _resident(w_a.shape, lambda i: (0, 0)),
                  _resident(w_b.shape, lambda i: (0, 0))],
        out_specs=pl.BlockSpec((tm, d), lambda i: (i, 0)),
        out_shape=jax.ShapeDtypeStruct((t, d), BF16),
        scratch_shapes=[pltpu.VMEM((tm, wdt), BF16)],
        compiler_params=_cparams(("parallel",)),
        name="gated_merge",
    )(z, z, yh, z, z, sgu_w16, sgu_bfull, w_a, w_b)


def _oproj_kernel(m_ref, x_ref, g1_ref, w_ref, b_ref, lg_ref, lb_ref, o_ref, *, alpha, rc):
    for rows in _row_chunks(m_ref.shape[0], rc):
        mix = jnp.dot(m_ref[rows, :], w_ref[...].astype(BF16),
                      preferred_element_type=F32) + b_ref[...]
        o_ref[rows, :] = _layer_norm(alpha * x_ref[rows, :] + g1_ref[0] * mix,
                                     lg_ref[...], lb_ref[...])


def _oproj_call(m, xt, mod3, w_o, b_o, ln_g, ln_b, seq, alpha):
    t, d = xt.shape
    tm = TOKEN_ROWS
    per_b = seq // tm
    row = lambda i: (0, 0)
    return pl.pallas_call(
        functools.partial(_oproj_kernel, alpha=alpha, rc=EPILOGUE_ROWS),
        grid=(t // tm,),
        in_specs=[pl.BlockSpec((tm, d), lambda i: (i, 0)),
                  pl.BlockSpec((tm, d), lambda i: (i, 0)),
                  pl.BlockSpec((1, 1, d), lambda i: (i // per_b, 0, 2)),
                  _resident((d, d), row),
                  pl.BlockSpec((1, d), row), pl.BlockSpec((1, d), row), pl.BlockSpec((1, d), row)],
        out_specs=pl.BlockSpec((tm, d), lambda i: (i, 0)),
        out_shape=jax.ShapeDtypeStruct((t, d), F32),
        compiler_params=_cparams(("parallel",)),
        name="out_proj_ln",
    )(m, xt, mod3, w_o, b_o.reshape(1, d), ln_g.reshape(1, d), ln_b.reshape(1, d))


def _mlp_kernel(x_ref, sc_ref, sh_ref, g2_ref, w1_ref, b1_ref, w2_ref, b2_ref, lg_ref, lb_ref,
                o_ref, h_scr, acc_scr, *, alpha, rc):
    j = pl.program_id(1)
    last = pl.num_programs(1) - 1
    chunks = _row_chunks(x_ref.shape[0], rc)

    def partial_out(rows):
        hid = jnp.dot(h_scr[rows, :], w1_ref[...], preferred_element_type=F32) + b1_ref[...]
        hid = jnp.square(jnp.maximum(hid, 0.0)).astype(BF16)
        return jnp.dot(hid, w2_ref[...], preferred_element_type=F32)

    @pl.when(j == 0)
    def _():
        for rows in chunks:
            h_scr[rows, :] = (x_ref[rows, :] * (1.0 + sc_ref[0]) + sh_ref[0]).astype(BF16)
            acc_scr[rows, :] = partial_out(rows)

    @pl.when(jnp.logical_and(j > 0, j < last))
    def _():
        acc_scr[...] += partial_out(slice(None))

    @pl.when(j == last)
    def _():
        for rows in chunks:
            f = acc_scr[rows, :] + partial_out(rows) + b2_ref[...]
            o_ref[rows, :] = _layer_norm(alpha * x_ref[rows, :] + g2_ref[0] * f,
                                         lg_ref[...], lb_ref[...])


def _mlp_call(x1, mod3, w_m1, b_m1, w_m2, b_m2, ln_g, ln_b, seq, alpha):
    t, d = x1.shape
    dff = w_m1.shape[1]
    tm = TOKEN_ROWS
    tf = MLP_HIDDEN
    per_b = seq // tm
    row = lambda i, j: (0, 0)
    return pl.pallas_call(
        functools.partial(_mlp_kernel, alpha=alpha, rc=EPILOGUE_ROWS),
        grid=(t // tm, dff // tf),
        in_specs=[pl.BlockSpec((tm, d), lambda i, j: (i, 0)),
                  pl.BlockSpec((1, 1, d), lambda i, j: (i // per_b, 0, 4)),
                  pl.BlockSpec((1, 1, d), lambda i, j: (i // per_b, 0, 3)),
                  pl.BlockSpec((1, 1, d), lambda i, j: (i // per_b, 0, 5)),
                  pl.BlockSpec((d, tf), lambda i, j: (0, j)),
                  pl.BlockSpec((1, tf), lambda i, j: (0, j)),
                  pl.BlockSpec((tf, d), lambda i, j: (j, 0)),
                  pl.BlockSpec((1, d), row), pl.BlockSpec((1, d), row), pl.BlockSpec((1, d), row)],
        out_specs=pl.BlockSpec((tm, d), lambda i, j: (i, 0)),
        out_shape=jax.ShapeDtypeStruct((t, d), F32),
        scratch_shapes=[pltpu.VMEM((tm, d), BF16), pltpu.VMEM((tm, d), F32)],
        compiler_params=_cparams(("parallel", "arbitrary")),
        name="mlp_ln",
    )(x1, mod3, mod3, mod3, w_m1, b_m1.reshape(1, dff), w_m2, b_m2.reshape(1, d),
      ln_g.reshape(1, d), ln_b.reshape(1, d))


def _filter_features(seq):
    bands_n = (HY_EMB - 1) // 2
    t = np.linspace(0.0, 1.0, seq)[:, None]
    omega = 2.0 * math.pi * np.arange(seq)[:, None] / seq
    bands = np.linspace(1e-4, bands_n - 1, bands_n)[None, :]
    feats = np.concatenate([t, np.cos(bands * omega), -np.sin(bands * omega)], axis=-1)
    return jnp.asarray(t, F32), jnp.asarray(feats, F32)


def _decay_rates(width):
    min_decay = math.log(HY_DECAY_TARGET) / HY_SLOW_DECAY
    max_decay = math.log(HY_DECAY_TARGET) / HY_FAST_DECAY
    return jnp.asarray(np.abs(np.linspace(min_decay, max_decay, width))[None, :], F32)


def kernel(x, c, w_ada, b_ada, w_in, b_in, sgu_ln_g, sgu_ln_b, sgu_w, sgu_b, hy_conv_w, hy_conv_b, hy_w1, hy_b1, hy_w2, hy_b2, hy_freq, hy_w3, hy_skip, w_branch_a, w_branch_b, w_o, b_o, ln1_g, ln1_b, w_m1, b_m1, w_m2, b_m2, ln2_g, ln2_b):
    bsz, seq, d = x.shape
    depth = w_ada.shape[0]
    alpha = (2.0 * depth) ** 0.25
    sgu_width = w_branch_a.shape[1]
    hy_width = w_branch_b.shape[1]
    assert seq % SGU_CHUNK == 0 and sgu_w.shape[-1] == SGU_CHUNK

    assert seq % (8 * MXU_DIM) == 0
    (t_s1, t_real, t_cplx, t_gauss), inv_tables = _dft_stage_tables(seq)
    tcol, feats = _filter_features(seq)
    deltas = _decay_rates(hy_width)

    xt = x.reshape(bsz * seq, d)
    for l in range(depth):
        mod3 = _mod_call(c, w_ada[l], b_ada[l]).reshape(bsz, 1, N_MOD * d)

        h2 = _filter_mlp_call(feats, hy_w1[l], hy_b1[l], hy_w2[l], hy_b2[l], hy_freq[l])
        kspec, w_in_16 = _filter_spec_call(h2, hy_w3[l], tcol, deltas, hy_skip[l],
                                           t_s1, t_real, t_cplx, w_in[l], hy_width)

        z = _inproj_call(xt, mod3, w_in_16, b_in[l], sgu_ln_g[l], sgu_ln_b[l],
                         seq, sgu_width, hy_width)
        yh, w_m1_16, w_m2_16 = _hyena_call(z, hy_conv_w[l], hy_conv_b[l], kspec,
                                           (t_s1, t_real, t_gauss) + inv_tables, (w_m1[l], w_m2[l]),
                                           bsz, seq, hy_width, 2 * d + 2 * sgu_width)

        gd = sgu_width // SGU_GROUPS
        bs_full = jnp.repeat(sgu_b[l].T, gd, axis=1)
        m = _merge_call(z, yh, sgu_w[l].astype(BF16), bs_full, w_branch_a[l], w_branch_b[l], d)

        x1 = _oproj_call(m, xt, mod3, w_o[l], b_o[l], ln1_g[l], ln1_b[l], seq, alpha)
        xt = _mlp_call(x1, mod3, w_m1_16, b_m1[l], w_m2_16, b_m2[l], ln2_g[l], ln2_b[l], seq, alpha)
    return xt.reshape(bsz, seq, d)
```

```python
import functools
import math

import jax
import jax.numpy as jnp
import numpy as np
from jax import lax
from jax.experimental import pallas as pl
from jax.experimental.pallas import tpu as pltpu

F32 = jnp.float32
BF16 = jnp.bfloat16

SGU_CHUNK = 128
SGU_GROUPS = 8
HY_ORDER = 2
HY_SHORT = 3
HY_EMB = 33
HY_DECAY_TARGET = 1e-2
HY_FAST_DECAY = 0.3
HY_SLOW_DECAY = 1.5
N_MOD = 6
LN_EPS = 1e-5

SUBLANES = 8
MXU_DIM = 256
VMEM_LIMIT = 48 * 1024 * 1024

MOD_COLS = 1024
INPROJ_ROWS = 1024
TOKEN_ROWS = 512
MLP_HIDDEN = 1024
EPILOGUE_ROWS = 256


def _cparams(sem):
    return pltpu.CompilerParams(dimension_semantics=sem, vmem_limit_bytes=VMEM_LIMIT)


def _resident(block_shape, index_map):
    return pl.BlockSpec(block_shape, index_map, pipeline_mode=pl.Buffered(1))


def _row_chunks(total, size):
    return [slice(r, r + size) for r in range(0, total, size)]


def _gelu(x):
    return 0.5 * x * (1.0 + lax.erf(x * (1.0 / math.sqrt(2.0))))


def _layer_norm(x, g, b):
    mu = jnp.mean(x, axis=-1, keepdims=True)
    xc = x - mu
    var = jnp.mean(xc * xc, axis=-1, keepdims=True)
    return xc * lax.rsqrt(var + LN_EPS) * g + b


def _dot_bf16x3(a, b):
    m = a.shape[0]
    a_hi = a.astype(BF16)
    b_hi = b.astype(BF16)
    a_hi32 = a_hi.astype(F32)
    b_lo = (b - b_hi.astype(F32)).astype(BF16)
    stacked = jnp.concatenate([a_hi32, a - a_hi32], axis=0).astype(BF16)
    top = jnp.dot(stacked, b_hi, preferred_element_type=F32)
    return top[:m] + top[m:] + jnp.dot(a_hi, b_lo, preferred_element_type=F32)


def _mod_kernel(c_ref, w_ref, b_ref, o_ref):
    c = c_ref[...]
    cond = c * jax.nn.sigmoid(c)
    o_ref[...] = _dot_bf16x3(cond, w_ref[...]) + b_ref[...]


def _mod_call(c, w_ada, b_ada):
    bsz, d = c.shape
    n = w_ada.shape[1]
    tn = MOD_COLS
    return pl.pallas_call(
        _mod_kernel,
        grid=(n // tn,),
        in_specs=[pl.BlockSpec((bsz, d), lambda j: (0, 0)),
                  pl.BlockSpec((d, tn), lambda j: (0, j)),
                  pl.BlockSpec((1, tn), lambda j: (0, j))],
        out_specs=pl.BlockSpec((bsz, tn), lambda j: (0, j)),
        out_shape=jax.ShapeDtypeStruct((bsz, n), F32),
        compiler_params=_cparams(("arbitrary",)),
        name="adaln_mod",
    )(c, w_ada, b_ada.reshape(1, n))


def _inproj_kernel(x_ref, sc_ref, sh_ref, w_ref, b_ref, g_ref, be_ref, o_ref, h_scr, *, seg, rc):
    j = pl.program_id(1)
    chunks = _row_chunks(x_ref.shape[0], rc)

    def emit(act, modulate=False):
        for rows in chunks:
            if modulate:
                h_scr[rows, :] = (x_ref[rows, :] * (1.0 + sc_ref[0]) + sh_ref[0]).astype(BF16)
            acc = jnp.dot(h_scr[rows, :], w_ref[...], preferred_element_type=F32) + b_ref[...]
            o_ref[rows, :] = act(acc).astype(o_ref.dtype)

    @pl.when(j == 0)
    def _():
        emit(jax.nn.sigmoid, modulate=True)

    @pl.when(jnp.logical_and(j > 0, j < seg[0]))
    def _():
        emit(jax.nn.sigmoid)

    @pl.when(jnp.logical_and(j >= seg[0], j < seg[1]))
    def _():
        emit(_gelu)

    @pl.when(jnp.logical_and(j >= seg[1], j < seg[2]))
    def _():
        emit(lambda a: _layer_norm(_gelu(a), g_ref[...], be_ref[...]))

    @pl.when(j >= seg[2])
    def _():
        emit(lambda a: a)


def _inproj_call(xt, mod3, w_in, b_in, ln_g, ln_b, seq, sgu_w, hy_w):
    t, d = xt.shape
    n = w_in.shape[1]
    tm = INPROJ_ROWS
    tn = sgu_w
    assert hy_w == sgu_w and seq % tm == 0 and d % tn == 0
    nblk = n // tn
    n_gate = 2 * d // tn
    seg = (n_gate, n_gate + 1, n_gate + 2)
    per_b = seq // tm
    wcol = lambda i, j: (0, (j + (nblk - n_gate)) % nblk)
    return pl.pallas_call(
        functools.partial(_inproj_kernel, seg=seg, rc=EPILOGUE_ROWS),
        grid=(t // tm, n // tn),
        in_specs=[pl.BlockSpec((tm, d), lambda i, j: (i, 0)),
                  pl.BlockSpec((1, 1, d), lambda i, j: (i // per_b, 0, 1)),
                  pl.BlockSpec((1, 1, d), lambda i, j: (i // per_b, 0, 0)),
                  pl.BlockSpec((d, tn), wcol),
                  pl.BlockSpec((1, tn), wcol),
                  pl.BlockSpec((1, tn), lambda i, j: (0, 0)),
                  pl.BlockSpec((1, tn), lambda i, j: (0, 0))],
        out_specs=pl.BlockSpec((tm, tn), lambda i, j: (i, j)),
        out_shape=jax.ShapeDtypeStruct((t, n), BF16),
        scratch_shapes=[pltpu.VMEM((tm, d), BF16)],
        compiler_params=_cparams(("parallel", "arbitrary")),
        name="in_proj",
    )(xt, mod3, mod3, w_in, b_in.reshape(1, n), ln_g.reshape(1, tn), ln_b.reshape(1, tn))


def _filter_mlp_kernel(f_ref, w1_ref, b1_ref, w2_ref, b2_ref, fr_ref, o_ref):
    hp = lax.Precision.HIGHEST
    fr = fr_ref[...]
    h = jnp.sin(fr * (jnp.dot(f_ref[...], w1_ref[...], preferred_element_type=F32, precision=hp)
                      + b1_ref[...]))
    o_ref[...] = jnp.sin(fr * (jnp.dot(h, w2_ref[...], preferred_element_type=F32, precision=hp)
                               + b2_ref[...]))


def _filter_mlp_call(feats, w1, b1, w2, b2, freq):
    seq = feats.shape[0]
    hid = w1.shape[1]
    return pl.pallas_call(
        _filter_mlp_kernel,
        out_shape=jax.ShapeDtypeStruct((seq, hid), F32),
        name="hyena_filter_mlp",
    )(feats, w1, b1.reshape(1, hid), w2, b2.reshape(1, hid), freq.reshape(1, hid))


DFT_GROUPS = 9
DFT_SLABS = 8
DFT_COMPS = 16
DFT_J = 32


def _packed_start(g, n2):
    return {0: 0, 8: n2}.get(g, 2 * n2 * g)


def _store_slab_block_order(dst_ref, val, t1, n2):
    step = DFT_SLABS * DFT_J
    for k in range(n2 // DFT_J):
        dst = step * k + DFT_J * t1
        dst_ref[dst:dst + DFT_J, :] = val[DFT_J * k:DFT_J * (k + 1), :].astype(dst_ref.dtype)


def _store_block_order(dst_ref, val, n2):
    for t1 in range(DFT_SLABS):
        _store_slab_block_order(dst_ref, val[n2 * t1:n2 * (t1 + 1), :], t1, n2)


def _stage1_fwd(s1_ref, src_ref, dst_ref, n2):
    step = DFT_SLABS * DFT_J
    for k in range(n2 // DFT_J):
        comps = jnp.dot(s1_ref[...], src_ref[step * k:step * (k + 1), :], preferred_element_type=F32)
        for c in range(DFT_COMPS):
            dst_ref[n2 * c + DFT_J * k:n2 * c + DFT_J * (k + 1), :] = (
                comps[DFT_J * c:DFT_J * (c + 1), :].astype(dst_ref.dtype))


def _stage2_fwd_part(g, part, tr_ref, tc_ref, src_ref, n2):
    start = _packed_start(g, n2)
    rows = slice(n2 * part, n2 * (part + 1))
    if g in (0, 8):
        return jnp.dot(tr_ref[0 if g == 0 else 1, rows, :], src_ref[start:start + n2, :],
                       preferred_element_type=F32)
    return jnp.dot(tc_ref[g - 1, rows, :], src_ref[start:start + 2 * n2, :],
                   preferred_element_type=F32)


def _dft_stage_tables(seq):
    n = 2 * seq
    n2 = n // 16
    f2 = np.arange(n2)[:, None]
    t2 = np.arange(n2)[None, :]

    def cs(g):
        ang = ((t2 * (g + 16 * f2)) % n) * (2.0 * math.pi / n)
        return np.cos(ang), np.sin(ang)

    real = np.stack([np.concatenate([c, -s], axis=0) for c, s in (cs(0), cs(8))])
    cplx = np.stack([np.block([[c, s], [-s, c]]) for c, s in (cs(g) for g in range(1, 8))])
    gauss = np.stack([np.stack([c, s - c, c + s]) for c, s in (cs(g) for g in range(1, 8))])

    t1 = np.arange(DFT_SLABS)
    rows = [np.ones(DFT_SLABS), np.where(t1 % 2 == 0, 1.0, -1.0)]
    for g in range(1, 8):
        ang = ((g * t1) % 16) * (math.pi / 8.0)
        rows += [np.cos(ang), -np.sin(ang)]
    s1 = np.kron(np.stack(rows), np.eye(DFT_J))
    fwd = (s1, real, cplx, gauss)
    inv = (s1.T, np.swapaxes(real, 1, 2), np.swapaxes(gauss, 2, 3))
    as_bf16 = lambda t: jnp.asarray(np.ascontiguousarray(t, dtype=np.float32)).astype(BF16)
    return tuple(map(as_bf16, fwd)), tuple(map(as_bf16, inv))


def _filter_spec_kernel(h_ref, wf_ref, wb_ref, t_ref, dl_ref, skip_ref, s1_ref, tr_ref, tc_ref,
                        cast_ref, k_ref, cast_out, e_scr, o_scr, ge_scr, go_scr):
    seq = h_ref.shape[0]
    n2 = seq // 8
    cast_out[...] = cast_ref[...].astype(cast_out.dtype)
    window = jnp.exp(-t_ref[...] * dl_ref[...])
    _store_block_order(e_scr, _dot_bf16x3(h_ref[...], wf_ref[...] + wb_ref[...]) * window, n2)
    _store_block_order(o_scr, _dot_bf16x3(h_ref[...], wf_ref[...] - wb_ref[...]) * window, n2)
    _stage1_fwd(s1_ref, e_scr, ge_scr, n2)
    _stage1_fwd(s1_ref, o_scr, go_scr, n2)
    for g in range(DFT_GROUPS):
        wgt = (1.0 if g in (0, 8) else 2.0) / (2 * seq)
        rows = slice(2 * n2 * g, 2 * n2 * g + n2)
        rows_im = slice(2 * n2 * g + n2, 2 * n2 * (g + 1))
        k_ref[0, rows, :] = ((_stage2_fwd_part(g, 0, tr_ref, tc_ref, ge_scr, n2) + skip_ref[0])
                             * wgt).astype(k_ref.dtype)
        k_ref[0, rows_im, :] = (_stage2_fwd_part(g, 1, tr_ref, tc_ref, go_scr, n2)
                                * wgt).astype(k_ref.dtype)


def _filter_spec_call(h2, w3, tcol, deltas, skip, t_s1, t_real, t_cplx, to_cast, width):
    seq, hid = h2.shape
    n2 = seq // 8
    cb = MXU_DIM
    ncb = width // cb
    per_dir = HY_ORDER * ncb
    rows = DFT_GROUPS * 2 * n2
    const = lambda n, k: (0, 0, 0)
    steps = HY_ORDER * ncb
    assert to_cast.shape[0] % (steps * 2 * SUBLANES) == 0
    slab = pl.BlockSpec((to_cast.shape[0] // steps, to_cast.shape[1]), lambda n, k: (n * ncb + k, 0))
    return pl.pallas_call(
        _filter_spec_kernel,
        grid=(HY_ORDER, ncb),
        in_specs=[_resident((seq, hid), lambda n, k: (0, 0)),
                  pl.BlockSpec((hid, cb), lambda n, k: (0, n * ncb + k)),
                  pl.BlockSpec((hid, cb), lambda n, k: (0, per_dir + n * ncb + k)),
                  _resident((seq, 1), lambda n, k: (0, 0)),
                  pl.BlockSpec((1, cb), lambda n, k: (0, k)),
                  pl.BlockSpec((1, 1, cb), lambda n, k: (n, 0, k)),
                  _resident(t_s1.shape, lambda n, k: (0, 0)),
                  _resident(t_real.shape, const), _resident(t_cplx.shape, const), slab],
        out_specs=[pl.BlockSpec((1, rows, cb), lambda n, k: (n, 0, k)), slab],
        out_shape=[jax.ShapeDtypeStruct((HY_ORDER, rows, width), BF16),
                   jax.ShapeDtypeStruct(to_cast.shape, BF16)],
        scratch_shapes=[pltpu.VMEM((seq, cb), BF16), pltpu.VMEM((seq, cb), BF16),
                        pltpu.VMEM((2 * seq, cb), BF16), pltpu.VMEM((2 * seq, cb), BF16)],
        compiler_params=_cparams(("arbitrary", "arbitrary")),
        name="hyena_filter_spectrum",
    )(h2, w3, w3, tcol, deltas, skip.reshape(HY_ORDER, 1, width), t_s1, t_real, t_cplx, to_cast)


def _short_conv(z, w, b):
    seq = z.shape[0]
    prev = pltpu.roll(z, 1, 0)
    nxt = pltpu.roll(z, seq - 1, 0)
    row = lax.broadcasted_iota(jnp.int32, (SUBLANES, z.shape[1]), 0)
    prev = jnp.concatenate([jnp.where(row == 0, 0.0, prev[:SUBLANES]), prev[SUBLANES:]], axis=0)
    nxt = jnp.concatenate([nxt[:-SUBLANES], jnp.where(row == SUBLANES - 1, 0.0, nxt[-SUBLANES:])],
                          axis=0)
    return w[0:1] * prev + w[1:2] * z + w[2:3] * nxt + b


def _hyena_kernel(v_ref, x1_ref, x2_ref, wv_ref, w1_ref, w2_ref, bv_ref, b1_ref, b2_ref,
                  k_ref, s1_ref, tr_ref, tg_ref, s1t_ref, ur_ref, ug_ref, cast1_ref, cast2_ref,
                  o_ref, cast1_out, cast2_out, y16_scr, gate_scr, g_scr, yc_scr, h_scr):
    seq, cb = v_ref.shape
    n2 = seq // DFT_SLABS
    step = DFT_SLABS * DFT_J
    dot = functools.partial(jnp.dot, preferred_element_type=F32)
    cast1_out[...] = cast1_ref[...].astype(cast1_out.dtype)
    cast2_out[...] = cast2_ref[...].astype(cast2_out.dtype)

    conv3 = lambda z_ref, w_ref, b_ref: _short_conv(z_ref[...].astype(F32), w_ref[...], b_ref[...])
    _store_block_order(y16_scr, conv3(v_ref, wv_ref, bv_ref), n2)
    _store_block_order(gate_scr.at[0], conv3(x1_ref, w1_ref, b1_ref), n2)
    _store_block_order(gate_scr.at[1], conv3(x2_ref, w2_ref, b2_ref), n2)

    for n in range(HY_ORDER):
        last = n + 1 == HY_ORDER
        _stage1_fwd(s1_ref, y16_scr, g_scr, n2)
        for g in range(DFT_GROUPS):
            start = _packed_start(g, n2)
            if g in (0, 8):
                x = dot(tr_ref[0 if g == 0 else 1], g_scr[start:start + n2, :])
                xr, xi = x[:n2], x[n2:]
            else:
                a = g_scr[start:start + n2, :]
                b = g_scr[start + n2:start + 2 * n2, :]
                p1 = dot(tg_ref[g - 1, 0], a + b)
                xr = p1 + dot(tg_ref[g - 1, 1], b)
                xi = p1 - dot(tg_ref[g - 1, 2], a)
            xr, xi = xr.astype(BF16), xi.astype(BF16)
            base = 2 * n2 * g
            kr = k_ref[n, base:base + n2, :]
            ki = k_ref[n, base + n2:base + 2 * n2, :]
            yc_scr[base:base + n2, :] = xr * kr - xi * ki
            yc_scr[base + n2:base + 2 * n2, :] = xr * ki + xi * kr
        for g in range(DFT_GROUPS):
            base = 2 * n2 * g
            if g in (0, 8):
                parts = [(0 if g == 0 else 1,
                          dot(ur_ref[0 if g == 0 else 1], yc_scr[base:base + 2 * n2, :]))]
            else:
                a = yc_scr[base:base + n2, :]
                b = yc_scr[base + n2:base + 2 * n2, :]
                q1 = dot(ug_ref[g - 1, 0], a + b)
                parts = [(2 * g, q1 - dot(ug_ref[g - 1, 2], b)),
                         (2 * g + 1, q1 + dot(ug_ref[g - 1, 1], a))]
            for c, hp in parts:
                for k in range(n2 // DFT_J):
                    dst = DFT_COMPS * DFT_J * k + DFT_J * c
                    h_scr[dst:dst + DFT_J, :] = hp[DFT_J * k:DFT_J * (k + 1), :].astype(BF16)
        for k in range(n2 // DFT_J):
            rows = slice(step * k, step * (k + 1))
            conv = jnp.dot(s1t_ref[...], h_scr[DFT_COMPS * DFT_J * k:DFT_COMPS * DFT_J * (k + 1), :],
                           preferred_element_type=F32)
            y_new = (gate_scr[n, rows, :] * conv).astype(BF16)
            if last:
                for t1 in range(DFT_SLABS):
                    dst = n2 * t1 + DFT_J * k
                    o_ref[dst:dst + DFT_J, :] = y_new[DFT_J * t1:DFT_J * (t1 + 1), :]
            else:
                y16_scr[rows, :] = y_new


def _hyena_call(z, conv_w, conv_b, kspec, tables, to_cast, bsz, seq, width, col0):
    cb = MXU_DIM
    ncb = width // cb
    base = col0 // cb
    zspec = lambda part: pl.BlockSpec((seq, cb), lambda k, b: (b, base + part * ncb + k))
    wspec = lambda part: pl.BlockSpec((HY_SHORT, cb), lambda k, b: (0, part * ncb + k))
    bspec = lambda part: pl.BlockSpec((1, cb), lambda k, b: (0, part * ncb + k))
    cb_all = conv_b.reshape(1, -1)
    const = lambda t: (lambda k, b: (0,) * t.ndim)
    steps = ncb * bsz
    slab = lambda w: pl.BlockSpec((w.shape[0] // steps, w.shape[1]), lambda k, b: (k * bsz + b, 0))
    assert all(w.shape[0] % (steps * 2 * SUBLANES) == 0 for w in to_cast)
    return pl.pallas_call(
        _hyena_kernel,
        grid=(ncb, bsz),
        in_specs=[zspec(0), zspec(1), zspec(2), wspec(0), wspec(1), wspec(2),
                  bspec(0), bspec(1), bspec(2),
                  pl.BlockSpec((HY_ORDER, kspec.shape[1], cb), lambda k, b: (0, 0, k))]
                 + [_resident(t.shape, const(t)) for t in tables] + [slab(w) for w in to_cast],
        out_specs=[pl.BlockSpec((seq, cb), lambda k, b: (b, k))] + [slab(w) for w in to_cast],
        out_shape=[jax.ShapeDtypeStruct((bsz * seq, width), BF16)]
                  + [jax.ShapeDtypeStruct(w.shape, BF16) for w in to_cast],
        scratch_shapes=[pltpu.VMEM((seq, cb), BF16),
                        pltpu.VMEM((HY_ORDER, seq, cb), F32),
                        pltpu.VMEM((2 * seq, cb), BF16),
                        pltpu.VMEM((kspec.shape[1], cb), BF16),
                        pltpu.VMEM((2 * seq, cb), BF16)],
        compiler_params=_cparams(("arbitrary", "arbitrary")),
        name="hyena_mixer",
    )(z, z, z, conv_w, conv_w, conv_w, cb_all, cb_all, cb_all, kspec, *tables, *to_cast)


def _merge_kernel(u_ref, vn_ref, yh_ref, ga_ref, gb_ref, ws_ref, bs_ref, wa_ref, wb_ref, o_ref,
                  s_scr):
    tm = u_ref.shape[0]
    gd = u_ref.shape[1] // SGU_GROUPS
    for n in range(tm // SGU_CHUNK):
        rows = slice(n * SGU_CHUNK, (n + 1) * SGU_CHUNK)
        for g in range(SGU_GROUPS):
            cols = slice(g * gd, (g + 1) * gd)
            mixed = jnp.dot(ws_ref[g], vn_ref[rows, cols], preferred_element_type=F32)
            s_scr[rows, cols] = (u_ref[rows, cols].astype(F32)
                                 * (mixed + bs_ref[:, cols])).astype(BF16)
    y_a = jnp.dot(s_scr[...], wa_ref[...].astype(BF16), preferred_element_type=F32)
    y_b = jnp.dot(yh_ref[...], wb_ref[...].astype(BF16), preferred_element_type=F32)
    o_ref[...] = (ga_ref[...].astype(F32) * y_a + gb_ref[...].astype(F32) * y_b).astype(o_ref.dtype)


def _merge_call(z, yh, sgu_w16, sgu_bfull, w_a, w_b, d):
    t = z.shape[0]
    wdt = w_a.shape[0]
    tm = TOKEN_ROWS
    u_blk = 2 * d // wdt
    return pl.pallas_call(
        _merge_kernel,
        grid=(t // tm,),
        in_specs=[pl.BlockSpec((tm, wdt), lambda i: (i, u_blk)),
                  pl.BlockSpec((tm, wdt), lambda i: (i, u_blk + 1)),
                  pl.BlockSpec((tm, wdt), lambda i: (i, 0)),
                  pl.BlockSpec((tm, d), lambda i: (i, 0)),
                  pl.BlockSpec((tm, d), lambda i: (i, 1)),
                  _resident(sgu_w16.shape, lambda i: (0, 0, 0)),
                  _resident(sgu_bfull.shape, lambda i: (0, 0)),
                  _resident(w_a.shape, lambda i: (0, 0)),
                  _resident(w_b.shape, lambda i: (0, 0))],
        out_specs=pl.BlockSpec((tm, d), lambda i: (i, 0)),
        out_shape=jax.ShapeDtypeStruct((t, d), BF16),
        scratch_shapes=[pltpu.VMEM((tm, wdt), BF16)],
        compiler_params=_cparams(("parallel",)),
        name="gated_merge",
    )(z, z, yh, z, z, sgu_w16, sgu_bfull, w_a, w_b)


def _oproj_kernel(m_ref, x_ref, g1_ref, w_ref, b_ref, lg_ref, lb_ref, o_ref, *, alpha, rc):
    for rows in _row_chunks(m_ref.shape[0], rc):
        mix = jnp.dot(m_ref[rows, :], w_ref[...].astype(BF16),
                      preferred_element_type=F32) + b_ref[...]
        o_ref[rows, :] = _layer_norm(alpha * x_ref[rows, :] + g1_ref[0] * mix,
                                     lg_ref[...], lb_ref[...])


def _oproj_call(m, xt, mod3, w_o, b_o, ln_g, ln_b, seq, alpha):
    t, d = xt.shape
    tm = TOKEN_ROWS
    per_b = seq // tm
    row = lambda i: (0, 0)
    return pl.pallas_call(
        functools.partial(_oproj_kernel, alpha=alpha, rc=EPILOGUE_ROWS),
        grid=(t // tm,),
        in_specs=[pl.BlockSpec((tm, d), lambda i: (i, 0)),
                  pl.BlockSpec((tm, d), lambda i: (i, 0)),
                  pl.BlockSpec((1, 1, d), lambda i: (i // per_b, 0, 2)),
                  _resident((d, d), row),
                  pl.BlockSpec((1, d), row), pl.BlockSpec((1, d), row), pl.BlockSpec((1, d), row)],
        out_specs=pl.BlockSpec((tm, d), lambda i: (i, 0)),
        out_shape=jax.ShapeDtypeStruct((t, d), F32),
        compiler_params=_cparams(("parallel",)),
        name="out_proj_ln",
    )(m, xt, mod3, w_o, b_o.reshape(1, d), ln_g.reshape(1, d), ln_b.reshape(1, d))


def _mlp_kernel(x_ref, sc_ref, sh_ref, g2_ref, w1_ref, b1_ref, w2_ref, b2_ref, lg_ref, lb_ref,
                o_ref, h_scr, acc_scr, *, alpha, rc):
    j = pl.program_id(1)
    last = pl.num_programs(1) - 1
    chunks = _row_chunks(x_ref.shape[0], rc)

    def partial_out(rows):
        hid = jnp.dot(h_scr[rows, :], w1_ref[...], preferred_element_type=F32) + b1_ref[...]
        hid = jnp.square(jnp.maximum(hid, 0.0)).astype(BF16)
        return jnp.dot(hid, w2_ref[...], preferred_element_type=F32)

    @pl.when(j == 0)
    def _():
        for rows in chunks:
            h_scr[rows, :] = (x_ref[rows, :] * (1.0 + sc_ref[0]) + sh_ref[0]).astype(BF16)
            acc_scr[rows, :] = partial_out(rows)

    @pl.when(jnp.logical_and(j > 0, j < last))
    def _():
        acc_scr[...] += partial_out(slice(None))

    @pl.when(j == last)
    def _():
        for rows in chunks:
            f = acc_scr[rows, :] + partial_out(rows) + b2_ref[...]
            o_ref[rows, :] = _layer_norm(alpha * x_ref[rows, :] + g2_ref[0] * f,
                                         lg_ref[...], lb_ref[...])


def _mlp_call(x1, mod3, w_m1, b_m1, w_m2, b_m2, ln_g, ln_b, seq, alpha):
    t, d = x1.shape
    dff = w_m1.shape[1]
    tm = TOKEN_ROWS
    tf = MLP_HIDDEN
    per_b = seq // tm
    row = lambda i, j: (0, 0)
    return pl.pallas_call(
        functools.partial(_mlp_kernel, alpha=alpha, rc=EPILOGUE_ROWS),
        grid=(t // tm, dff // tf),
        in_specs=[pl.BlockSpec((tm, d), lambda i, j: (i, 0)),
                  pl.BlockSpec((1, 1, d), lambda i, j: (i // per_b, 0, 4)),
                  pl.BlockSpec((1, 1, d), lambda i, j: (i // per_b, 0, 3)),
                  pl.BlockSpec((1, 1, d), lambda i, j: (i // per_b, 0, 5)),
                  pl.BlockSpec((d, tf), lambda i, j: (0, j)),
                  pl.BlockSpec((1, tf), lambda i, j: (0, j)),
                  pl.BlockSpec((tf, d), lambda i, j: (j, 0)),
                  pl.BlockSpec((1, d), row), pl.BlockSpec((1, d), row), pl.BlockSpec((1, d), row)],
        out_specs=pl.BlockSpec((tm, d), lambda i, j: (i, 0)),
        out_shape=jax.ShapeDtypeStruct((t, d), F32),
        scratch_shapes=[pltpu.VMEM((tm, d), BF16), pltpu.VMEM((tm, d), F32)],
        compiler_params=_cparams(("parallel", "arbitrary")),
        name="mlp_ln",
    )(x1, mod3, mod3, mod3, w_m1, b_m1.reshape(1, dff), w_m2, b_m2.reshape(1, d),
      ln_g.reshape(1, d), ln_b.reshape(1, d))


def _filter_features(seq):
    bands_n = (HY_EMB - 1) // 2
    t = np.linspace(0.0, 1.0, seq)[:, None]
    omega = 2.0 * math.pi * np.arange(seq)[:, None] / seq
    bands = np.linspace(1e-4, bands_n - 1, bands_n)[None, :]
    feats = np.concatenate([t, np.cos(bands * omega), -np.sin(bands * omega)], axis=-1)
    return jnp.asarray(t, F32), jnp.asarray(feats, F32)


def _decay_rates(width):
    min_decay = math.log(HY_DECAY_TARGET) / HY_SLOW_DECAY
    max_decay = math.log(HY_DECAY_TARGET) / HY_FAST_DECAY
    return jnp.asarray(np.abs(np.linspace(min_decay, max_decay, width))[None, :], F32)


def kernel(x, c, w_ada, b_ada, w_in, b_in, sgu_ln_g, sgu_ln_b, sgu_w, sgu_b, hy_conv_w, hy_conv_b, hy_w1, hy_b1, hy_w2, hy_b2, hy_freq, hy_w3, hy_skip, w_branch_a, w_branch_b, w_o, b_o, ln1_g, ln1_b, w_m1, b_m1, w_m2, b_m2, ln2_g, ln2_b):
    bsz, seq, d = x.shape
    depth = w_ada.shape[0]
    alpha = (2.0 * depth) ** 0.25
    sgu_width = w_branch_a.shape[1]
    hy_width = w_branch_b.shape[1]
    assert seq % SGU_CHUNK == 0 and sgu_w.shape[-1] == SGU_CHUNK

    assert seq % (8 * MXU_DIM) == 0
    (t_s1, t_real, t_cplx, t_gauss), inv_tables = _dft_stage_tables(seq)
    tcol, feats = _filter_features(seq)
    deltas = _decay_rates(hy_width)

    xt = x.reshape(bsz * seq, d)
    for l in range(depth):
        mod3 = _mod_call(c, w_ada[l], b_ada[l]).reshape(bsz, 1, N_MOD * d)

        h2 = _filter_mlp_call(feats, hy_w1[l], hy_b1[l], hy_w2[l], hy_b2[l], hy_freq[l])
        kspec, w_in_16 = _filter_spec_call(h2, hy_w3[l], tcol, deltas, hy_skip[l],
                                           t_s1, t_real, t_cplx, w_in[l], hy_width)

        z = _inproj_call(xt, mod3, w_in_16, b_in[l], sgu_ln_g[l], sgu_ln_b[l],
                         seq, sgu_width, hy_width)
        yh, w_m1_16, w_m2_16 = _hyena_call(z, hy_conv_w[l], hy_conv_b[l], kspec,
                                           (t_s1, t_real, t_gauss) + inv_tables, (w_m1[l], w_m2[l]),
                                           bsz, seq, hy_width, 2 * d + 2 * sgu_width)

        gd = sgu_width // SGU_GROUPS
        bs_full = jnp.repeat(sgu_b[l].T, gd, axis=1)
        m = _merge_call(z, yh, sgu_w[l].astype(BF16), bs_full, w_branch_a[l], w_branch_b[l], d)

        x1 = _oproj_call(m, xt, mod3, w_o[l], b_o[l], ln1_g[l], ln1_b[l], seq, alpha)
        xt = _mlp_call(x1, mod3, w_m1_16, b_m1[l], w_m2_16, b_m2[l], ln2_g[l], ln2_b[l], seq, alpha)
    return xt.reshape(bsz, seq, d)
```

```python
import functools
import math

import jax
import jax.numpy as jnp
import numpy as np
from jax import lax
from jax.experimental import pallas as pl
from jax.experimental.pallas import tpu as pltpu

F32 = jnp.float32
BF16 = jnp.bfloat16

SGU_CHUNK = 128
SGU_GROUPS = 8
HY_ORDER = 2
HY_SHORT = 3
HY_EMB = 33
HY_DECAY_TARGET = 1e-2
HY_FAST_DECAY = 0.3
HY_SLOW_DECAY = 1.5
N_MOD = 6
LN_EPS = 1e-5

SUBLANES = 8
MXU_DIM = 256
VMEM_LIMIT = 48 * 1024 * 1024

MOD_COLS = 1024
INPROJ_ROWS = 1024
TOKEN_ROWS = 512
MLP_HIDDEN = 1024
EPILOGUE_ROWS = 256


def _cparams(sem):
    return pltpu.CompilerParams(dimension_semantics=sem, vmem_limit_bytes=VMEM_LIMIT)


def _resident(block_shape, index_map):
    return pl.BlockSpec(block_shape, index_map, pipeline_mode=pl.Buffered(1))


def _row_chunks(total, size):
    return [slice(r, r + size) for r in range(0, total, size)]


def _gelu(x):
    return 0.5 * x * (1.0 + lax.erf(x * (1.0 / math.sqrt(2.0))))


def _layer_norm(x, g, b):
    mu = jnp.mean(x, axis=-1, keepdims=True)
    xc = x - mu
    var = jnp.mean(xc * xc, axis=-1, keepdims=True)
    return xc * lax.rsqrt(var + LN_EPS) * g + b


def _dot_bf16x3(a, b):
    m = a.shape[0]
    a_hi = a.astype(BF16)
    b_hi = b.astype(BF16)
    a_hi32 = a_hi.astype(F32)
    b_lo = (b - b_hi.astype(F32)).astype(BF16)
    stacked = jnp.concatenate([a_hi32, a - a_hi32], axis=0).astype(BF16)
    top = jnp.dot(stacked, b_hi, preferred_element_type=F32)
    return top[:m] + top[m:] + jnp.dot(a_hi, b_lo, preferred_element_type=F32)


def _mod_kernel(c_ref, w_ref, b_ref, o_ref):
    c = c_ref[...]
    cond = c * jax.nn.sigmoid(c)
    o_ref[...] = _dot_bf16x3(cond, w_ref[...]) + b_ref[...]


def _mod_call(c, w_ada, b_ada):
    bsz, d = c.shape
    n = w_ada.shape[1]
    tn = MOD_COLS
    return pl.pallas_call(
        _mod_kernel,
        grid=(n // tn,),
        in_specs=[pl.BlockSpec((bsz, d), lambda j: (0, 0)),
                  pl.BlockSpec((d, tn), lambda j: (0, j)),
                  pl.BlockSpec((1, tn), lambda j: (0, j))],
        out_specs=pl.BlockSpec((bsz, tn), lambda j: (0, j)),
        out_shape=jax.ShapeDtypeStruct((bsz, n), F32),
        compiler_params=_cparams(("arbitrary",)),
        name="adaln_mod",
    )(c, w_ada, b_ada.reshape(1, n))


def _inproj_kernel(x_ref, sc_ref, sh_ref, w_ref, b_ref, g_ref, be_ref, o_ref, h_scr, *, seg, rc):
    j = pl.program_id(1)
    chunks = _row_chunks(x_ref.shape[0], rc)

    def emit(act, modulate=False):
        for rows in chunks:
            if modulate:
                h_scr[rows, :] = (x_ref[rows, :] * (1.0 + sc_ref[0]) + sh_ref[0]).astype(BF16)
            acc = jnp.dot(h_scr[rows, :], w_ref[...], preferred_element_type=F32) + b_ref[...]
            o_ref[rows, :] = act(acc).astype(o_ref.dtype)

    @pl.when(j == 0)
    def _():
        emit(jax.nn.sigmoid, modulate=True)

    @pl.when(jnp.logical_and(j > 0, j < seg[0]))
    def _():
        emit(jax.nn.sigmoid)

    @pl.when(jnp.logical_and(j >= seg[0], j < seg[1]))
    def _():
        emit(_gelu)

    @pl.when(jnp.logical_and(j >= seg[1], j < seg[2]))
    def _():
        emit(lambda a: _layer_norm(_gelu(a), g_ref[...], be_ref[...]))

    @pl.when(j >= seg[2])
    def _():
        emit(lambda a: a)


def _inproj_call(xt, mod3, w_in, b_in, ln_g, ln_b, seq, sgu_w, hy_w):
    t, d = xt.shape
    n = w_in.shape[1]
    tm = INPROJ_ROWS
    tn = sgu_w
    assert hy_w == sgu_w and seq % tm == 0 and d % tn == 0
    nblk = n // tn
    n_gate = 2 * d // tn
    seg = (n_gate, n_gate + 1, n_gate + 2)
    per_b = seq // tm
    wcol = lambda i, j: (0, (j + (nblk - n_gate)) % nblk)
    return pl.pallas_call(
        functools.partial(_inproj_kernel, seg=seg, rc=EPILOGUE_ROWS),
        grid=(t // tm, n // tn),
        in_specs=[pl.BlockSpec((tm, d), lambda i, j: (i, 0)),
                  pl.BlockSpec((1, 1, d), lambda i, j: (i // per_b, 0, 1)),
                  pl.BlockSpec((1, 1, d), lambda i, j: (i // per_b, 0, 0)),
                  pl.BlockSpec((d, tn), wcol),
                  pl.BlockSpec((1, tn), wcol),
                  pl.BlockSpec((1, tn), lambda i, j: (0, 0)),
                  pl.BlockSpec((1, tn), lambda i, j: (0, 0))],
        out_specs=pl.BlockSpec((tm, tn), lambda i, j: (i, j)),
        out_shape=jax.ShapeDtypeStruct((t, n), BF16),
        scratch_shapes=[pltpu.VMEM((tm, d), BF16)],
        compiler_params=_cparams(("parallel", "arbitrary")),
        name="in_proj",
    )(xt, mod3, mod3, w_in, b_in.reshape(1, n), ln_g.reshape(1, tn), ln_b.reshape(1, tn))


def _filter_mlp_kernel(f_ref, w1_ref, b1_ref, w2_ref, b2_ref, fr_ref, o_ref):
    hp = lax.Precision.HIGHEST
    fr = fr_ref[...]
    h = jnp.sin(fr * (jnp.dot(f_ref[...], w1_ref[...], preferred_element_type=F32, precision=hp)
                      + b1_ref[...]))
    o_ref[...] = jnp.sin(fr * (jnp.dot(h, w2_ref[...], preferred_element_type=F32, precision=hp)
                               + b2_ref[...]))


def _filter_mlp_call(feats, w1, b1, w2, b2, freq):
    seq = feats.shape[0]
    hid = w1.shape[1]
    return pl.pallas_call(
        _filter_mlp_kernel,
        out_shape=jax.ShapeDtypeStruct((seq, hid), F32),
        name="hyena_filter_mlp",
    )(feats, w1, b1.reshape(1, hid), w2, b2.reshape(1, hid), freq.reshape(1, hid))


DFT_GROUPS = 9
DFT_SLABS = 8
DFT_COMPS = 16
DFT_J = 32


def _packed_start(g, n2):
    return {0: 0, 8: n2}.get(g, 2 * n2 * g)


def _store_slab_block_order(dst_ref, val, t1, n2):
    step = DFT_SLABS * DFT_J
    for k in range(n2 // DFT_J):
        dst = step * k + DFT_J * t1
        dst_ref[dst:dst + DFT_J, :] = val[DFT_J * k:DFT_J * (k + 1), :].astype(dst_ref.dtype)


def _store_block_order(dst_ref, val, n2):
    for t1 in range(DFT_SLABS):
        _store_slab_block_order(dst_ref, val[n2 * t1:n2 * (t1 + 1), :], t1, n2)


def _stage1_fwd(s1_ref, src_ref, dst_ref, n2):
    step = DFT_SLABS * DFT_J
    for k in range(n2 // DFT_J):
        comps = jnp.dot(s1_ref[...], src_ref[step * k:step * (k + 1), :], preferred_element_type=F32)
        for c in range(DFT_COMPS):
            dst_ref[n2 * c + DFT_J * k:n2 * c + DFT_J * (k + 1), :] = (
                comps[DFT_J * c:DFT_J * (c + 1), :].astype(dst_ref.dtype))


def _stage2_fwd_part(g, part, tr_ref, tc_ref, src_ref, n2):
    start = _packed_start(g, n2)
    rows = slice(n2 * part, n2 * (part + 1))
    if g in (0, 8):
        return jnp.dot(tr_ref[0 if g == 0 else 1, rows, :], src_ref[start:start + n2, :],
                       preferred_element_type=F32)
    return jnp.dot(tc_ref[g - 1, rows, :], src_ref[start:start + 2 * n2, :],
                   preferred_element_type=F32)


def _dft_stage_tables(seq):
    n = 2 * seq
    n2 = n // 16
    f2 = np.arange(n2)[:, None]
    t2 = np.arange(n2)[None, :]

    def cs(g):
        ang = ((t2 * (g + 16 * f2)) % n) * (2.0 * math.pi / n)
        return np.cos(ang), np.sin(ang)

    real = np.stack([np.concatenate([c, -s], axis=0) for c, s in (cs(0), cs(8))])
    cplx = np.stack([np.block([[c, s], [-s, c]]) for c, s in (cs(g) for g in range(1, 8))])
    gauss = np.stack([np.stack([c, s - c, c + s]) for c, s in (cs(g) for g in range(1, 8))])

    t1 = np.arange(DFT_SLABS)
    rows = [np.ones(DFT_SLABS), np.where(t1 % 2 == 0, 1.0, -1.0)]
    for g in range(1, 8):
        ang = ((g * t1) % 16) * (math.pi / 8.0)
        rows += [np.cos(ang), -np.sin(ang)]
    s1 = np.kron(np.stack(rows), np.eye(DFT_J))
    fwd = (s1, real, cplx, gauss)
    inv = (s1.T, np.swapaxes(real, 1, 2), np.swapaxes(gauss, 2, 3))
    as_bf16 = lambda t: jnp.asarray(np.ascontiguousarray(t, dtype=np.float32)).astype(BF16)
    return tuple(map(as_bf16, fwd)), tuple(map(as_bf16, inv))


def _filter_spec_kernel(h_ref, wf_ref, wb_ref, t_ref, dl_ref, skip_ref, s1_ref, tr_ref, tc_ref,
                        cast_ref, k_ref, cast_out, e_scr, o_scr, ge_scr, go_scr):
    seq = h_ref.shape[0]
    n2 = seq // 8
    cast_out[...] = cast_ref[...].astype(cast_out.dtype)
    window = jnp.exp(-t_ref[...] * dl_ref[...])
    _store_block_order(e_scr, _dot_bf16x3(h_ref[...], wf_ref[...] + wb_ref[...]) * window, n2)
    _store_block_order(o_scr, _dot_bf16x3(h_ref[...], wf_ref[...] - wb_ref[...]) * window, n2)
    _stage1_fwd(s1_ref, e_scr, ge_scr, n2)
    _stage1_fwd(s1_ref, o_scr, go_scr, n2)
    for g in range(DFT_GROUPS):
        wgt = (1.0 if g in (0, 8) else 2.0) / (2 * seq)
        rows = slice(2 * n2 * g, 2 * n2 * g + n2)
        rows_im = slice(2 * n2 * g + n2, 2 * n2 * (g + 1))
        k_ref[0, rows, :] = ((_stage2_fwd_part(g, 0, tr_ref, tc_ref, ge_scr, n2) + skip_ref[0])
                             * wgt).astype(k_ref.dtype)
        k_ref[0, rows_im, :] = (_stage2_fwd_part(g, 1, tr_ref, tc_ref, go_scr, n2)
                                * wgt).astype(k_ref.dtype)


def _filter_spec_call(h2, w3, tcol, deltas, skip, t_s1, t_real, t_cplx, to_cast, width):
    seq, hid = h2.shape
    n2 = seq // 8
    cb = MXU_DIM
    ncb = width // cb
    per_dir = HY_ORDER * ncb
    rows = DFT_GROUPS * 2 * n2
    const = lambda n, k: (0, 0, 0)
    steps = HY_ORDER * ncb
    assert to_cast.shape[0] % (steps * 2 * SUBLANES) == 0
    slab = pl.BlockSpec((to_cast.shape[0] // steps, to_cast.shape[1]), lambda n, k: (n * ncb + k, 0))
    return pl.pallas_call(
        _filter_spec_kernel,
        grid=(HY_ORDER, ncb),
        in_specs=[_resident((seq, hid), lambda n, k: (0, 0)),
                  pl.BlockSpec((hid, cb), lambda n, k: (0, n * ncb + k)),
                  pl.BlockSpec((hid, cb), lambda n, k: (0, per_dir + n * ncb + k)),
                  _resident((seq, 1), lambda n, k: (0, 0)),
                  pl.BlockSpec((1, cb), lambda n, k: (0, k)),
                  pl.BlockSpec((1, 1, cb), lambda n, k: (n, 0, k)),
                  _resident(t_s1.shape, lambda n, k: (0, 0)),
                  _resident(t_real.shape, const), _resident(t_cplx.shape, const), slab],
        out_specs=[pl.BlockSpec((1, rows, cb), lambda n, k: (n, 0, k)), slab],
        out_shape=[jax.ShapeDtypeStruct((HY_ORDER, rows, width), BF16),
                   jax.ShapeDtypeStruct(to_cast.shape, BF16)],
        scratch_shapes=[pltpu.VMEM((seq, cb), BF16), pltpu.VMEM((seq, cb), BF16),
                        pltpu.VMEM((2 * seq, cb), BF16), pltpu.VMEM((2 * seq, cb), BF16)],
        compiler_params=_cparams(("arbitrary", "arbitrary")),
        name="hyena_filter_spectrum",
    )(h2, w3, w3, tcol, deltas, skip.reshape(HY_ORDER, 1, width), t_s1, t_real, t_cplx, to_cast)


def _short_conv(z, w, b):
    seq = z.shape[0]
    prev = pltpu.roll(z, 1, 0)
    nxt = pltpu.roll(z, seq - 1, 0)
    row = lax.broadcasted_iota(jnp.int32, (SUBLANES, z.shape[1]), 0)
    prev = jnp.concatenate([jnp.where(row == 0, 0.0, prev[:SUBLANES]), prev[SUBLANES:]], axis=0)
    nxt = jnp.concatenate([nxt[:-SUBLANES], jnp.where(row == SUBLANES - 1, 0.0, nxt[-SUBLANES:])],
                          axis=0)
    return w[0:1] * prev + w[1:2] * z + w[2:3] * nxt + b


HYENA_INPUTS = 16


def _hyena_kernel(*refs, n_cast):
    (v_ref, x1_ref, x2_ref, wv_ref, w1_ref, w2_ref, bv_ref, b1_ref, b2_ref,
     k_ref, s1_ref, tr_ref, tg_ref, s1t_ref, ur_ref, ug_ref) = refs[:HYENA_INPUTS]
    cast_in = refs[HYENA_INPUTS:HYENA_INPUTS + n_cast]
    o_ref = refs[HYENA_INPUTS + n_cast]
    cast_out = refs[HYENA_INPUTS + n_cast + 1:HYENA_INPUTS + 2 * n_cast + 1]
    y16_scr, gate_scr, g_scr, yc_scr, h_scr = refs[HYENA_INPUTS + 2 * n_cast + 1:]
    seq, cb = v_ref.shape
    n2 = seq // DFT_SLABS
    step = DFT_SLABS * DFT_J
    dot = functools.partial(jnp.dot, preferred_element_type=F32)
    for src, dst in zip(cast_in, cast_out):
        dst[...] = src[...].astype(dst.dtype)

    conv3 = lambda z_ref, w_ref, b_ref: _short_conv(z_ref[...].astype(F32), w_ref[...], b_ref[...])
    _store_block_order(y16_scr, conv3(v_ref, wv_ref, bv_ref), n2)
    _store_block_order(gate_scr.at[0], conv3(x1_ref, w1_ref, b1_ref), n2)
    _store_block_order(gate_scr.at[1], conv3(x2_ref, w2_ref, b2_ref), n2)

    for n in range(HY_ORDER):
        last = n + 1 == HY_ORDER
        _stage1_fwd(s1_ref, y16_scr, g_scr, n2)
        for g in range(DFT_GROUPS):
            start = _packed_start(g, n2)
            if g in (0, 8):
                x = dot(tr_ref[0 if g == 0 else 1], g_scr[start:start + n2, :])
                xr, xi = x[:n2], x[n2:]
            else:
                a = g_scr[start:start + n2, :]
                b = g_scr[start + n2:start + 2 * n2, :]
                p1 = dot(tg_ref[g - 1, 0], a + b)
                xr = p1 + dot(tg_ref[g - 1, 1], b)
                xi = p1 - dot(tg_ref[g - 1, 2], a)
            xr, xi = xr.astype(BF16), xi.astype(BF16)
            base = 2 * n2 * g
            kr = k_ref[n, base:base + n2, :]
            ki = k_ref[n, base + n2:base + 2 * n2, :]
            yc_scr[base:base + n2, :] = xr * kr - xi * ki
            yc_scr[base + n2:base + 2 * n2, :] = xr * ki + xi * kr
        for g in range(DFT_GROUPS):
            base = 2 * n2 * g
            if g in (0, 8):
                parts = [(0 if g == 0 else 1,
                          dot(ur_ref[0 if g == 0 else 1], yc_scr[base:base + 2 * n2, :]))]
            else:
                a = yc_scr[base:base + n2, :]
                b = yc_scr[base + n2:base + 2 * n2, :]
                q1 = dot(ug_ref[g - 1, 0], a + b)
                parts = [(2 * g, q1 - dot(ug_ref[g - 1, 2], b)),
                         (2 * g + 1, q1 + dot(ug_ref[g - 1, 1], a))]
            for c, hp in parts:
                for k in range(n2 // DFT_J):
                    dst = DFT_COMPS * DFT_J * k + DFT_J * c
                    h_scr[dst:dst + DFT_J, :] = hp[DFT_J * k:DFT_J * (k + 1), :].astype(BF16)
        for k in range(n2 // DFT_J):
            rows = slice(step * k, step * (k + 1))
            conv = jnp.dot(s1t_ref[...], h_scr[DFT_COMPS * DFT_J * k:DFT_COMPS * DFT_J * (k + 1), :],
                           preferred_element_type=F32)
            y_new = (gate_scr[n, rows, :] * conv).astype(BF16)
            if last:
                for t1 in range(DFT_SLABS):
                    dst = n2 * t1 + DFT_J * k
                    o_ref[dst:dst + DFT_J, :] = y_new[DFT_J * t1:DFT_J * (t1 + 1), :]
            else:
                y16_scr[rows, :] = y_new


def _hyena_call(z, conv_w, conv_b, kspec, tables, to_cast, bsz, seq, width, col0):
    cb = MXU_DIM
    ncb = width // cb
    base = col0 // cb
    zspec = lambda part: pl.BlockSpec((seq, cb), lambda k, b: (b, base + part * ncb + k))
    wspec = lambda part: pl.BlockSpec((HY_SHORT, cb), lambda k, b: (0, part * ncb + k))
    bspec = lambda part: pl.BlockSpec((1, cb), lambda k, b: (0, part * ncb + k))
    cb_all = conv_b.reshape(1, -1)
    const = lambda t: (lambda k, b: (0,) * t.ndim)
    steps = ncb * bsz
    slab = lambda w: pl.BlockSpec((w.shape[0] // steps, w.shape[1]), lambda k, b: (k * bsz + b, 0))
    assert all(w.shape[0] % (steps * 2 * SUBLANES) == 0 for w in to_cast)
    return pl.pallas_call(
        functools.partial(_hyena_kernel, n_cast=len(to_cast)),
        grid=(ncb, bsz),
        in_specs=[zspec(0), zspec(1), zspec(2), wspec(0), wspec(1), wspec(2),
                  bspec(0), bspec(1), bspec(2),
                  pl.BlockSpec((HY_ORDER, kspec.shape[1], cb), lambda k, b: (0, 0, k))]
                 + [_resident(t.shape, const(t)) for t in tables] + [slab(w) for w in to_cast],
        out_specs=[pl.BlockSpec((seq, cb), lambda k, b: (b, k))] + [slab(w) for w in to_cast],
        out_shape=[jax.ShapeDtypeStruct((bsz * seq, width), BF16)]
                  + [jax.ShapeDtypeStruct(w.shape, BF16) for w in to_cast],
        scratch_shapes=[pltpu.VMEM((seq, cb), BF16),
                        pltpu.VMEM((HY_ORDER, seq, cb), F32),
                        pltpu.VMEM((2 * seq, cb), BF16),
                        pltpu.VMEM((kspec.shape[1], cb), BF16),
                        pltpu.VMEM((2 * seq, cb), BF16)],
        compiler_params=_cparams(("arbitrary", "arbitrary")),
        name="hyena_mixer",
    )(z, z, z, conv_w, conv_w, conv_w, cb_all, cb_all, cb_all, kspec, *tables, *to_cast)


def _merge_kernel(u_ref, vn_ref, yh_ref, ga_ref, gb_ref, ws_ref, bs_ref, wa_ref, wb_ref, o_ref,
                  s_scr):
    tm = u_ref.shape[0]
    gd = u_ref.shape[1] // SGU_GROUPS
    for n in range(tm // SGU_CHUNK):
        rows = slice(n * SGU_CHUNK, (n + 1) * SGU_CHUNK)
        for g in range(SGU_GROUPS):
            cols = slice(g * gd, (g + 1) * gd)
            mixed = jnp.dot(ws_ref[g], vn_ref[rows, cols], preferred_element_type=F32)
            s_scr[rows, cols] = (u_ref[rows, cols].astype(F32)
                                 * (mixed + bs_ref[:, cols])).astype(BF16)
    y_a = jnp.dot(s_scr[...], wa_ref[...], preferred_element_type=F32)
    y_b = jnp.dot(yh_ref[...], wb_ref[...], preferred_element_type=F32)
    o_ref[...] = (ga_ref[...].astype(F32) * y_a + gb_ref[...].astype(F32) * y_b).astype(o_ref.dtype)


def _merge_call(z, yh, sgu_w16, sgu_bfull, w_a, w_b, d):
    t = z.shape[0]
    wdt = w_a.shape[0]
    tm = TOKEN_ROWS
    u_blk = 2 * d // wdt
    return pl.pallas_call(
        _merge_kernel,
        grid=(t // tm,),
        in_specs=[pl.BlockSpec((tm, wdt), lambda i: (i, u_blk)),
                  pl.BlockSpec((tm, wdt), lambda i: (i, u_blk + 1)),
                  pl.BlockSpec((tm, wdt), lambda i: (i, 0)),
                  pl.BlockSpec((tm, d), lambda i: (i, 0)),
                  pl.BlockSpec((tm, d), lambda i: (i, 1)),
                  _resident(sgu_w16.shape, lambda i: (0, 0, 0)),
                  _resident(sgu_bfull.shape, lambda i: (0, 0)),
                  _resident(w_a.shape, lambda i: (0, 0)),
                  _resident(w_b.shape, lambda i: (0, 0))],
        out_specs=pl.BlockSpec((tm, d), lambda i: (i, 0)),
        out_shape=jax.ShapeDtypeStruct((t, d), BF16),
        scratch_shapes=[pltpu.VMEM((tm, wdt), BF16)],
        compiler_params=_cparams(("parallel",)),
        name="gated_merge",
    )(z, z, yh, z, z, sgu_w16, sgu_bfull, w_a, w_b)


def _oproj_kernel(m_ref, x_ref, g1_ref, w_ref, b_ref, lg_ref, lb_ref, o_ref, *, alpha, rc):
    for rows in _row_chunks(m_ref.shape[0], rc):
        mix = jnp.dot(m_ref[rows, :], w_ref[...], preferred_element_type=F32) + b_ref[...]
        o_ref[rows, :] = _layer_norm(alpha * x_ref[rows, :] + g1_ref[0] * mix,
                                     lg_ref[...], lb_ref[...])


def _oproj_call(m, xt, mod3, w_o, b_o, ln_g, ln_b, seq, alpha):
    t, d = xt.shape
    tm = TOKEN_ROWS
    per_b = seq // tm
    row = lambda i: (0, 0)
    return pl.pallas_call(
        functools.partial(_oproj_kernel, alpha=alpha, rc=EPILOGUE_ROWS),
        grid=(t // tm,),
        in_specs=[pl.BlockSpec((tm, d), lambda i: (i, 0)),
                  pl.BlockSpec((tm, d), lambda i: (i, 0)),
                  pl.BlockSpec((1, 1, d), lambda i: (i // per_b, 0, 2)),
                  _resident((d, d), row),
                  pl.BlockSpec((1, d), row), pl.BlockSpec((1, d), row), pl.BlockSpec((1, d), row)],
        out_specs=pl.BlockSpec((tm, d), lambda i: (i, 0)),
        out_shape=jax.ShapeDtypeStruct((t, d), F32),
        compiler_params=_cparams(("parallel",)),
        name="out_proj_ln",
    )(m, xt, mod3, w_o, b_o.reshape(1, d), ln_g.reshape(1, d), ln_b.reshape(1, d))


def _mlp_kernel(x_ref, sc_ref, sh_ref, g2_ref, w1_ref, b1_ref, w2_ref, b2_ref, lg_ref, lb_ref,
                o_ref, h_scr, acc_scr, *, alpha, rc):
    j = pl.program_id(1)
    last = pl.num_programs(1) - 1
    chunks = _row_chunks(x_ref.shape[0], rc)

    def partial_out(rows):
        hid = jnp.dot(h_scr[rows, :], w1_ref[...], preferred_element_type=F32) + b1_ref[...]
        hid = jnp.square(jnp.maximum(hid, 0.0)).astype(BF16)
        return jnp.dot(hid, w2_ref[...], preferred_element_type=F32)

    @pl.when(j == 0)
    def _():
        for rows in chunks:
            h_scr[rows, :] = (x_ref[rows, :] * (1.0 + sc_ref[0]) + sh_ref[0]).astype(BF16)
            acc_scr[rows, :] = partial_out(rows)

    @pl.when(jnp.logical_and(j > 0, j < last))
    def _():
        acc_scr[...] += partial_out(slice(None))

    @pl.when(j == last)
    def _():
        for rows in chunks:
            f = acc_scr[rows, :] + partial_out(rows) + b2_ref[...]
            o_ref[rows, :] = _layer_norm(alpha * x_ref[rows, :] + g2_ref[0] * f,
                                         lg_ref[...], lb_ref[...])


def _mlp_call(x1, mod3, w_m1, b_m1, w_m2, b_m2, ln_g, ln_b, seq, alpha):
    t, d = x1.shape
    dff = w_m1.shape[1]
    tm = TOKEN_ROWS
    tf = MLP_HIDDEN
    per_b = seq // tm
    row = lambda i, j: (0, 0)
    return pl.pallas_call(
        functools.partial(_mlp_kernel, alpha=alpha, rc=EPILOGUE_ROWS),
        grid=(t // tm, dff // tf),
        in_specs=[pl.BlockSpec((tm, d), lambda i, j: (i, 0)),
                  pl.BlockSpec((1, 1, d), lambda i, j: (i // per_b, 0, 4)),
                  pl.BlockSpec((1, 1, d), lambda i, j: (i // per_b, 0, 3)),
                  pl.BlockSpec((1, 1, d), lambda i, j: (i // per_b, 0, 5)),
                  pl.BlockSpec((d, tf), lambda i, j: (0, j)),
                  pl.BlockSpec((1, tf), lambda i, j: (0, j)),
                  pl.BlockSpec((tf, d), lambda i, j: (j, 0)),
                  pl.BlockSpec((1, d), row), pl.BlockSpec((1, d), row), pl.BlockSpec((1, d), row)],
        out_specs=pl.BlockSpec((tm, d), lambda i, j: (i, 0)),
        out_shape=jax.ShapeDtypeStruct((t, d), F32),
        scratch_shapes=[pltpu.VMEM((tm, d), BF16), pltpu.VMEM((tm, d), F32)],
        compiler_params=_cparams(("parallel", "arbitrary")),
        name="mlp_ln",
    )(x1, mod3, mod3, mod3, w_m1, b_m1.reshape(1, dff), w_m2, b_m2.reshape(1, d),
      ln_g.reshape(1, d), ln_b.reshape(1, d))


def _filter_features(seq):
    bands_n = (HY_EMB - 1) // 2
    t = np.linspace(0.0, 1.0, seq)[:, None]
    omega = 2.0 * math.pi * np.arange(seq)[:, None] / seq
    bands = np.linspace(1e-4, bands_n - 1, bands_n)[None, :]
    feats = np.concatenate([t, np.cos(bands * omega), -np.sin(bands * omega)], axis=-1)
    return jnp.asarray(t, F32), jnp.asarray(feats, F32)


def _decay_rates(width):
    min_decay = math.log(HY_DECAY_TARGET) / HY_SLOW_DECAY
    max_decay = math.log(HY_DECAY_TARGET) / HY_FAST_DECAY
    return jnp.asarray(np.abs(np.linspace(min_decay, max_decay, width))[None, :], F32)


def kernel(x, c, w_ada, b_ada, w_in, b_in, sgu_ln_g, sgu_ln_b, sgu_w, sgu_b, hy_conv_w, hy_conv_b, hy_w1, hy_b1, hy_w2, hy_b2, hy_freq, hy_w3, hy_skip, w_branch_a, w_branch_b, w_o, b_o, ln1_g, ln1_b, w_m1, b_m1, w_m2, b_m2, ln2_g, ln2_b):
    bsz, seq, d = x.shape
    depth = w_ada.shape[0]
    alpha = (2.0 * depth) ** 0.25
    sgu_width = w_branch_a.shape[1]
    hy_width = w_branch_b.shape[1]
    assert seq % SGU_CHUNK == 0 and sgu_w.shape[-1] == SGU_CHUNK

    assert seq % (8 * MXU_DIM) == 0
    (t_s1, t_real, t_cplx, t_gauss), inv_tables = _dft_stage_tables(seq)
    tcol, feats = _filter_features(seq)
    deltas = _decay_rates(hy_width)

    xt = x.reshape(bsz * seq, d)
    for l in range(depth):
        mod3 = _mod_call(c, w_ada[l], b_ada[l]).reshape(bsz, 1, N_MOD * d)

        h2 = _filter_mlp_call(feats, hy_w1[l], hy_b1[l], hy_w2[l], hy_b2[l], hy_freq[l])
        kspec, w_in_16 = _filter_spec_call(h2, hy_w3[l], tcol, deltas, hy_skip[l],
                                           t_s1, t_real, t_cplx, w_in[l], hy_width)

        z = _inproj_call(xt, mod3, w_in_16, b_in[l], sgu_ln_g[l], sgu_ln_b[l],
                         seq, sgu_width, hy_width)
        later_weights = (w_branch_a[l], w_branch_b[l], w_o[l], w_m1[l], w_m2[l])
        yh, w_a_16, w_b_16, w_o_16, w_m1_16, w_m2_16 = _hyena_call(
            z, hy_conv_w[l], hy_conv_b[l], kspec, (t_s1, t_real, t_gauss) + inv_tables,
            later_weights, bsz, seq, hy_width, 2 * d + 2 * sgu_width)

        gd = sgu_width // SGU_GROUPS
        bs_full = jnp.repeat(sgu_b[l].T, gd, axis=1)
        m = _merge_call(z, yh, sgu_w[l].astype(BF16), bs_full, w_a_16, w_b_16, d)

        x1 = _oproj_call(m, xt, mod3, w_o_16, b_o[l], ln1_g[l], ln1_b[l], seq, alpha)
        xt = _mlp_call(x1, mod3, w_m1_16, b_m1[l], w_m2_16, b_m2[l], ln2_g[l], ln2_b[l], seq, alpha)
    return xt.reshape(bsz, seq, d)
```

```python
import functools
import math

import jax
import jax.numpy as jnp
import numpy as np
from jax import lax
from jax.experimental import pallas as pl
from jax.experimental.pallas import tpu as pltpu

F32 = jnp.float32
BF16 = jnp.bfloat16

SGU_CHUNK = 128
SGU_GROUPS = 8
HY_ORDER = 2
HY_SHORT = 3
HY_EMB = 33
HY_DECAY_TARGET = 1e-2
HY_FAST_DECAY = 0.3
HY_SLOW_DECAY = 1.5
N_MOD = 6
LN_EPS = 1e-5

SUBLANES = 8
MXU_DIM = 256
VMEM_LIMIT = 48 * 1024 * 1024

MOD_COLS = 1024
INPROJ_ROWS = 1024
TOKEN_ROWS = 512
MLP_HIDDEN = 1024
EPILOGUE_ROWS = 256


def _cparams(sem):
    return pltpu.CompilerParams(dimension_semantics=sem, vmem_limit_bytes=VMEM_LIMIT)


def _resident(block_shape, index_map):
    return pl.BlockSpec(block_shape, index_map, pipeline_mode=pl.Buffered(1))


def _row_chunks(total, size):
    return [slice(r, r + size) for r in range(0, total, size)]


def _gelu(x):
    return 0.5 * x * (1.0 + lax.erf(x * (1.0 / math.sqrt(2.0))))


def _layer_norm(x, g, b):
    mu = jnp.mean(x, axis=-1, keepdims=True)
    xc = x - mu
    var = jnp.mean(xc * xc, axis=-1, keepdims=True)
    return xc * lax.rsqrt(var + LN_EPS) * g + b


def _dot_bf16x3(a, b):
    m = a.shape[0]
    a_hi = a.astype(BF16)
    b_hi = b.astype(BF16)
    a_hi32 = a_hi.astype(F32)
    b_lo = (b - b_hi.astype(F32)).astype(BF16)
    stacked = jnp.concatenate([a_hi32, a - a_hi32], axis=0).astype(BF16)
    top = jnp.dot(stacked, b_hi, preferred_element_type=F32)
    return top[:m] + top[m:] + jnp.dot(a_hi, b_lo, preferred_element_type=F32)


def _mod_kernel(c_ref, w_ref, b_ref, o_ref):
    c = c_ref[...]
    cond = c * jax.nn.sigmoid(c)
    o_ref[...] = _dot_bf16x3(cond, w_ref[...]) + b_ref[...]


def _mod_call(c, w_ada, b_ada):
    bsz, d = c.shape
    n = w_ada.shape[1]
    tn = MOD_COLS
    return pl.pallas_call(
        _mod_kernel,
        grid=(n // tn,),
        in_specs=[pl.BlockSpec((bsz, d), lambda j: (0, 0)),
                  pl.BlockSpec((d, tn), lambda j: (0, j)),
                  pl.BlockSpec((1, tn), lambda j: (0, j))],
        out_specs=pl.BlockSpec((bsz, tn), lambda j: (0, j)),
        out_shape=jax.ShapeDtypeStruct((bsz, n), F32),
        compiler_params=_cparams(("arbitrary",)),
        name="adaln_mod",
    )(c, w_ada, b_ada.reshape(1, n))


def _inproj_kernel(x_ref, sc_ref, sh_ref, w_ref, b_ref, g_ref, be_ref, o_ref, h_scr, *, seg, rc):
    j = pl.program_id(1)
    chunks = _row_chunks(x_ref.shape[0], rc)

    def emit(act, modulate=False):
        for rows in chunks:
            if modulate:
                h_scr[rows, :] = (x_ref[rows, :] * (1.0 + sc_ref[0]) + sh_ref[0]).astype(BF16)
            acc = jnp.dot(h_scr[rows, :], w_ref[...], preferred_element_type=F32) + b_ref[...]
            o_ref[rows, :] = act(acc).astype(o_ref.dtype)

    @pl.when(j == 0)
    def _():
        emit(jax.nn.sigmoid, modulate=True)

    @pl.when(jnp.logical_and(j > 0, j < seg[0]))
    def _():
        emit(jax.nn.sigmoid)

    @pl.when(jnp.logical_and(j >= seg[0], j < seg[1]))
    def _():
        emit(_gelu)

    @pl.when(jnp.logical_and(j >= seg[1], j < seg[2]))
    def _():
        emit(lambda a: _layer_norm(_gelu(a), g_ref[...], be_ref[...]))

    @pl.when(j >= seg[2])
    def _():
        emit(lambda a: a)


def _inproj_call(xt, mod3, w_in, b_in, ln_g, ln_b, seq, sgu_w, hy_w):
    t, d = xt.shape
    n = w_in.shape[1]
    tm = INPROJ_ROWS
    tn = sgu_w
    assert hy_w == sgu_w and seq % tm == 0 and d % tn == 0
    nblk = n // tn
    n_gate = 2 * d // tn
    seg = (n_gate, n_gate + 1, n_gate + 2)
    per_b = seq // tm
    wcol = lambda i, j: (0, (j + (nblk - n_gate)) % nblk)
    return pl.pallas_call(
        functools.partial(_inproj_kernel, seg=seg, rc=EPILOGUE_ROWS),
        grid=(t // tm, n // tn),
        in_specs=[pl.BlockSpec((tm, d), lambda i, j: (i, 0)),
                  pl.BlockSpec((1, 1, d), lambda i, j: (i // per_b, 0, 1)),
                  pl.BlockSpec((1, 1, d), lambda i, j: (i // per_b, 0, 0)),
                  pl.BlockSpec((d, tn), wcol),
                  pl.BlockSpec((1, tn), wcol),
                  pl.BlockSpec((1, tn), lambda i, j: (0, 0)),
                  pl.BlockSpec((1, tn), lambda i, j: (0, 0))],
        out_specs=pl.BlockSpec((tm, tn), lambda i, j: (i, j)),
        out_shape=jax.ShapeDtypeStruct((t, n), BF16),
        scratch_shapes=[pltpu.VMEM((tm, d), BF16)],
        compiler_params=_cparams(("parallel", "arbitrary")),
        name="in_proj",
    )(xt, mod3, mod3, w_in, b_in.reshape(1, n), ln_g.reshape(1, tn), ln_b.reshape(1, tn))


def _filter_mlp(f_ref, w1_ref, b1_ref, w2_ref, b2_ref, fr_ref):
    hp = lax.Precision.HIGHEST
    fr = fr_ref[...]
    h = jnp.sin(fr * (jnp.dot(f_ref[...], w1_ref[...], preferred_element_type=F32, precision=hp)
                      + b1_ref[...]))
    return jnp.sin(fr * (jnp.dot(h, w2_ref[...], preferred_element_type=F32, precision=hp)
                         + b2_ref[...]))


DFT_GROUPS = 9
DFT_SLABS = 8
DFT_COMPS = 16
DFT_J = 32


def _packed_start(g, n2):
    return {0: 0, 8: n2}.get(g, 2 * n2 * g)


def _store_slab_block_order(dst_ref, val, t1, n2):
    step = DFT_SLABS * DFT_J
    for k in range(n2 // DFT_J):
        dst = step * k + DFT_J * t1
        dst_ref[dst:dst + DFT_J, :] = val[DFT_J * k:DFT_J * (k + 1), :].astype(dst_ref.dtype)


def _store_block_order(dst_ref, val, n2):
    for t1 in range(DFT_SLABS):
        _store_slab_block_order(dst_ref, val[n2 * t1:n2 * (t1 + 1), :], t1, n2)


def _stage1_fwd(s1_ref, src_ref, dst_ref, n2):
    step = DFT_SLABS * DFT_J
    for k in range(n2 // DFT_J):
        comps = jnp.dot(s1_ref[...], src_ref[step * k:step * (k + 1), :], preferred_element_type=F32)
        for c in range(DFT_COMPS):
            dst_ref[n2 * c + DFT_J * k:n2 * c + DFT_J * (k + 1), :] = (
                comps[DFT_J * c:DFT_J * (c + 1), :].astype(dst_ref.dtype))


def _stage2_fwd_part(g, part, tr_ref, tc_ref, src_ref, n2):
    start = _packed_start(g, n2)
    rows = slice(n2 * part, n2 * (part + 1))
    if g in (0, 8):
        return jnp.dot(tr_ref[0 if g == 0 else 1, rows, :], src_ref[start:start + n2, :],
                       preferred_element_type=F32)
    return jnp.dot(tc_ref[g - 1, rows, :], src_ref[start:start + 2 * n2, :],
                   preferred_element_type=F32)


def _dft_stage_tables(seq):
    n = 2 * seq
    n2 = n // 16
    f2 = np.arange(n2)[:, None]
    t2 = np.arange(n2)[None, :]

    def cs(g):
        ang = ((t2 * (g + 16 * f2)) % n) * (2.0 * math.pi / n)
        return np.cos(ang), np.sin(ang)

    real = np.stack([np.concatenate([c, -s], axis=0) for c, s in (cs(0), cs(8))])
    cplx = np.stack([np.block([[c, s], [-s, c]]) for c, s in (cs(g) for g in range(1, 8))])
    gauss = np.stack([np.stack([c, s - c, c + s]) for c, s in (cs(g) for g in range(1, 8))])

    t1 = np.arange(DFT_SLABS)
    rows = [np.ones(DFT_SLABS), np.where(t1 % 2 == 0, 1.0, -1.0)]
    for g in range(1, 8):
        ang = ((g * t1) % 16) * (math.pi / 8.0)
        rows += [np.cos(ang), -np.sin(ang)]
    s1 = np.kron(np.stack(rows), np.eye(DFT_J))
    fwd = (s1, real, cplx, gauss)
    inv = (s1.T, np.swapaxes(real, 1, 2), np.swapaxes(gauss, 2, 3))
    as_bf16 = lambda t: jnp.asarray(np.ascontiguousarray(t, dtype=np.float32)).astype(BF16)
    return tuple(map(as_bf16, fwd)), tuple(map(as_bf16, inv))


def _filter_spec_kernel(f_ref, w1_ref, b1_ref, w2_ref, b2_ref, fr_ref,
                        wf_ref, wb_ref, t_ref, dl_ref, skip_ref, s1_ref, tr_ref, tc_ref,
                        cast_ref, k_ref, cast_out, h_ref, e_scr, o_scr, ge_scr, go_scr):
    seq = h_ref.shape[0]
    n2 = seq // 8
    cast_out[...] = cast_ref[...].astype(cast_out.dtype)

    @pl.when(jnp.logical_and(pl.program_id(0) == 0, pl.program_id(1) == 0))
    def _():
        h_ref[...] = _filter_mlp(f_ref, w1_ref, b1_ref, w2_ref, b2_ref, fr_ref)

    window = jnp.exp(-t_ref[...] * dl_ref[...])
    _store_block_order(e_scr, _dot_bf16x3(h_ref[...], wf_ref[...] + wb_ref[...]) * window, n2)
    _store_block_order(o_scr, _dot_bf16x3(h_ref[...], wf_ref[...] - wb_ref[...]) * window, n2)
    _stage1_fwd(s1_ref, e_scr, ge_scr, n2)
    _stage1_fwd(s1_ref, o_scr, go_scr, n2)
    for g in range(DFT_GROUPS):
        wgt = (1.0 if g in (0, 8) else 2.0) / (2 * seq)
        rows = slice(2 * n2 * g, 2 * n2 * g + n2)
        rows_im = slice(2 * n2 * g + n2, 2 * n2 * (g + 1))
        k_ref[0, rows, :] = ((_stage2_fwd_part(g, 0, tr_ref, tc_ref, ge_scr, n2) + skip_ref[0])
                             * wgt).astype(k_ref.dtype)
        k_ref[0, rows_im, :] = (_stage2_fwd_part(g, 1, tr_ref, tc_ref, go_scr, n2)
                                * wgt).astype(k_ref.dtype)


def _filter_spec_call(feats, w1, b1, w2, b2, freq, w3, tcol, deltas, skip, t_s1, t_real, t_cplx,
                      to_cast, width):
    seq, emb = feats.shape
    hid = w1.shape[1]
    n2 = seq // 8
    whole = lambda a: _resident(a.shape, lambda n, k: (0,) * a.ndim)
    mlp_args = (feats, w1, b1.reshape(1, hid), w2, b2.reshape(1, hid), freq.reshape(1, hid))
    cb = MXU_DIM
    ncb = width // cb
    per_dir = HY_ORDER * ncb
    rows = DFT_GROUPS * 2 * n2
    const = lambda n, k: (0, 0, 0)
    steps = HY_ORDER * ncb
    assert to_cast.shape[0] % (steps * 2 * SUBLANES) == 0
    slab = pl.BlockSpec((to_cast.shape[0] // steps, to_cast.shape[1]), lambda n, k: (n * ncb + k, 0))
    return pl.pallas_call(
        _filter_spec_kernel,
        grid=(HY_ORDER, ncb),
        in_specs=[whole(a) for a in mlp_args] + [
                  pl.BlockSpec((hid, cb), lambda n, k: (0, n * ncb + k)),
                  pl.BlockSpec((hid, cb), lambda n, k: (0, per_dir + n * ncb + k)),
                  _resident((seq, 1), lambda n, k: (0, 0)),
                  pl.BlockSpec((1, cb), lambda n, k: (0, k)),
                  pl.BlockSpec((1, 1, cb), lambda n, k: (n, 0, k)),
                  _resident(t_s1.shape, lambda n, k: (0, 0)),
                  _resident(t_real.shape, const), _resident(t_cplx.shape, const), slab],
        out_specs=[pl.BlockSpec((1, rows, cb), lambda n, k: (n, 0, k)), slab],
        out_shape=[jax.ShapeDtypeStruct((HY_ORDER, rows, width), BF16),
                   jax.ShapeDtypeStruct(to_cast.shape, BF16)],
        scratch_shapes=[pltpu.VMEM((seq, hid), F32),
                        pltpu.VMEM((seq, cb), BF16), pltpu.VMEM((seq, cb), BF16),
                        pltpu.VMEM((2 * seq, cb), BF16), pltpu.VMEM((2 * seq, cb), BF16)],
        compiler_params=_cparams(("arbitrary", "arbitrary")),
        name="hyena_filter_spectrum",
    )(*mlp_args, w3, w3, tcol, deltas, skip.reshape(HY_ORDER, 1, width), t_s1, t_real, t_cplx,
      to_cast)


def _short_conv(z, w, b):
    seq = z.shape[0]
    prev = pltpu.roll(z, 1, 0)
    nxt = pltpu.roll(z, seq - 1, 0)
    row = lax.broadcasted_iota(jnp.int32, (SUBLANES, z.shape[1]), 0)
    prev = jnp.concatenate([jnp.where(row == 0, 0.0, prev[:SUBLANES]), prev[SUBLANES:]], axis=0)
    nxt = jnp.concatenate([nxt[:-SUBLANES], jnp.where(row == SUBLANES - 1, 0.0, nxt[-SUBLANES:])],
                          axis=0)
    return w[0:1] * prev + w[1:2] * z + w[2:3] * nxt + b


HYENA_INPUTS = 16


def _hyena_kernel(*refs, n_cast):
    (v_ref, x1_ref, x2_ref, wv_ref, w1_ref, w2_ref, bv_ref, b1_ref, b2_ref,
     k_ref, s1_ref, tr_ref, tg_ref, s1t_ref, ur_ref, ug_ref) = refs[:HYENA_INPUTS]
    cast_in = refs[HYENA_INPUTS:HYENA_INPUTS + n_cast]
    o_ref = refs[HYENA_INPUTS + n_cast]
    cast_out = refs[HYENA_INPUTS + n_cast + 1:HYENA_INPUTS + 2 * n_cast + 1]
    y16_scr, gate_scr, g_scr, yc_scr, h_scr = refs[HYENA_INPUTS + 2 * n_cast + 1:]
    seq, cb = v_ref.shape
    n2 = seq // DFT_SLABS
    step = DFT_SLABS * DFT_J
    dot = functools.partial(jnp.dot, preferred_element_type=F32)
    for src, dst in zip(cast_in, cast_out):
        dst[...] = src[...].astype(dst.dtype)

    conv3 = lambda z_ref, w_ref, b_ref: _short_conv(z_ref[...].astype(F32), w_ref[...], b_ref[...])
    _store_block_order(y16_scr, conv3(v_ref, wv_ref, bv_ref), n2)
    _store_block_order(gate_scr.at[0], conv3(x1_ref, w1_ref, b1_ref), n2)
    _store_block_order(gate_scr.at[1], conv3(x2_ref, w2_ref, b2_ref), n2)

    for n in range(HY_ORDER):
        last = n + 1 == HY_ORDER
        _stage1_fwd(s1_ref, y16_scr, g_scr, n2)
        for g in range(DFT_GROUPS):
            start = _packed_start(g, n2)
            if g in (0, 8):
                x = dot(tr_ref[0 if g == 0 else 1], g_scr[start:start + n2, :])
                xr, xi = x[:n2], x[n2:]
            else:
                a = g_scr[start:start + n2, :]
                b = g_scr[start + n2:start + 2 * n2, :]
                p1 = dot(tg_ref[g - 1, 0], a + b)
                xr = p1 + dot(tg_ref[g - 1, 1], b)
                xi = p1 - dot(tg_ref[g - 1, 2], a)
            xr, xi = xr.astype(BF16), xi.astype(BF16)
            base = 2 * n2 * g
            kr = k_ref[n, base:base + n2, :]
            ki = k_ref[n, base + n2:base + 2 * n2, :]
            yc_scr[base:base + n2, :] = xr * kr - xi * ki
            yc_scr[base + n2:base + 2 * n2, :] = xr * ki + xi * kr
        for g in range(DFT_GROUPS):
            base = 2 * n2 * g
            if g in (0, 8):
                parts = [(0 if g == 0 else 1,
                          dot(ur_ref[0 if g == 0 else 1], yc_scr[base:base + 2 * n2, :]))]
            else:
                a = yc_scr[base:base + n2, :]
                b = yc_scr[base + n2:base + 2 * n2, :]
                q1 = dot(ug_ref[g - 1, 0], a + b)
                parts = [(2 * g, q1 - dot(ug_ref[g - 1, 2], b)),
                         (2 * g + 1, q1 + dot(ug_ref[g - 1, 1], a))]
            for c, hp in parts:
                for k in range(n2 // DFT_J):
                    dst = DFT_COMPS * DFT_J * k + DFT_J * c
                    h_scr[dst:dst + DFT_J, :] = hp[DFT_J * k:DFT_J * (k + 1), :].astype(BF16)
        for k in range(n2 // DFT_J):
            rows = slice(step * k, step * (k + 1))
            conv = jnp.dot(s1t_ref[...], h_scr[DFT_COMPS * DFT_J * k:DFT_COMPS * DFT_J * (k + 1), :],
                           preferred_element_type=F32)
            y_new = (gate_scr[n, rows, :] * conv).astype(BF16)
            if last:
                for t1 in range(DFT_SLABS):
                    dst = n2 * t1 + DFT_J * k
                    o_ref[dst:dst + DFT_J, :] = y_new[DFT_J * t1:DFT_J * (t1 + 1), :]
            else:
                y16_scr[rows, :] = y_new


def _hyena_call(z, conv_w, conv_b, kspec, tables, to_cast, bsz, seq, width, col0):
    cb = MXU_DIM
    ncb = width // cb
    base = col0 // cb
    zspec = lambda part: pl.BlockSpec((seq, cb), lambda k, b: (b, base + part * ncb + k))
    wspec = lambda part: pl.BlockSpec((HY_SHORT, cb), lambda k, b: (0, part * ncb + k))
    bspec = lambda part: pl.BlockSpec((1, cb), lambda k, b: (0, part * ncb + k))
    cb_all = conv_b.reshape(1, -1)
    const = lambda t: (lambda k, b: (0,) * t.ndim)
    steps = ncb * bsz
    slab = lambda w: pl.BlockSpec((w.shape[0] // steps, w.shape[1]), lambda k, b: (k * bsz + b, 0))
    assert all(w.shape[0] % (steps * 2 * SUBLANES) == 0 for w in to_cast)
    return pl.pallas_call(
        functools.partial(_hyena_kernel, n_cast=len(to_cast)),
        grid=(ncb, bsz),
        in_specs=[zspec(0), zspec(1), zspec(2), wspec(0), wspec(1), wspec(2),
                  bspec(0), bspec(1), bspec(2),
                  pl.BlockSpec((HY_ORDER, kspec.shape[1], cb), lambda k, b: (0, 0, k))]
                 + [_resident(t.shape, const(t)) for t in tables] + [slab(w) for w in to_cast],
        out_specs=[pl.BlockSpec((seq, cb), lambda k, b: (b, k))] + [slab(w) for w in to_cast],
        out_shape=[jax.ShapeDtypeStruct((bsz * seq, width), BF16)]
                  + [jax.ShapeDtypeStruct(w.shape, BF16) for w in to_cast],
        scratch_shapes=[pltpu.VMEM((seq, cb), BF16),
                        pltpu.VMEM((HY_ORDER, seq, cb), F32),
                        pltpu.VMEM((2 * seq, cb), BF16),
                        pltpu.VMEM((kspec.shape[1], cb), BF16),
                        pltpu.VMEM((2 * seq, cb), BF16)],
        compiler_params=_cparams(("arbitrary", "arbitrary")),
        name="hyena_mixer",
    )(z, z, z, conv_w, conv_w, conv_w, cb_all, cb_all, cb_all, kspec, *tables, *to_cast)


def _merge_kernel(u_ref, vn_ref, yh_ref, ga_ref, gb_ref, ws_ref, bs_ref, wa_ref, wb_ref, o_ref,
                  s_scr):
    tm = u_ref.shape[0]
    gd = u_ref.shape[1] // SGU_GROUPS
    for n in range(tm // SGU_CHUNK):
        rows = slice(n * SGU_CHUNK, (n + 1) * SGU_CHUNK)
        for g in range(SGU_GROUPS):
            cols = slice(g * gd, (g + 1) * gd)
            mixed = jnp.dot(ws_ref[g], vn_ref[rows, cols], preferred_element_type=F32)
            s_scr[rows, cols] = (u_ref[rows, cols].astype(F32)
                                 * (mixed + bs_ref[:, cols])).astype(BF16)
    y_a = jnp.dot(s_scr[...], wa_ref[...], preferred_element_type=F32)
    y_b = jnp.dot(yh_ref[...], wb_ref[...], preferred_element_type=F32)
    o_ref[...] = (ga_ref[...].astype(F32) * y_a + gb_ref[...].astype(F32) * y_b).astype(o_ref.dtype)


def _merge_call(z, yh, sgu_w16, sgu_bfull, w_a, w_b, d):
    t = z.shape[0]
    wdt = w_a.shape[0]
    tm = TOKEN_ROWS
    u_blk = 2 * d // wdt
    return pl.pallas_call(
        _merge_kernel,
        grid=(t // tm,),
        in_specs=[pl.BlockSpec((tm, wdt), lambda i: (i, u_blk)),
                  pl.BlockSpec((tm, wdt), lambda i: (i, u_blk + 1)),
                  pl.BlockSpec((tm, wdt), lambda i: (i, 0)),
                  pl.BlockSpec((tm, d), lambda i: (i, 0)),
                  pl.BlockSpec((tm, d), lambda i: (i, 1)),
                  _resident(sgu_w16.shape, lambda i: (0, 0, 0)),
                  _resident(sgu_bfull.shape, lambda i: (0, 0)),
                  _resident(w_a.shape, lambda i: (0, 0)),
                  _resident(w_b.shape, lambda i: (0, 0))],
        out_specs=pl.BlockSpec((tm, d), lambda i: (i, 0)),
        out_shape=jax.ShapeDtypeStruct((t, d), BF16),
        scratch_shapes=[pltpu.VMEM((tm, wdt), BF16)],
        compiler_params=_cparams(("parallel",)),
        name="gated_merge",
    )(z, z, yh, z, z, sgu_w16, sgu_bfull, w_a, w_b)


def _oproj_kernel(m_ref, x_ref, g1_ref, w_ref, b_ref, lg_ref, lb_ref, o_ref, *, alpha, rc):
    for rows in _row_chunks(m_ref.shape[0], rc):
        mix = jnp.dot(m_ref[rows, :], w_ref[...], preferred_element_type=F32) + b_ref[...]
        o_ref[rows, :] = _layer_norm(alpha * x_ref[rows, :] + g1_ref[0] * mix,
                                     lg_ref[...], lb_ref[...])


def _oproj_call(m, xt, mod3, w_o, b_o, ln_g, ln_b, seq, alpha):
    t, d = xt.shape
    tm = TOKEN_ROWS
    per_b = seq // tm
    row = lambda i: (0, 0)
    return pl.pallas_call(
        functools.partial(_oproj_kernel, alpha=alpha, rc=EPILOGUE_ROWS),
        grid=(t // tm,),
        in_specs=[pl.BlockSpec((tm, d), lambda i: (i, 0)),
                  pl.BlockSpec((tm, d), lambda i: (i, 0)),
                  pl.BlockSpec((1, 1, d), lambda i: (i // per_b, 0, 2)),
                  _resident((d, d), row),
                  pl.BlockSpec((1, d), row), pl.BlockSpec((1, d), row), pl.BlockSpec((1, d), row)],
        out_specs=pl.BlockSpec((tm, d), lambda i: (i, 0)),
        out_shape=jax.ShapeDtypeStruct((t, d), F32),
        compiler_params=_cparams(("parallel",)),
        name="out_proj_ln",
    )(m, xt, mod3, w_o, b_o.reshape(1, d), ln_g.reshape(1, d), ln_b.reshape(1, d))


def _mlp_kernel(x_ref, sc_ref, sh_ref, g2_ref, w1_ref, b1_ref, w2_ref, b2_ref, lg_ref, lb_ref,
                o_ref, h_scr, acc_scr, *, alpha, rc):
    j = pl.program_id(1)
    last = pl.num_programs(1) - 1
    chunks = _row_chunks(x_ref.shape[0], rc)

    def partial_out(rows):
        hid = jnp.dot(h_scr[rows, :], w1_ref[...], preferred_element_type=F32) + b1_ref[...]
        hid = jnp.square(jnp.maximum(hid, 0.0)).astype(BF16)
        return jnp.dot(hid, w2_ref[...], preferred_element_type=F32)

    @pl.when(j == 0)
    def _():
        for rows in chunks:
            h_scr[rows, :] = (x_ref[rows, :] * (1.0 + sc_ref[0]) + sh_ref[0]).astype(BF16)
            acc_scr[rows, :] = partial_out(rows)

    @pl.when(jnp.logical_and(j > 0, j < last))
    def _():
        acc_scr[...] += partial_out(slice(None))

    @pl.when(j == last)
    def _():
        for rows in chunks:
            f = acc_scr[rows, :] + partial_out(rows) + b2_ref[...]
            o_ref[rows, :] = _layer_norm(alpha * x_ref[rows, :] + g2_ref[0] * f,
                                         lg_ref[...], lb_ref[...])


def _mlp_call(x1, mod3, w_m1, b_m1, w_m2, b_m2, ln_g, ln_b, seq, alpha):
    t, d = x1.shape
    dff = w_m1.shape[1]
    tm = TOKEN_ROWS
    tf = MLP_HIDDEN
    per_b = seq // tm
    row = lambda i, j: (0, 0)
    return pl.pallas_call(
        functools.partial(_mlp_kernel, alpha=alpha, rc=EPILOGUE_ROWS),
        grid=(t // tm, dff // tf),
        in_specs=[pl.BlockSpec((tm, d), lambda i, j: (i, 0)),
                  pl.BlockSpec((1, 1, d), lambda i, j: (i // per_b, 0, 4)),
                  pl.BlockSpec((1, 1, d), lambda i, j: (i // per_b, 0, 3)),
                  pl.BlockSpec((1, 1, d), lambda i, j: (i // per_b, 0, 5)),
                  pl.BlockSpec((d, tf), lambda i, j: (0, j)),
                  pl.BlockSpec((1, tf), lambda i, j: (0, j)),
                  pl.BlockSpec((tf, d), lambda i, j: (j, 0)),
                  pl.BlockSpec((1, d), row), pl.BlockSpec((1, d), row), pl.BlockSpec((1, d), row)],
        out_specs=pl.BlockSpec((tm, d), lambda i, j: (i, 0)),
        out_shape=jax.ShapeDtypeStruct((t, d), F32),
        scratch_shapes=[pltpu.VMEM((tm, d), BF16), pltpu.VMEM((tm, d), F32)],
        compiler_params=_cparams(("parallel", "arbitrary")),
        name="mlp_ln",
    )(x1, mod3, mod3, mod3, w_m1, b_m1.reshape(1, dff), w_m2, b_m2.reshape(1, d),
      ln_g.reshape(1, d), ln_b.reshape(1, d))


def _filter_features(seq):
    bands_n = (HY_EMB - 1) // 2
    t = np.linspace(0.0, 1.0, seq)[:, None]
    omega = 2.0 * math.pi * np.arange(seq)[:, None] / seq
    bands = np.linspace(1e-4, bands_n - 1, bands_n)[None, :]
    feats = np.concatenate([t, np.cos(bands * omega), -np.sin(bands * omega)], axis=-1)
    return jnp.asarray(t, F32), jnp.asarray(feats, F32)


def _decay_rates(width):
    min_decay = math.log(HY_DECAY_TARGET) / HY_SLOW_DECAY
    max_decay = math.log(HY_DECAY_TARGET) / HY_FAST_DECAY
    return jnp.asarray(np.abs(np.linspace(min_decay, max_decay, width))[None, :], F32)


def kernel(x, c, w_ada, b_ada, w_in, b_in, sgu_ln_g, sgu_ln_b, sgu_w, sgu_b, hy_conv_w, hy_conv_b, hy_w1, hy_b1, hy_w2, hy_b2, hy_freq, hy_w3, hy_skip, w_branch_a, w_branch_b, w_o, b_o, ln1_g, ln1_b, w_m1, b_m1, w_m2, b_m2, ln2_g, ln2_b):
    bsz, seq, d = x.shape
    depth = w_ada.shape[0]
    alpha = (2.0 * depth) ** 0.25
    sgu_width = w_branch_a.shape[1]
    hy_width = w_branch_b.shape[1]
    assert seq % SGU_CHUNK == 0 and sgu_w.shape[-1] == SGU_CHUNK

    assert seq % (8 * MXU_DIM) == 0
    (t_s1, t_real, t_cplx, t_gauss), inv_tables = _dft_stage_tables(seq)
    tcol, feats = _filter_features(seq)
    deltas = _decay_rates(hy_width)

    xt = x.reshape(bsz * seq, d)
    for l in range(depth):
        mod3 = _mod_call(c, w_ada[l], b_ada[l]).reshape(bsz, 1, N_MOD * d)

        kspec, w_in_16 = _filter_spec_call(feats, hy_w1[l], hy_b1[l], hy_w2[l], hy_b2[l], hy_freq[l],
                                           hy_w3[l], tcol, deltas, hy_skip[l],
                                           t_s1, t_real, t_cplx, w_in[l], hy_width)

        z = _inproj_call(xt, mod3, w_in_16, b_in[l], sgu_ln_g[l], sgu_ln_b[l],
                         seq, sgu_width, hy_width)
        later_weights = (w_branch_a[l], w_branch_b[l], w_o[l], w_m1[l], w_m2[l])
        yh, w_a_16, w_b_16, w_o_16, w_m1_16, w_m2_16 = _hyena_call(
            z, hy_conv_w[l], hy_conv_b[l], kspec, (t_s1, t_real, t_gauss) + inv_tables,
            later_weights, bsz, seq, hy_width, 2 * d + 2 * sgu_width)

        gd = sgu_width // SGU_GROUPS
        bs_full = jnp.repeat(sgu_b[l].T, gd, axis=1)
        m = _merge_call(z, yh, sgu_w[l].astype(BF16), bs_full, w_a_16, w_b_16, d)

        x1 = _oproj_call(m, xt, mod3, w_o_16, b_o[l], ln1_g[l], ln1_b[l], seq, alpha)
        xt = _mlp_call(x1, mod3, w_m1_16, b_m1[l], w_m2_16, b_m2[l], ln2_g[l], ln2_b[l], seq, alpha)
    return xt.reshape(bsz, seq, d)
```
